```python
import jax, jax.numpy as jnp
from jax import lax
import numpy as np

D_MODEL = 1024
BATCH = 8
SEQ = 2048
DEPTH = 1
DEC_BATCH = 128
DEC_SEQ = 1
PAST_LEN = 16384
PAGE_SIZE = 128

EXPAND = 2
D_MIX = EXPAND * D_MODEL
D_POOL = D_MIX // 2
D_RET = D_MIX - D_POOL
POOL_WINDOWS = (2, 4, 8, 16)
N_POOL_GROUPS = len(POOL_WINDOWS)
POOL_GROUP = D_POOL // N_POOL_GROUPS
POOL_BUF = max(POOL_WINDOWS) - 1
N_HEADS = 8
HEAD_DIM = D_RET // N_HEADS
D_IN_PROJ = 2 * D_POOL + 4 * D_RET
N_META = 16
CHUNK = 128
ROPE_BASE = 10000.0
EPS = 1e-6

kernel_name = "hymba_pool_retention_step"


def rms_norm(x, w):
    xf = x.astype(jnp.float32)
    y = xf * lax.rsqrt(jnp.mean(xf * xf, axis=-1, keepdims=True) + EPS)
    return (y * w.astype(jnp.float32)).astype(x.dtype)


def retention_log_decay():
    return jnp.log(1.0 - 2.0 ** (-5.0 - jnp.arange(N_HEADS, dtype=jnp.float32)))


def rotary(x, pos):
    half = HEAD_DIM // 2
    inv = ROPE_BASE ** (-jnp.arange(half, dtype=jnp.float32) / half)
    ang = pos.astype(jnp.float32)[:, None] * inv[None, :]
    cos, sin = jnp.cos(ang), jnp.sin(ang)
    x1, x2 = x[..., :half], x[..., half:]
    return jnp.concatenate([x1 * cos - x2 * sin, x1 * sin + x2 * cos], axis=-1)


def split_heads(t):
    b, L, _ = t.shape
    return t.reshape(b, L, N_HEADS, HEAD_DIM).transpose(0, 2, 1, 3)


def layer_inputs(h, norm_w, w_in, pos):
    hn = rms_norm(h, norm_w)
    proj = hn @ w_in
    o1 = D_POOL
    o2 = 2 * D_POOL
    o3 = o2 + D_RET
    o4 = o3 + D_RET
    o5 = o4 + D_RET
    u, gp, q, k, v, gr = jnp.split(proj, [o1, o2, o3, o4, o5], axis=-1)
    q = rotary(split_heads(q).astype(jnp.float32), pos)
    k = rotary(split_heads(k).astype(jnp.float32), pos) * (HEAD_DIM ** -0.5)
    v = split_heads(v).astype(jnp.float32)
    return u, gp, q, k, v, gr


def pool_mixer(u, buf, pos, w_grp, scale):
    b, L, _ = u.shape
    ext = jnp.concatenate([buf, u], axis=1).astype(jnp.float32)
    c = jnp.concatenate([jnp.zeros((b, 1, D_POOL), jnp.float32), jnp.cumsum(ext, axis=1)], axis=1)
    end = c[:, POOL_BUF + 1:]
    parts = []
    for g, w in enumerate(POOL_WINDOWS):
        sl = slice(g * POOL_GROUP, (g + 1) * POOL_GROUP)
        start = c[:, POOL_BUF + 1 - w: POOL_BUF + 1 - w + L, sl]
        cnt = jnp.minimum(pos + 1, w).astype(jnp.float32)[None, :, None]
        parts.append((end[..., sl] - start) / cnt)
    pooled = jnp.concatenate(parts, axis=-1) - u.astype(jnp.float32)
    pooled = pooled.reshape(b, L, N_POOL_GROUPS, POOL_GROUP)
    mixed = jnp.einsum("blgc,gcd->blgd", pooled, w_grp.astype(jnp.float32)).reshape(b, L, D_POOL)
    mixed = mixed * scale.astype(jnp.float32)
    return mixed.astype(u.dtype), ext[:, -POOL_BUF:].astype(u.dtype)


def retention_block(S, q, k, v, log_g):
    L = q.shape[2]
    idx = jnp.arange(L, dtype=jnp.float32)
    diff = idx[:, None] - idx[None, :]
    dmask = jnp.where(diff >= 0, jnp.exp(log_g[:, None, None] * jnp.maximum(diff, 0.0)), 0.0)
    scores = jnp.einsum("bhld,bhmd->bhlm", q, k) * dmask[None]
    inner = jnp.einsum("bhlm,bhmv->bhlv", scores, v)
    q_dec = q * jnp.exp(log_g[:, None] * (idx[None, :] + 1.0))[None, :, :, None]
    cross = jnp.einsum("bhld,bhdv->bhlv", q_dec, S)
    k_dec = k * jnp.exp(log_g[:, None] * (L - 1.0 - idx[None, :]))[None, :, :, None]
    S_new = jnp.exp(log_g * L)[None, :, None, None] * S + jnp.einsum("bhld,bhlv->bhdv", k_dec, v)
    return S_new, inner + cross


def retention_prompt(q, k, v, log_g):
    b, h, L, d = q.shape
    S0 = jnp.zeros((b, h, d, d), jnp.float32)
    S, o_meta = retention_block(S0, q[:, :, :N_META], k[:, :, :N_META], v[:, :, :N_META], log_g)
    n_chunks = (L - N_META) // CHUNK

    def to_chunks(t):
        return t[:, :, N_META:].reshape(b, h, n_chunks, CHUNK, d).transpose(2, 0, 1, 3, 4)

    def step(s, xs):
        return retention_block(s, xs[0], xs[1], xs[2], log_g)

    S, o_chunks = lax.scan(step, S, (to_chunks(q), to_chunks(k), to_chunks(v)))
    o_rest = o_chunks.transpose(1, 2, 0, 3, 4).reshape(b, h, n_chunks * CHUNK, d)
    return S, jnp.concatenate([o_meta, o_rest], axis=2)


def merge_branches(pool_o, gp, ret_o, gr, ret_norm_w, w_out):
    b, h, L, d = ret_o.shape
    rn = ret_o * lax.rsqrt(jnp.mean(ret_o * ret_o, axis=-1, keepdims=True) + EPS)
    rn = rn.transpose(0, 2, 1, 3).reshape(b, L, D_RET) * ret_norm_w.astype(jnp.float32)
    ret_y = (rn * jax.nn.silu(gr.astype(jnp.float32))).astype(pool_o.dtype)
    pool_y = pool_o * jax.nn.silu(gp)
    return jnp.concatenate([pool_y, ret_y], axis=-1) @ w_out


def setup_inputs(seed: int = 0) -> dict:
    key = jax.random.key(seed)
    ks = jax.random.split(key, 12)
    f32 = jnp.float32
    nrm = jax.random.normal
    return {
        "x_prompt": nrm(ks[0], (BATCH, SEQ, D_MODEL), f32),
        "x_sample": nrm(ks[1], (DEC_BATCH, DEC_SEQ, D_MODEL), f32),
        "state_ret": nrm(ks[2], (DEPTH, DEC_BATCH, N_HEADS, HEAD_DIM, HEAD_DIM), f32),
        "state_pool": nrm(ks[3], (DEPTH, DEC_BATCH, POOL_BUF, D_POOL), f32),
        "meta_tokens": nrm(ks[4], (N_META, D_MODEL), f32),
        "norm_w": 1.0 + 0.1 * nrm(ks[5], (DEPTH, D_MODEL), f32),
        "w_in": nrm(ks[6], (DEPTH, D_MODEL, D_IN_PROJ), f32) * D_MODEL ** -0.5,
        "w_pool": nrm(ks[7], (DEPTH, N_POOL_GROUPS, POOL_GROUP, POOL_GROUP), f32) * POOL_GROUP ** -0.5,
        "pool_scale": 1.0 + 0.1 * nrm(ks[8], (DEPTH, D_POOL), f32),
        "ret_norm_w": 1.0 + 0.1 * nrm(ks[9], (DEPTH, D_RET), f32),
        "w_out": nrm(ks[10], (DEPTH, D_MIX, D_MODEL), f32) * D_MIX ** -0.5,
        "final_norm_w": 1.0 + 0.1 * nrm(ks[11], (D_MODEL,), f32),
    }


def reference(x_prompt, x_sample, state_ret, state_pool, meta_tokens, norm_w, w_in, w_pool,
              pool_scale, ret_norm_w, w_out, final_norm_w):
    log_g = retention_log_decay()
    b_p = x_prompt.shape[0]
    meta = jnp.broadcast_to(meta_tokens.astype(x_prompt.dtype)[None], (b_p, N_META, D_MODEL))
    h_p = jnp.concatenate([meta, x_prompt], axis=1)
    pos_p = jnp.arange(h_p.shape[1], dtype=jnp.int32)
    h_s = x_sample
    pos_s = PAST_LEN + jnp.arange(x_sample.shape[1], dtype=jnp.int32)

    ret_p, ret_s, buf_p_list, buf_s_list = [], [], [], []
    for l in range(DEPTH):
        u, gp, q, k, v, gr = layer_inputs(h_p, norm_w[l], w_in[l], pos_p)
        buf0 = jnp.zeros((b_p, POOL_BUF, D_POOL), u.dtype)
        pool_o, buf_p = pool_mixer(u, buf0, pos_p, w_pool[l], pool_scale[l])
        S_p, ret_o = retention_prompt(q, k, v, log_g)
        h_p = h_p + merge_branches(pool_o, gp, ret_o, gr, ret_norm_w[l], w_out[l])
        u, gp, q, k, v, gr = layer_inputs(h_s, norm_w[l], w_in[l], pos_s)
        pool_o, buf_s = pool_mixer(u, state_pool[l].astype(u.dtype), pos_s, w_pool[l], pool_scale[l])
        S_s, ret_o = retention_block(state_ret[l].astype(jnp.float32), q, k, v, log_g)
        h_s = h_s + merge_branches(pool_o, gp, ret_o, gr, ret_norm_w[l], w_out[l])
        ret_p.append(S_p)
        ret_s.append(S_s)
        buf_p_list.append(buf_p)
        buf_s_list.append(buf_s)

    y_prompt = rms_norm(h_p, final_norm_w)[:, N_META:]
    y_sample = rms_norm(h_s, final_norm_w)
    ret_state_prompt = jnp.stack(ret_p, axis=0).astype(state_ret.dtype)
    ret_state_sample = jnp.stack(ret_s, axis=0).astype(state_ret.dtype)
    pool_buf_prompt = jnp.stack(buf_p_list, axis=0).astype(state_pool.dtype)
    pool_buf_sample = jnp.stack(buf_s_list, axis=0).astype(state_pool.dtype)
    return (y_prompt, y_sample, ret_state_prompt, ret_state_sample, pool_buf_prompt, pool_buf_sample)
```

```python
import functools
import math

import jax
import jax.numpy as jnp
from jax import lax
from jax.experimental import pallas as pl
from jax.experimental.pallas import tpu as pltpu

D_MODEL = 1024
D_POOL = 1024
D_RET = 1024
D_MIX = D_POOL + D_RET
POOL_WINDOWS = (2, 4, 8, 16)
POOL_GROUP = D_POOL // len(POOL_WINDOWS)
POOL_BUF = max(POOL_WINDOWS) - 1
N_HEADS = 8
HEAD_DIM = D_RET // N_HEADS
D_IN_PROJ = 2 * D_POOL + 4 * D_RET
N_META = 16
PAST_LEN = 16384
CHUNK = 128
ROPE_BASE = 10000.0
EPS = 1e-6
K_SCALE = HEAD_DIM ** -0.5

OFF_U, OFF_GP, OFF_Q, OFF_K, OFF_V, OFF_GR = (i * 1024 for i in range(6))

LOG_DECAY = tuple(math.log(1.0 - 2.0 ** (-5.0 - h)) for h in range(N_HEADS))

HIST = 16
SAMPLE_BLOCK = 8

VMEM_LIMIT_BYTES = 56 * 1024 * 1024

F32 = jnp.float32
BF16 = jnp.bfloat16


def _rms(x, w):
    return x * lax.rsqrt(jnp.mean(x * x, axis=-1, keepdims=True) + EPS) * w


def _silu(x):
    return x * (1.0 / (1.0 + jnp.exp(-x)))


def _dot(a, b):
    return jnp.dot(a, b, preferred_element_type=F32)


def _rotary(x, cosf, sinf):
    return x * cosf + pltpu.roll(x, HEAD_DIM // 2, 1) * sinf


def _head(h):
    return slice(h * HEAD_DIM, (h + 1) * HEAD_DIM)


def _meta_kernel(meta_ref, cos_ref, sin_ref, normw_ref, win_ref, s_ref, u_ref, kd_ref, v_ref):
    hb = _rms(meta_ref[...], normw_ref[...]).astype(BF16)
    u_ref[...] = _dot(hb, win_ref[:, OFF_U:OFF_U + D_POOL])
    k = _dot(hb, win_ref[:, OFF_K:OFF_K + D_RET])
    v = _dot(hb, win_ref[:, OFF_V:OFF_V + D_RET])
    cosf, sinf = cos_ref[...], sin_ref[...]
    row = lax.broadcasted_iota(jnp.int32, (N_META, HEAD_DIM), 0).astype(F32)
    kd_ref[...] = jnp.zeros_like(kd_ref)
    v_ref[...] = jnp.zeros_like(v_ref)
    for h in range(N_HEADS):
        kr = _rotary(k[:, _head(h)], cosf, sinf) * K_SCALE
        kd_ref[0:N_META, _head(h)] = kr * jnp.exp(LOG_DECAY[h] * (N_META - 1.0 - row))
        v_ref[0:N_META, _head(h)] = v[:, _head(h)]
    for h in range(N_HEADS):
        s_ref[h] = lax.dot_general(
            kd_ref[:, _head(h)].astype(BF16), v_ref[:, _head(h)].astype(BF16),
            (((0,), (0,)), ((), ())), preferred_element_type=F32)


def _meta_call(meta, cosf, sinf, normw, win_bf):
    return pl.pallas_call(
        _meta_kernel,
        out_shape=(jax.ShapeDtypeStruct((N_HEADS, HEAD_DIM, HEAD_DIM), F32),
                   jax.ShapeDtypeStruct((N_META, D_POOL), F32)),
        scratch_shapes=[pltpu.VMEM((CHUNK, D_RET), F32), pltpu.VMEM((CHUNK, D_RET), F32)],
        compiler_params=pltpu.CompilerParams(vmem_limit_bytes=VMEM_LIMIT_BYTES),
        name="meta",
    )(meta, cosf, sinf, normw, win_bf)


def _prompt_kernel(x_ref, cos_ref, sin_ref, smeta_ref, umeta_ref, normw_ref, pscale_ref, rnw_ref,
                   fnw_ref, win_ref, wpool_ref, wout_ref,
                   y_ref, s_ref, pbuf_ref,
                   ext_ref, proj_ref, mix_ref):
    c = pl.program_id(1)

    @pl.when(c == 0)
    def _():
        s_ref[...] = smeta_ref[...]
        ext_ref[0:HIST, :] = umeta_ref[...]

    x = x_ref[...]
    hb = _rms(x, normw_ref[...]).astype(BF16)
    ext_ref[HIST:HIST + CHUNK, :] = _dot(hb, win_ref[:, OFF_U:OFF_U + D_POOL])
    proj_ref[...] = _dot(hb, win_ref[:, OFF_GP:D_IN_PROJ])

    for g, w in enumerate(POOL_WINDOWS):
        cols = slice(g * POOL_GROUP, (g + 1) * POOL_GROUP)
        u = ext_ref[HIST:HIST + CHUNK, cols]
        acc = u
        for j in range(1, w):
            acc = acc + ext_ref[HIST - j:HIST - j + CHUNK, cols]
        pooled = acc / float(w) - u
        mixed = _dot(pooled.astype(BF16), wpool_ref[g]) * pscale_ref[:, cols]
        mix_ref[:, cols] = (mixed * _silu(proj_ref[:, cols])).astype(BF16)

    cosf, sinf = cos_ref[...], sin_ref[...]
    li = lax.broadcasted_iota(jnp.int32, (CHUNK, CHUNK), 0).astype(F32)
    mi = lax.broadcasted_iota(jnp.int32, (CHUNK, CHUNK), 1).astype(F32)
    diff = li - mi
    for h in range(N_HEADS):
        lg = LOG_DECAY[h]
        hq = slice(OFF_Q - OFF_GP + h * HEAD_DIM, OFF_Q - OFF_GP + (h + 1) * HEAD_DIM)
        hk = slice(OFF_K - OFF_GP + h * HEAD_DIM, OFF_K - OFF_GP + (h + 1) * HEAD_DIM)
        hv = slice(OFF_V - OFF_GP + h * HEAD_DIM, OFF_V - OFF_GP + (h + 1) * HEAD_DIM)
        hg = slice(OFF_GR - OFF_GP + h * HEAD_DIM, OFF_GR - OFF_GP + (h + 1) * HEAD_DIM)
        qr = _rotary(proj_ref[:, hq], cosf, sinf)
        kr = _rotary(proj_ref[:, hk], cosf, sinf) * K_SCALE
        vb = proj_ref[:, hv].astype(BF16)
        state = s_ref[h]
        dmask = jnp.where(diff >= 0.0, jnp.exp(lg * jnp.maximum(diff, 0.0)), 0.0)
        scores = lax.dot_general(qr.astype(BF16), kr.astype(BF16), (((1,), (1,)), ((), ())),
                                 preferred_element_type=F32) * dmask
        q_dec = qr * jnp.exp(lg * (li + 1.0))
        o = _dot(scores.astype(BF16), vb) + _dot(q_dec.astype(BF16), state.astype(BF16))
        k_dec = kr * jnp.exp(lg * (CHUNK - 1.0 - li))
        s_ref[h] = math.exp(lg * CHUNK) * state + lax.dot_general(
            k_dec.astype(BF16), vb, (((0,), (0,)), ((), ())), preferred_element_type=F32)
        rn = o * lax.rsqrt(jnp.mean(o * o, axis=-1, keepdims=True) + EPS)
        ret = rn * rnw_ref[:, _head(h)] * _silu(proj_ref[:, hg])
        mix_ref[:, D_POOL + h * HEAD_DIM:D_POOL + (h + 1) * HEAD_DIM] = ret.astype(BF16)

    y_ref[...] = _rms(x + _dot(mix_ref[...], wout_ref[...]), fnw_ref[...])

    @pl.when(c == pl.num_programs(1) - 1)
    def _():
        pbuf_ref[...] = ext_ref[HIST + CHUNK - POOL_BUF:HIST + CHUNK, :]

    ext_ref[0:HIST, :] = ext_ref[CHUNK:CHUNK + HIST, :]


def _const_spec(shape):
    zeros = (0,) * len(shape)
    return pl.BlockSpec(shape, lambda b, c: zeros, pipeline_mode=pl.Buffered(1))


def _prompt_call(x, cosf, sinf, smeta, umeta, normw, pscale, rnw, fnw, win_bf, wpool_bf, wout_bf):
    batch, seq, _ = x.shape
    n_chunks = seq // CHUNK
    return pl.pallas_call(
        _prompt_kernel,
        grid=(batch, n_chunks),
        in_specs=[
            pl.BlockSpec((None, CHUNK, D_MODEL), lambda b, c: (b, c, 0)),
            pl.BlockSpec((CHUNK, HEAD_DIM), lambda b, c: (c, 0)),
            pl.BlockSpec((CHUNK, HEAD_DIM), lambda b, c: (c, 0)),
            _const_spec((N_HEADS, HEAD_DIM, HEAD_DIM)),
            _const_spec((N_META, D_POOL)),
            _const_spec((1, D_MODEL)),
            _const_spec((1, D_POOL)),
            _const_spec((1, D_RET)),
            _const_spec((1, D_MODEL)),
            _const_spec((D_MODEL, D_IN_PROJ)),
            _const_spec((len(POOL_WINDOWS), POOL_GROUP, POOL_GROUP)),
            _const_spec((D_MIX, D_MODEL)),
        ],
        out_specs=[
            pl.BlockSpec((None, CHUNK, D_MODEL), lambda b, c: (b, c, 0)),
            pl.BlockSpec((None, N_HEADS, HEAD_DIM, HEAD_DIM), lambda b, c: (b, 0, 0, 0)),
            pl.BlockSpec((None, POOL_BUF, D_POOL), lambda b, c: (b, 0, 0)),
        ],
        out_shape=(jax.ShapeDtypeStruct((batch, seq, D_MODEL), F32),
                   jax.ShapeDtypeStruct((batch, N_HEADS, HEAD_DIM, HEAD_DIM), F32),
                   jax.ShapeDtypeStruct((batch, POOL_BUF, D_POOL), F32)),
        scratch_shapes=[pltpu.VMEM((HIST + CHUNK, D_POOL), F32),
                        pltpu.VMEM((CHUNK, D_IN_PROJ - OFF_GP), F32),
                        pltpu.VMEM((CHUNK, D_MIX), BF16)],
        compiler_params=pltpu.CompilerParams(
            dimension_semantics=("arbitrary", "arbitrary"), vmem_limit_bytes=VMEM_LIMIT_BYTES),
        name="prompt",
    )(x, cosf, sinf, smeta, umeta, normw, pscale, rnw, fnw, win_bf, wpool_bf, wout_bf)


def _sample_kernel(x_ref, cos_ref, sin_ref, sin_state_ref, pin_ref, normw_ref, pscale_ref, rnw_ref,
                   fnw_ref, win_ref, wpool_ref, wout_ref,
                   y_ref, sout_ref, pout_ref,
                   proj_ref, pooled_ref, o_ref, pblk_ref, oblk_ref):
    i = pl.program_id(0)
    nb = SAMPLE_BLOCK

    @pl.when(i == 0)
    def _():
        hb = _rms(x_ref[...], normw_ref[...]).astype(BF16)
        proj_ref[...] = _dot(hb, win_ref[...])

    base = pl.multiple_of(i * nb, nb)
    rows = pl.ds(base, nb)
    u8 = proj_ref[rows, OFF_U:OFF_U + D_POOL]

    for j in range(nb):
        for g, w in enumerate(POOL_WINDOWS):
            cols = slice(g * POOL_GROUP, (g + 1) * POOL_GROUP)
            uj = u8[j:j + 1, cols]
            acc = uj + jnp.sum(pin_ref[j, POOL_BUF - (w - 1):POOL_BUF, cols], axis=0, keepdims=True)
            pblk_ref[j:j + 1, cols] = acc / float(w) - uj
        pout_ref[j, 0:POOL_BUF - 1, :] = pin_ref[j, 1:POOL_BUF, :]
        pout_ref[j, POOL_BUF - 1:POOL_BUF, :] = u8[j:j + 1, :]
    pooled_ref[rows, :] = pblk_ref[...]

    cosf, sinf = cos_ref[...], sin_ref[...]
    for h in range(N_HEADS):
        g1 = math.exp(LOG_DECAY[h])
        q8 = _rotary(proj_ref[rows, OFF_Q + h * HEAD_DIM:OFF_Q + (h + 1) * HEAD_DIM], cosf, sinf)
        k8 = _rotary(proj_ref[rows, OFF_K + h * HEAD_DIM:OFF_K + (h + 1) * HEAD_DIM], cosf, sinf) * K_SCALE
        v8 = proj_ref[rows, OFF_V + h * HEAD_DIM:OFF_V + (h + 1) * HEAD_DIM]
        inner8 = jnp.sum(q8 * k8, axis=-1, keepdims=True) * v8
        qt = (q8 * g1).T
        kt = k8.T
        for j in range(nb):
            state = sin_state_ref[j, h]
            cross = jnp.sum(qt[:, j:j + 1] * state, axis=0, keepdims=True)
            oblk_ref[j:j + 1, _head(h)] = inner8[j:j + 1, :] + cross
            sout_ref[j, h] = g1 * state + kt[:, j:j + 1] * v8[j:j + 1, :]
    o_ref[rows, :] = oblk_ref[...]

    @pl.when(i == pl.num_programs(0) - 1)
    def _():
        pool_parts = []
        for g in range(len(POOL_WINDOWS)):
            cols = slice(g * POOL_GROUP, (g + 1) * POOL_GROUP)
            mixed = _dot(pooled_ref[:, cols].astype(BF16), wpool_ref[g]) * pscale_ref[:, cols]
            pool_parts.append((mixed * _silu(proj_ref[:, OFF_GP + g * POOL_GROUP:OFF_GP + (g + 1) * POOL_GROUP])).astype(BF16))
        ret_parts = []
        for h in range(N_HEADS):
            o = o_ref[:, _head(h)]
            rn = o * lax.rsqrt(jnp.mean(o * o, axis=-1, keepdims=True) + EPS)
            gr = proj_ref[:, OFF_GR + h * HEAD_DIM:OFF_GR + (h + 1) * HEAD_DIM]
            ret_parts.append((rn * rnw_ref[:, _head(h)] * _silu(gr)).astype(BF16))
        mix = jnp.concatenate(pool_parts + ret_parts, axis=-1)
        y_ref[...] = _rms(x_ref[...] + _dot(mix, wout_ref[...]), fnw_ref[...])


def _sample_call(x, cosf, sinf, state, pool, normw, pscale, rnw, fnw, win_bf, wpool_bf, wout_bf):
    n = x.shape[0]
    nb = SAMPLE_BLOCK

    def const(shape):
        zeros = (0,) * len(shape)
        return pl.BlockSpec(shape, lambda i: zeros, pipeline_mode=pl.Buffered(1))

    return pl.pallas_call(
        _sample_kernel,
        grid=(n // nb,),
        in_specs=[
            const((n, D_MODEL)),
            const((1, HEAD_DIM)),
            const((1, HEAD_DIM)),
            pl.BlockSpec((nb, N_HEADS, HEAD_DIM, HEAD_DIM), lambda i: (i, 0, 0, 0)),
            pl.BlockSpec((nb, POOL_BUF, D_POOL), lambda i: (i, 0, 0)),
            const((1, D_MODEL)),
            const((1, D_POOL)),
            const((1, D_RET)),
            const((1, D_MODEL)),
            const((D_MODEL, D_IN_PROJ)),
            const((len(POOL_WINDOWS), POOL_GROUP, POOL_GROUP)),
            const((D_MIX, D_MODEL)),
        ],
        out_specs=[
            pl.BlockSpec((n, D_MODEL), lambda i: (0, 0)),
            pl.BlockSpec((nb, N_HEADS, HEAD_DIM, HEAD_DIM), lambda i: (i, 0, 0, 0)),
            pl.BlockSpec((nb, POOL_BUF, D_POOL), lambda i: (i, 0, 0)),
        ],
        out_shape=(jax.ShapeDtypeStruct((n, D_MODEL), F32),
                   jax.ShapeDtypeStruct((n, N_HEADS, HEAD_DIM, HEAD_DIM), F32),
                   jax.ShapeDtypeStruct((n, POOL_BUF, D_POOL), F32)),
        scratch_shapes=[pltpu.VMEM((n, D_IN_PROJ), F32),
                        pltpu.VMEM((n, D_POOL), F32),
                        pltpu.VMEM((n, D_RET), F32),
                        pltpu.VMEM((nb, D_POOL), F32),
                        pltpu.VMEM((nb, D_RET), F32)],
        compiler_params=pltpu.CompilerParams(
            dimension_semantics=("arbitrary",), vmem_limit_bytes=VMEM_LIMIT_BYTES),
        name="sample",
    )(x, cosf, sinf, state, pool, normw, pscale, rnw, fnw, win_bf, wpool_bf, wout_bf)


def _rotary_tables(pos):
    half = HEAD_DIM // 2
    inv = ROPE_BASE ** (-jnp.arange(half, dtype=F32) / half)
    ang = pos.astype(F32)[:, None] * inv[None, :]
    cos, sin = jnp.cos(ang), jnp.sin(ang)
    return jnp.concatenate([cos, cos], axis=-1), jnp.concatenate([-sin, sin], axis=-1)


def kernel(x_prompt, x_sample, state_ret, state_pool, meta_tokens, norm_w, w_in, w_pool,
           pool_scale, ret_norm_w, w_out, final_norm_w):
    assert norm_w.shape[0] == 1, "single-layer stack"
    seq = x_prompt.shape[1]
    win_bf = w_in[0].astype(BF16)
    wpool_bf = w_pool[0].astype(BF16)
    wout_bf = w_out[0].astype(BF16)
    normw, pscale, rnw = norm_w, pool_scale, ret_norm_w
    fnw = final_norm_w[None, :]

    cos_p, sin_p = _rotary_tables(jnp.arange(N_META + seq, dtype=jnp.int32))
    cos_s, sin_s = _rotary_tables(PAST_LEN + jnp.arange(1, dtype=jnp.int32))

    smeta, umeta = _meta_call(meta_tokens.astype(x_prompt.dtype), cos_p[:N_META], sin_p[:N_META],
                              normw, win_bf)
    y_p, s_p, buf_p = _prompt_call(x_prompt, cos_p[N_META:], sin_p[N_META:], smeta, umeta,
                                   normw, pscale, rnw, fnw, win_bf, wpool_bf, wout_bf)
    y_s, s_s, buf_s = _sample_call(x_sample[:, 0, :], cos_s, sin_s, state_ret[0], state_pool[0],
                                   normw, pscale, rnw, fnw, win_bf, wpool_bf, wout_bf)
    return (y_p, y_s[:, None, :], s_p[None], s_s[None], buf_p[None], buf_s[None])
```

```python
import math

import jax
import jax.numpy as jnp
from jax import lax
from jax.experimental import pallas as pl
from jax.experimental.pallas import tpu as pltpu

D_MODEL = 1024
D_POOL = 1024
D_RET = 1024
D_MIX = D_POOL + D_RET
POOL_WINDOWS = (2, 4, 8, 16)
POOL_GROUP = D_POOL // len(POOL_WINDOWS)
POOL_BUF = max(POOL_WINDOWS) - 1
N_HEADS = 8
HEAD_DIM = D_RET // N_HEADS
D_IN_PROJ = 2 * D_POOL + 4 * D_RET
N_META = 16
PAST_LEN = 16384
CHUNK = 128
ROPE_BASE = 10000.0
EPS = 1e-6
K_SCALE = HEAD_DIM ** -0.5

OFF_U, OFF_GP, OFF_Q, OFF_K, OFF_V, OFF_GR = (i * 1024 for i in range(6))

LOG_DECAY = tuple(math.log(1.0 - 2.0 ** (-5.0 - h)) for h in range(N_HEADS))

TILE_N = 256
PROMPT_BLOCK = 2
HIST = 16
SAMPLE_BLOCK = 8

VMEM_LIMIT_BYTES = 56 * 1024 * 1024

F32 = jnp.float32
BF16 = jnp.bfloat16


def _rms(x, w):
    return x * lax.rsqrt(jnp.mean(x * x, axis=-1, keepdims=True) + EPS) * w


def _silu(x):
    return x * (1.0 / (1.0 + jnp.exp(-x)))


def _dot(a, b):
    return jnp.dot(a, b, preferred_element_type=F32)


def _rotary(x, cosf, sinf):
    return x * cosf + pltpu.roll(x, HEAD_DIM // 2, 1) * sinf


def _head(h):
    return slice(h * HEAD_DIM, (h + 1) * HEAD_DIM)


def _meta_kernel(meta_ref, cos_ref, sin_ref, normw_ref, win_ref, s_ref, u_ref, kd_ref, v_ref):
    hb = _rms(meta_ref[...], normw_ref[...]).astype(BF16)
    u_ref[...] = _dot(hb, win_ref[:, OFF_U:OFF_U + D_POOL])
    k = _dot(hb, win_ref[:, OFF_K:OFF_K + D_RET])
    v = _dot(hb, win_ref[:, OFF_V:OFF_V + D_RET])
    cosf, sinf = cos_ref[...], sin_ref[...]
    row = lax.broadcasted_iota(jnp.int32, (N_META, HEAD_DIM), 0).astype(F32)
    kd_ref[...] = jnp.zeros_like(kd_ref)
    v_ref[...] = jnp.zeros_like(v_ref)
    for h in range(N_HEADS):
        kr = _rotary(k[:, _head(h)], cosf, sinf) * K_SCALE
        kd_ref[0:N_META, _head(h)] = kr * jnp.exp(LOG_DECAY[h] * (N_META - 1.0 - row))
        v_ref[0:N_META, _head(h)] = v[:, _head(h)]
    for h in range(N_HEADS):
        s_ref[h] = lax.dot_general(
            kd_ref[:, _head(h)].astype(BF16), v_ref[:, _head(h)].astype(BF16),
            (((0,), (0,)), ((), ())), preferred_element_type=F32)


def _meta_call(meta, cosf, sinf, normw, win_bf):
    return pl.pallas_call(
        _meta_kernel,
        out_shape=(jax.ShapeDtypeStruct((N_HEADS, HEAD_DIM, HEAD_DIM), F32),
                   jax.ShapeDtypeStruct((N_META, D_POOL), F32)),
        scratch_shapes=[pltpu.VMEM((CHUNK, D_RET), F32), pltpu.VMEM((CHUNK, D_RET), F32)],
        compiler_params=pltpu.CompilerParams(vmem_limit_bytes=VMEM_LIMIT_BYTES),
        name="meta",
    )(meta, cosf, sinf, normw, win_bf)


def _prompt_kernel(x_ref, cos_ref, sin_ref, smeta_ref, umeta_ref, normw_ref, pscale_ref, rnw_ref,
                   fnw_ref, win_ref, wpool_ref, wout_ref,
                   y_ref, s_ref, pbuf_ref,
                   ext_ref):
    c = pl.program_id(1)
    nb = x_ref.shape[0]
    batches = range(nb)

    @pl.when(c == 0)
    def _():
        for b in batches:
            s_ref[b] = smeta_ref[...]
            ext_ref[b, 0:HIST, :] = umeta_ref[...]

    x = x_ref[...].reshape(nb * CHUNK, D_MODEL)
    hb = _rms(x, normw_ref[...]).astype(BF16)

    def proj(off):
        return _dot(hb, win_ref[:, off:off + TILE_N])

    def rows(t, b):
        return t[b * CHUNK:(b + 1) * CHUNK]

    cosf, sinf = cos_ref[...], sin_ref[...]
    li = lax.broadcasted_iota(jnp.int32, (CHUNK, CHUNK), 0).astype(F32)
    mi = lax.broadcasted_iota(jnp.int32, (CHUNK, CHUNK), 1).astype(F32)
    diff = li - mi

    qkv, part, outs = {}, {}, {}

    def issue_qkv(pair):
        qkv[pair] = tuple(proj(off + pair * TILE_N) for off in (OFF_Q, OFF_K, OFF_V))

    def stage_a(pair):
        q2, k2, v2 = qkv.pop(pair)
        for i in range(2):
            h = 2 * pair + i
            lg = LOG_DECAY[h]
            sub = slice(i * HEAD_DIM, (i + 1) * HEAD_DIM)
            dmask = jnp.where(diff >= 0.0, jnp.exp(lg * jnp.maximum(diff, 0.0)), 0.0)
            q_decay = jnp.exp(lg * (li + 1.0))
            k_decay = jnp.exp(lg * (CHUNK - 1.0 - li))
            for b in batches:
                qr = _rotary(rows(q2, b)[:, sub], cosf, sinf)
                kr = _rotary(rows(k2, b)[:, sub], cosf, sinf) * K_SCALE
                vb = rows(v2, b)[:, sub].astype(BF16)
                state = s_ref[b, h]
                scores = lax.dot_general(qr.astype(BF16), kr.astype(BF16), (((1,), (1,)), ((), ())),
                                         preferred_element_type=F32)
                cross = _dot((qr * q_decay).astype(BF16), state.astype(BF16))
                s_ref[b, h] = math.exp(lg * CHUNK) * state + lax.dot_general(
                    (kr * k_decay).astype(BF16), vb, (((0,), (0,)), ((), ())),
                    preferred_element_type=F32)
                part[b, h] = ((scores * dmask).astype(BF16), vb, cross)

    def stage_b(pair):
        for i in range(2):
            h = 2 * pair + i
            for b in batches:
                p, vb, cross = part.pop((b, h))
                outs[b, h] = _dot(p, vb) + cross

    issue_qkv(0)
    issue_qkv(1)
    stage_a(0)
    issue_qkv(2)
    stage_a(1)
    stage_b(0)
    issue_qkv(3)
    stage_a(2)
    stage_b(1)
    u_tiles = [proj(OFF_U + g * POOL_GROUP) for g in range(len(POOL_WINDOWS))]
    stage_a(3)
    stage_b(2)
    gp_tiles = [proj(OFF_GP + g * POOL_GROUP) for g in range(len(POOL_WINDOWS))]
    stage_b(3)

    pool_y = [[None] * len(POOL_WINDOWS) for _ in batches]
    for g, w in enumerate(POOL_WINDOWS):
        cols = slice(g * POOL_GROUP, (g + 1) * POOL_GROUP)
        for b in batches:
            u = rows(u_tiles[g], b)
            ext_ref[b, HIST:HIST + CHUNK, cols] = u
            win_sum = u
            for j in range(1, w):
                win_sum = win_sum + ext_ref[b, HIST - j:HIST - j + CHUNK, cols]
            pooled = win_sum / float(w) - u
            mixed = _dot(pooled.astype(BF16), wpool_ref[g]) * pscale_ref[:, cols]
            pool_y[b][g] = (mixed * _silu(rows(gp_tiles[g], b))).astype(BF16)
    gr_tiles = [proj(OFF_GR + pair * TILE_N) for pair in range(N_HEADS // 2)]
    pool_mix = jnp.concatenate([jnp.concatenate(pool_y[b], axis=-1) for b in batches], axis=0)
    acc = x + _dot(pool_mix, wout_ref[0:D_POOL, :])

    ret_y = [[None] * N_HEADS for _ in batches]
    for h in range(N_HEADS):
        sub = slice((h % 2) * HEAD_DIM, (h % 2 + 1) * HEAD_DIM)
        for b in batches:
            o = outs.pop((b, h))
            rn = o * lax.rsqrt(jnp.mean(o * o, axis=-1, keepdims=True) + EPS)
            gate = _silu(rows(gr_tiles[h // 2], b)[:, sub])
            ret_y[b][h] = (rn * rnw_ref[:, _head(h)] * gate).astype(BF16)
    ret_mix = jnp.concatenate([jnp.concatenate(ret_y[b], axis=-1) for b in batches], axis=0)
    acc = acc + _dot(ret_mix, wout_ref[D_POOL:D_MIX, :])
    y_ref[...] = _rms(acc, fnw_ref[...]).reshape(nb, CHUNK, D_MODEL)

    @pl.when(c == pl.num_programs(1) - 1)
    def _():
        for b in batches:
            pbuf_ref[b] = ext_ref[b, HIST + CHUNK - POOL_BUF:HIST + CHUNK, :]

    for b in batches:
        ext_ref[b, 0:HIST, :] = ext_ref[b, CHUNK:CHUNK + HIST, :]


def _const_spec(shape):
    zeros = (0,) * len(shape)
    return pl.BlockSpec(shape, lambda b, c: zeros, pipeline_mode=pl.Buffered(1))


def _prompt_call(x, cosf, sinf, smeta, umeta, normw, pscale, rnw, fnw, win_bf, wpool_bf, wout_bf):
    batch, seq, _ = x.shape
    nb = PROMPT_BLOCK
    return pl.pallas_call(
        _prompt_kernel,
        grid=(batch // nb, seq // CHUNK),
        in_specs=[
            pl.BlockSpec((nb, CHUNK, D_MODEL), lambda b, c: (b, c, 0)),
            pl.BlockSpec((CHUNK, HEAD_DIM), lambda b, c: (c, 0)),
            pl.BlockSpec((CHUNK, HEAD_DIM), lambda b, c: (c, 0)),
            _const_spec((N_HEADS, HEAD_DIM, HEAD_DIM)),
            _const_spec((N_META, D_POOL)),
            _const_spec((1, D_MODEL)),
            _const_spec((1, D_POOL)),
            _const_spec((1, D_RET)),
            _const_spec((1, D_MODEL)),
            _const_spec((D_MODEL, D_IN_PROJ)),
            _const_spec((len(POOL_WINDOWS), POOL_GROUP, POOL_GROUP)),
            _const_spec((D_MIX, D_MODEL)),
        ],
        out_specs=[
            pl.BlockSpec((nb, CHUNK, D_MODEL), lambda b, c: (b, c, 0)),
            pl.BlockSpec((nb, N_HEADS, HEAD_DIM, HEAD_DIM), lambda b, c: (b, 0, 0, 0)),
            pl.BlockSpec((nb, POOL_BUF, D_POOL), lambda b, c: (b, 0, 0)),
        ],
        out_shape=(jax.ShapeDtypeStruct((batch, seq, D_MODEL), F32),
                   jax.ShapeDtypeStruct((batch, N_HEADS, HEAD_DIM, HEAD_DIM), F32),
                   jax.ShapeDtypeStruct((batch, POOL_BUF, D_POOL), F32)),
        scratch_shapes=[pltpu.VMEM((nb, HIST + CHUNK, D_POOL), F32)],
        compiler_params=pltpu.CompilerParams(
            dimension_semantics=("arbitrary", "arbitrary"), vmem_limit_bytes=VMEM_LIMIT_BYTES),
        name="prompt",
    )(x, cosf, sinf, smeta, umeta, normw, pscale, rnw, fnw, win_bf, wpool_bf, wout_bf)


def _sample_kernel(x_ref, cos_ref, sin_ref, sin_state_ref, pin_ref, normw_ref, pscale_ref, rnw_ref,
                   fnw_ref, win_ref, wpool_ref, wout_ref,
                   y_ref, sout_ref, pout_ref,
                   proj_ref, pooled_ref, o_ref, pblk_ref, oblk_ref):
    i = pl.program_id(0)
    nb = SAMPLE_BLOCK

    @pl.when(i == 0)
    def _():
        hb = _rms(x_ref[...], normw_ref[...]).astype(BF16)
        proj_ref[...] = _dot(hb, win_ref[...])

    base = pl.multiple_of(i * nb, nb)
    rows = pl.ds(base, nb)
    u8 = proj_ref[rows, OFF_U:OFF_U + D_POOL]

    for j in range(nb):
        for g, w in enumerate(POOL_WINDOWS):
            cols = slice(g * POOL_GROUP, (g + 1) * POOL_GROUP)
            uj = u8[j:j + 1, cols]
            acc = uj + jnp.sum(pin_ref[j, POOL_BUF - (w - 1):POOL_BUF, cols], axis=0, keepdims=True)
            pblk_ref[j:j + 1, cols] = acc / float(w) - uj
        pout_ref[j, 0:POOL_BUF - 1, :] = pin_ref[j, 1:POOL_BUF, :]
        pout_ref[j, POOL_BUF - 1:POOL_BUF, :] = u8[j:j + 1, :]
    pooled_ref[rows, :] = pblk_ref[...]

    cosf, sinf = cos_ref[...], sin_ref[...]
    for h in range(N_HEADS):
        g1 = math.exp(LOG_DECAY[h])
        q8 = _rotary(proj_ref[rows, OFF_Q + h * HEAD_DIM:OFF_Q + (h + 1) * HEAD_DIM], cosf, sinf)
        k8 = _rotary(proj_ref[rows, OFF_K + h * HEAD_DIM:OFF_K + (h + 1) * HEAD_DIM], cosf, sinf) * K_SCALE
        v8 = proj_ref[rows, OFF_V + h * HEAD_DIM:OFF_V + (h + 1) * HEAD_DIM]
        inner8 = jnp.sum(q8 * k8, axis=-1, keepdims=True) * v8
        qt = (q8 * g1).T
        kt = k8.T
        for j in range(nb):
            state = sin_state_ref[j, h]
            cross = jnp.sum(qt[:, j:j + 1] * state, axis=0, keepdims=True)
            oblk_ref[j:j + 1, _head(h)] = inner8[j:j + 1, :] + cross
            sout_ref[j, h] = g1 * state + kt[:, j:j + 1] * v8[j:j + 1, :]
    o_ref[rows, :] = oblk_ref[...]

    @pl.when(i == pl.num_programs(0) - 1)
    def _():
        pool_parts = []
        for g in range(len(POOL_WINDOWS)):
            cols = slice(g * POOL_GROUP, (g + 1) * POOL_GROUP)
            mixed = _dot(pooled_ref[:, cols].astype(BF16), wpool_ref[g]) * pscale_ref[:, cols]
            gate = _silu(proj_ref[:, OFF_GP + g * POOL_GROUP:OFF_GP + (g + 1) * POOL_GROUP])
            pool_parts.append((mixed * gate).astype(BF16))
        ret_parts = []
        for h in range(N_HEADS):
            o = o_ref[:, _head(h)]
            rn = o * lax.rsqrt(jnp.mean(o * o, axis=-1, keepdims=True) + EPS)
            gr = proj_ref[:, OFF_GR + h * HEAD_DIM:OFF_GR + (h + 1) * HEAD_DIM]
            ret_parts.append((rn * rnw_ref[:, _head(h)] * _silu(gr)).astype(BF16))
        mix = jnp.concatenate(pool_parts + ret_parts, axis=-1)
        y_ref[...] = _rms(x_ref[...] + _dot(mix, wout_ref[...]), fnw_ref[...])


def _sample_call(x, cosf, sinf, state, pool, normw, pscale, rnw, fnw, win_bf, wpool_bf, wout_bf):
    n = x.shape[0]
    nb = SAMPLE_BLOCK

    def const(shape):
        zeros = (0,) * len(shape)
        return pl.BlockSpec(shape, lambda i: zeros, pipeline_mode=pl.Buffered(1))

    return pl.pallas_call(
        _sample_kernel,
        grid=(n // nb,),
        in_specs=[
            const((n, D_MODEL)),
            const((1, HEAD_DIM)),
            const((1, HEAD_DIM)),
            pl.BlockSpec((nb, N_HEADS, HEAD_DIM, HEAD_DIM), lambda i: (i, 0, 0, 0)),
            pl.BlockSpec((nb, POOL_BUF, D_POOL), lambda i: (i, 0, 0)),
            const((1, D_MODEL)),
            const((1, D_POOL)),
            const((1, D_RET)),
            const((1, D_MODEL)),
            const((D_MODEL, D_IN_PROJ)),
            const((len(POOL_WINDOWS), POOL_GROUP, POOL_GROUP)),
            const((D_MIX, D_MODEL)),
        ],
        out_specs=[
            pl.BlockSpec((n, D_MODEL), lambda i: (0, 0)),
            pl.BlockSpec((nb, N_HEADS, HEAD_DIM, HEAD_DIM), lambda i: (i, 0, 0, 0)),
            pl.BlockSpec((nb, POOL_BUF, D_POOL), lambda i: (i, 0, 0)),
        ],
        out_shape=(jax.ShapeDtypeStruct((n, D_MODEL), F32),
                   jax.ShapeDtypeStruct((n, N_HEADS, HEAD_DIM, HEAD_DIM), F32),
                   jax.ShapeDtypeStruct((n, POOL_BUF, D_POOL), F32)),
        scratch_shapes=[pltpu.VMEM((n, D_IN_PROJ), F32),
                        pltpu.VMEM((n, D_POOL), F32),
                        pltpu.VMEM((n, D_RET), F32),
                        pltpu.VMEM((nb, D_POOL), F32),
                        pltpu.VMEM((nb, D_RET), F32)],
        compiler_params=pltpu.CompilerParams(
            dimension_semantics=("arbitrary",), vmem_limit_bytes=VMEM_LIMIT_BYTES),
        name="sample",
    )(x, cosf, sinf, state, pool, normw, pscale, rnw, fnw, win_bf, wpool_bf, wout_bf)


def _rotary_tables(pos):
    half = HEAD_DIM // 2
    inv = ROPE_BASE ** (-jnp.arange(half, dtype=F32) / half)
    ang = pos.astype(F32)[:, None] * inv[None, :]
    cos, sin = jnp.cos(ang), jnp.sin(ang)
    return jnp.concatenate([cos, cos], axis=-1), jnp.concatenate([-sin, sin], axis=-1)


def kernel(x_prompt, x_sample, state_ret, state_pool, meta_tokens, norm_w, w_in, w_pool,
           pool_scale, ret_norm_w, w_out, final_norm_w):
    assert norm_w.shape[0] == 1, "single-layer stack"
    seq = x_prompt.shape[1]
    win_bf = w_in[0].astype(BF16)
    wpool_bf = w_pool[0].astype(BF16)
    wout_bf = w_out[0].astype(BF16)
    normw, pscale, rnw = norm_w, pool_scale, ret_norm_w
    fnw = final_norm_w[None, :]

    cos_p, sin_p = _rotary_tables(jnp.arange(N_META + seq, dtype=jnp.int32))
    cos_s, sin_s = _rotary_tables(PAST_LEN + jnp.arange(1, dtype=jnp.int32))

    smeta, umeta = _meta_call(meta_tokens.astype(x_prompt.dtype), cos_p[:N_META], sin_p[:N_META],
                              normw, win_bf)
    y_p, s_p, buf_p = _prompt_call(x_prompt, cos_p[N_META:], sin_p[N_META:], smeta, umeta,
                                   normw, pscale, rnw, fnw, win_bf, wpool_bf, wout_bf)
    y_s, s_s, buf_s = _sample_call(x_sample[:, 0, :], cos_s, sin_s, state_ret[0], state_pool[0],
                                   normw, pscale, rnw, fnw, win_bf, wpool_bf, wout_bf)
    return (y_p, y_s[:, None, :], s_p[None], s_s[None], buf_p[None], buf_s[None])
```

```python
import math

import jax
import jax.numpy as jnp
from jax import lax
from jax.experimental import pallas as pl
from jax.experimental.pallas import tpu as pltpu

D_MODEL = 1024
D_POOL = 1024
D_RET = 1024
D_MIX = D_POOL + D_RET
POOL_WINDOWS = (2, 4, 8, 16)
POOL_GROUP = D_POOL // len(POOL_WINDOWS)
POOL_BUF = max(POOL_WINDOWS) - 1
N_HEADS = 8
HEAD_DIM = D_RET // N_HEADS
D_IN_PROJ = 2 * D_POOL + 4 * D_RET
N_META = 16
PAST_LEN = 16384
CHUNK = 128
ROPE_BASE = 10000.0
EPS = 1e-6
K_SCALE = HEAD_DIM ** -0.5

OFF_U, OFF_GP, OFF_Q, OFF_K, OFF_V, OFF_GR = (i * 1024 for i in range(6))

LOG_DECAY = tuple(math.log(1.0 - 2.0 ** (-5.0 - h)) for h in range(N_HEADS))

TILE_N = 256
PROMPT_BLOCK = 2
HIST = 16
SAMPLE_BLOCK = 8

VMEM_LIMIT_BYTES = 56 * 1024 * 1024

F32 = jnp.float32
BF16 = jnp.bfloat16


def _rms(x, w):
    return x * lax.rsqrt(jnp.mean(x * x, axis=-1, keepdims=True) + EPS) * w


def _silu(x):
    return x * (1.0 / (1.0 + jnp.exp(-x)))


def _dot(a, b):
    return jnp.dot(a, b, preferred_element_type=F32)


def _rotary(x, cosf, sinf):
    return x * cosf + pltpu.roll(x, HEAD_DIM // 2, 1) * sinf


def _head(h):
    return slice(h * HEAD_DIM, (h + 1) * HEAD_DIM)


def _meta_kernel(meta_ref, cos_ref, sin_ref, normw_ref, win_ref, s_ref, u_ref, kd_ref, v_ref):
    hb = _rms(meta_ref[...], normw_ref[...]).astype(BF16)
    u_ref[...] = _dot(hb, win_ref[:, OFF_U:OFF_U + D_POOL])
    k = _dot(hb, win_ref[:, OFF_K:OFF_K + D_RET])
    v = _dot(hb, win_ref[:, OFF_V:OFF_V + D_RET])
    cosf, sinf = cos_ref[...], sin_ref[...]
    row = lax.broadcasted_iota(jnp.int32, (N_META, HEAD_DIM), 0).astype(F32)
    kd_ref[...] = jnp.zeros_like(kd_ref)
    v_ref[...] = jnp.zeros_like(v_ref)
    for h in range(N_HEADS):
        kr = _rotary(k[:, _head(h)], cosf, sinf) * K_SCALE
        kd_ref[0:N_META, _head(h)] = kr * jnp.exp(LOG_DECAY[h] * (N_META - 1.0 - row))
        v_ref[0:N_META, _head(h)] = v[:, _head(h)]
    for h in range(N_HEADS):
        s_ref[h] = lax.dot_general(
            kd_ref[:, _head(h)].astype(BF16), v_ref[:, _head(h)].astype(BF16),
            (((0,), (0,)), ((), ())), preferred_element_type=F32)


def _meta_call(meta, cosf, sinf, normw, win_bf):
    return pl.pallas_call(
        _meta_kernel,
        out_shape=(jax.ShapeDtypeStruct((N_HEADS, HEAD_DIM, HEAD_DIM), F32),
                   jax.ShapeDtypeStruct((N_META, D_POOL), F32)),
        scratch_shapes=[pltpu.VMEM((CHUNK, D_RET), F32), pltpu.VMEM((CHUNK, D_RET), F32)],
        compiler_params=pltpu.CompilerParams(vmem_limit_bytes=VMEM_LIMIT_BYTES),
        name="meta",
    )(meta, cosf, sinf, normw, win_bf)


def _prompt_kernel(x_ref, cos_ref, sin_ref, smeta_ref, umeta_ref, normw_ref, pscale_ref, rnw_ref,
                   fnw_ref, win_ref, wpool_ref, wout_ref,
                   y_ref, s_ref, pbuf_ref,
                   ext_ref):
    c = pl.program_id(1)
    nb = x_ref.shape[0]
    batches = range(nb)

    @pl.when(c == 0)
    def _():
        for b in batches:
            s_ref[b] = smeta_ref[...]
            ext_ref[b, 0:HIST, :] = umeta_ref[...]

    x = x_ref[...].reshape(nb * CHUNK, D_MODEL)
    hb = _rms(x, normw_ref[...]).astype(BF16)

    def proj(off):
        return _dot(hb, win_ref[:, off:off + TILE_N])

    def rows(t, b):
        return t[b * CHUNK:(b + 1) * CHUNK]

    cosf, sinf = cos_ref[...], sin_ref[...]
    li = lax.broadcasted_iota(jnp.int32, (CHUNK, CHUNK), 0).astype(F32)
    mi = lax.broadcasted_iota(jnp.int32, (CHUNK, CHUNK), 1).astype(F32)
    diff = li - mi

    qkv, part, outs = {}, {}, {}

    def issue_qkv(pair):
        qkv[pair] = tuple(proj(off + pair * TILE_N) for off in (OFF_Q, OFF_K, OFF_V))

    def stage_a(pair):
        q2, k2, v2 = qkv.pop(pair)
        for i in range(2):
            h = 2 * pair + i
            lg = LOG_DECAY[h]
            sub = slice(i * HEAD_DIM, (i + 1) * HEAD_DIM)
            dmask = jnp.where(diff >= 0.0, jnp.exp(lg * jnp.maximum(diff, 0.0)), 0.0)
            q_decay = jnp.exp(lg * (li + 1.0))
            k_decay = jnp.exp(lg * (CHUNK - 1.0 - li))
            for b in batches:
                qr = _rotary(rows(q2, b)[:, sub], cosf, sinf)
                kr = _rotary(rows(k2, b)[:, sub], cosf, sinf) * K_SCALE
                vb = rows(v2, b)[:, sub].astype(BF16)
                state = s_ref[b, h]
                scores = lax.dot_general(qr.astype(BF16), kr.astype(BF16), (((1,), (1,)), ((), ())),
                                         preferred_element_type=F32)
                cross = _dot((qr * q_decay).astype(BF16), state.astype(BF16))
                s_ref[b, h] = math.exp(lg * CHUNK) * state + lax.dot_general(
                    (kr * k_decay).astype(BF16), vb, (((0,), (0,)), ((), ())),
                    preferred_element_type=F32)
                part[b, h] = ((scores * dmask).astype(BF16), vb, cross)

    def stage_b(pair):
        for i in range(2):
            h = 2 * pair + i
            for b in batches:
                p, vb, cross = part.pop((b, h))
                outs[b, h] = _dot(p, vb) + cross

    issue_qkv(0)
    issue_qkv(1)
    stage_a(0)
    issue_qkv(2)
    stage_a(1)
    stage_b(0)
    issue_qkv(3)
    stage_a(2)
    stage_b(1)
    u_tiles = [proj(OFF_U + g * POOL_GROUP) for g in range(len(POOL_WINDOWS))]
    stage_a(3)
    stage_b(2)
    gp_tiles = [proj(OFF_GP + g * POOL_GROUP) for g in range(len(POOL_WINDOWS))]
    stage_b(3)

    pool_y = [[None] * len(POOL_WINDOWS) for _ in batches]
    for g, w in enumerate(POOL_WINDOWS):
        cols = slice(g * POOL_GROUP, (g + 1) * POOL_GROUP)
        for b in batches:
            u = rows(u_tiles[g], b)
            ext_ref[b, HIST:HIST + CHUNK, cols] = u
            win_sum = u
            for j in range(1, w):
                win_sum = win_sum + ext_ref[b, HIST - j:HIST - j + CHUNK, cols]
            pooled = win_sum / float(w) - u
            mixed = _dot(pooled.astype(BF16), wpool_ref[g]) * pscale_ref[:, cols]
            pool_y[b][g] = (mixed * _silu(rows(gp_tiles[g], b))).astype(BF16)
    gr_tiles = [proj(OFF_GR + pair * TILE_N) for pair in range(N_HEADS // 2)]
    pool_mix = jnp.concatenate([jnp.concatenate(pool_y[b], axis=-1) for b in batches], axis=0)
    acc = x + _dot(pool_mix, wout_ref[0:D_POOL, :])

    ret_y = [[None] * N_HEADS for _ in batches]
    for h in range(N_HEADS):
        sub = slice((h % 2) * HEAD_DIM, (h % 2 + 1) * HEAD_DIM)
        for b in batches:
            o = outs.pop((b, h))
            rn = o * lax.rsqrt(jnp.mean(o * o, axis=-1, keepdims=True) + EPS)
            gate = _silu(rows(gr_tiles[h // 2], b)[:, sub])
            ret_y[b][h] = (rn * rnw_ref[:, _head(h)] * gate).astype(BF16)
    ret_mix = jnp.concatenate([jnp.concatenate(ret_y[b], axis=-1) for b in batches], axis=0)
    acc = acc + _dot(ret_mix, wout_ref[D_POOL:D_MIX, :])
    y_ref[...] = _rms(acc, fnw_ref[...]).reshape(nb, CHUNK, D_MODEL)

    @pl.when(c == pl.num_programs(1) - 1)
    def _():
        for b in batches:
            pbuf_ref[b] = ext_ref[b, HIST + CHUNK - POOL_BUF:HIST + CHUNK, :]

    for b in batches:
        ext_ref[b, 0:HIST, :] = ext_ref[b, CHUNK:CHUNK + HIST, :]


def _const_spec(shape):
    zeros = (0,) * len(shape)
    return pl.BlockSpec(shape, lambda b, c: zeros, pipeline_mode=pl.Buffered(1))


def _prompt_call(x, cosf, sinf, smeta, umeta, normw, pscale, rnw, fnw, win_bf, wpool_bf, wout_bf):
    batch, seq, _ = x.shape
    nb = PROMPT_BLOCK
    return pl.pallas_call(
        _prompt_kernel,
        grid=(batch // nb, seq // CHUNK),
        in_specs=[
            pl.BlockSpec((nb, CHUNK, D_MODEL), lambda b, c: (b, c, 0)),
            pl.BlockSpec((CHUNK, HEAD_DIM), lambda b, c: (c, 0)),
            pl.BlockSpec((CHUNK, HEAD_DIM), lambda b, c: (c, 0)),
            _const_spec((N_HEADS, HEAD_DIM, HEAD_DIM)),
            _const_spec((N_META, D_POOL)),
            _const_spec((1, D_MODEL)),
            _const_spec((1, D_POOL)),
            _const_spec((1, D_RET)),
            _const_spec((1, D_MODEL)),
            _const_spec((D_MODEL, D_IN_PROJ)),
            _const_spec((len(POOL_WINDOWS), POOL_GROUP, POOL_GROUP)),
            _const_spec((D_MIX, D_MODEL)),
        ],
        out_specs=[
            pl.BlockSpec((nb, CHUNK, D_MODEL), lambda b, c: (b, c, 0)),
            pl.BlockSpec((nb, N_HEADS, HEAD_DIM, HEAD_DIM), lambda b, c: (b, 0, 0, 0)),
            pl.BlockSpec((nb, POOL_BUF, D_POOL), lambda b, c: (b, 0, 0)),
        ],
        out_shape=(jax.ShapeDtypeStruct((batch, seq, D_MODEL), F32),
                   jax.ShapeDtypeStruct((batch, N_HEADS, HEAD_DIM, HEAD_DIM), F32),
                   jax.ShapeDtypeStruct((batch, POOL_BUF, D_POOL), F32)),
        scratch_shapes=[pltpu.VMEM((nb, HIST + CHUNK, D_POOL), F32)],
        compiler_params=pltpu.CompilerParams(
            dimension_semantics=("arbitrary", "arbitrary"), vmem_limit_bytes=VMEM_LIMIT_BYTES),
        name="prompt",
    )(x, cosf, sinf, smeta, umeta, normw, pscale, rnw, fnw, win_bf, wpool_bf, wout_bf)


def _sample_kernel(x_ref, cos_ref, sin_ref, state_ref, pin_ref, normw_ref, pscale_ref, rnw_ref,
                   fnw_ref, win_ref, wpool_ref, wout_ref,
                   y_ref, sout_ref, pout_ref,
                   proj_ref, pooled_ref, o_ref):
    i = pl.program_id(0)
    nb = SAMPLE_BLOCK

    @pl.when(i == 0)
    def _():
        hb = _rms(x_ref[...], normw_ref[...]).astype(BF16)
        proj_ref[...] = _dot(hb, win_ref[...])

    rows = pl.ds(pl.multiple_of(i * nb, nb), nb)
    u8 = proj_ref[rows, OFF_U:OFF_U + D_POOL]

    pooled = []
    for g, w in enumerate(POOL_WINDOWS):
        cols = slice(g * POOL_GROUP, (g + 1) * POOL_GROUP)
        win_sum = u8[:, cols]
        for r in range(POOL_BUF - (w - 1), POOL_BUF):
            win_sum = win_sum + pin_ref[r, :, cols]
        pooled.append(win_sum / float(w) - u8[:, cols])
    pooled_ref[rows, :] = jnp.concatenate(pooled, axis=-1)
    pout_ref[0:POOL_BUF - 1] = pin_ref[1:POOL_BUF]
    pout_ref[POOL_BUF - 1] = u8

    cosf, sinf = cos_ref[...], sin_ref[...]
    seq_of_row = lax.broadcasted_iota(jnp.int32, (nb, nb * HEAD_DIM), 0)
    seq_of_col = lax.broadcasted_iota(jnp.int32, (nb, nb * HEAD_DIM), 1) // HEAD_DIM
    own_block = seq_of_row == seq_of_col
    row_id = lax.broadcasted_iota(jnp.int32, (nb, HEAD_DIM), 0)
    for h in range(N_HEADS):
        g1 = math.exp(LOG_DECAY[h])
        q8 = _rotary(proj_ref[rows, OFF_Q + h * HEAD_DIM:OFF_Q + (h + 1) * HEAD_DIM], cosf, sinf)
        k8 = _rotary(proj_ref[rows, OFF_K + h * HEAD_DIM:OFF_K + (h + 1) * HEAD_DIM], cosf, sinf) * K_SCALE
        v8 = proj_ref[rows, OFF_V + h * HEAD_DIM:OFF_V + (h + 1) * HEAD_DIM]
        state = state_ref[:, h]
        q_blocks = jnp.where(own_block, jnp.concatenate([q8 * g1] * nb, axis=-1), 0.0)
        cross8 = _dot(q_blocks.astype(BF16), state.reshape(nb * HEAD_DIM, HEAD_DIM).astype(BF16))
        o_ref[rows, _head(h)] = jnp.sum(q8 * k8, axis=-1, keepdims=True) * v8 + cross8
        kb = k8.astype(BF16)
        for j in range(0, nb, 2):
            v_pair = jnp.concatenate([jnp.where(row_id == j, v8, 0.0),
                                      jnp.where(row_id == j + 1, v8, 0.0)], axis=-1).astype(BF16)
            upd = lax.dot_general(kb, v_pair, (((0,), (0,)), ((), ())), preferred_element_type=F32)
            sout_ref[j, h] = g1 * state[j] + upd[:, 0:HEAD_DIM]
            sout_ref[j + 1, h] = g1 * state[j + 1] + upd[:, HEAD_DIM:2 * HEAD_DIM]

    @pl.when(i == pl.num_programs(0) - 1)
    def _():
        pool_parts = []
        for g in range(len(POOL_WINDOWS)):
            cols = slice(g * POOL_GROUP, (g + 1) * POOL_GROUP)
            mixed = _dot(pooled_ref[:, cols].astype(BF16), wpool_ref[g]) * pscale_ref[:, cols]
            gate = _silu(proj_ref[:, OFF_GP + g * POOL_GROUP:OFF_GP + (g + 1) * POOL_GROUP])
            pool_parts.append((mixed * gate).astype(BF16))
        ret_parts = []
        for h in range(N_HEADS):
            o = o_ref[:, _head(h)]
            rn = o * lax.rsqrt(jnp.mean(o * o, axis=-1, keepdims=True) + EPS)
            gr = proj_ref[:, OFF_GR + h * HEAD_DIM:OFF_GR + (h + 1) * HEAD_DIM]
            ret_parts.append((rn * rnw_ref[:, _head(h)] * _silu(gr)).astype(BF16))
        mix = jnp.concatenate(pool_parts + ret_parts, axis=-1)
        y_ref[...] = _rms(x_ref[...] + _dot(mix, wout_ref[...]), fnw_ref[...])


def _sample_call(x, cosf, sinf, state, pool_t, normw, pscale, rnw, fnw, win_bf, wpool_bf, wout_bf):
    n = x.shape[0]
    nb = SAMPLE_BLOCK

    def const(shape):
        zeros = (0,) * len(shape)
        return pl.BlockSpec(shape, lambda i: zeros, pipeline_mode=pl.Buffered(1))

    return pl.pallas_call(
        _sample_kernel,
        grid=(n // nb,),
        in_specs=[
            const((n, D_MODEL)),
            const((1, HEAD_DIM)),
            const((1, HEAD_DIM)),
            pl.BlockSpec((nb, N_HEADS, HEAD_DIM, HEAD_DIM), lambda i: (i, 0, 0, 0)),
            pl.BlockSpec((POOL_BUF, nb, D_POOL), lambda i: (0, i, 0)),
            const((1, D_MODEL)),
            const((1, D_POOL)),
            const((1, D_RET)),
            const((1, D_MODEL)),
            const((D_MODEL, D_IN_PROJ)),
            const((len(POOL_WINDOWS), POOL_GROUP, POOL_GROUP)),
            const((D_MIX, D_MODEL)),
        ],
        out_specs=[
            pl.BlockSpec((n, D_MODEL), lambda i: (0, 0)),
            pl.BlockSpec((nb, N_HEADS, HEAD_DIM, HEAD_DIM), lambda i: (i, 0, 0, 0)),
            pl.BlockSpec((POOL_BUF, nb, D_POOL), lambda i: (0, i, 0)),
        ],
        out_shape=(jax.ShapeDtypeStruct((n, D_MODEL), F32),
                   jax.ShapeDtypeStruct((n, N_HEADS, HEAD_DIM, HEAD_DIM), F32),
                   jax.ShapeDtypeStruct((POOL_BUF, n, D_POOL), F32)),
        scratch_shapes=[pltpu.VMEM((n, D_IN_PROJ), F32),
                        pltpu.VMEM((n, D_POOL), F32),
                        pltpu.VMEM((n, D_RET), F32)],
        compiler_params=pltpu.CompilerParams(
            dimension_semantics=("arbitrary",), vmem_limit_bytes=VMEM_LIMIT_BYTES),
        name="sample",
    )(x, cosf, sinf, state, pool_t, normw, pscale, rnw, fnw, win_bf, wpool_bf, wout_bf)


def _rotary_tables(pos):
    half = HEAD_DIM // 2
    inv = ROPE_BASE ** (-jnp.arange(half, dtype=F32) / half)
    ang = pos.astype(F32)[:, None] * inv[None, :]
    cos, sin = jnp.cos(ang), jnp.sin(ang)
    return jnp.concatenate([cos, cos], axis=-1), jnp.concatenate([-sin, sin], axis=-1)


def kernel(x_prompt, x_sample, state_ret, state_pool, meta_tokens, norm_w, w_in, w_pool,
           pool_scale, ret_norm_w, w_out, final_norm_w):
    assert norm_w.shape[0] == 1, "single-layer stack"
    seq = x_prompt.shape[1]
    win_bf = w_in[0].astype(BF16)
    wpool_bf = w_pool[0].astype(BF16)
    wout_bf = w_out[0].astype(BF16)
    normw, pscale, rnw = norm_w, pool_scale, ret_norm_w
    fnw = final_norm_w[None, :]

    cos_p, sin_p = _rotary_tables(jnp.arange(N_META + seq, dtype=jnp.int32))
    cos_s, sin_s = _rotary_tables(PAST_LEN + jnp.arange(1, dtype=jnp.int32))

    smeta, umeta = _meta_call(meta_tokens.astype(x_prompt.dtype), cos_p[:N_META], sin_p[:N_META],
                              normw, win_bf)
    y_p, s_p, buf_p = _prompt_call(x_prompt, cos_p[N_META:], sin_p[N_META:], smeta, umeta,
                                   normw, pscale, rnw, fnw, win_bf, wpool_bf, wout_bf)
    y_s, s_s, buf_s = _sample_call(x_sample[:, 0, :], cos_s, sin_s, state_ret[0],
                                   jnp.transpose(state_pool[0], (1, 0, 2)),
                                   normw, pscale, rnw, fnw, win_bf, wpool_bf, wout_bf)
    return (y_p, y_s[:, None, :], s_p[None], s_s[None], buf_p[None],
            jnp.transpose(buf_s, (1, 0, 2))[None])
```

```python
import math

import jax
import jax.numpy as jnp
from jax import lax
from jax.experimental import pallas as pl
from jax.experimental.pallas import tpu as pltpu

D_MODEL = 1024
D_POOL = 1024
D_RET = 1024
D_MIX = D_POOL + D_RET
POOL_WINDOWS = (2, 4, 8, 16)
POOL_GROUP = D_POOL // len(POOL_WINDOWS)
POOL_BUF = max(POOL_WINDOWS) - 1
N_HEADS = 8
HEAD_DIM = D_RET // N_HEADS
D_IN_PROJ = 2 * D_POOL + 4 * D_RET
N_META = 16
PAST_LEN = 16384
CHUNK = 128
ROPE_BASE = 10000.0
EPS = 1e-6
K_SCALE = HEAD_DIM ** -0.5

OFF_U, OFF_GP, OFF_Q, OFF_K, OFF_V, OFF_GR = (i * 1024 for i in range(6))

LOG_DECAY = tuple(math.log(1.0 - 2.0 ** (-5.0 - h)) for h in range(N_HEADS))

TILE_N = 256
PROMPT_BLOCK = 4
HIST = 16
SAMPLE_BLOCK = 8

VMEM_LIMIT_BYTES = 56 * 1024 * 1024

F32 = jnp.float32
BF16 = jnp.bfloat16


def _rms(x, w):
    return x * lax.rsqrt(jnp.mean(x * x, axis=-1, keepdims=True) + EPS) * w


def _silu(x):
    return x * (1.0 / (1.0 + jnp.exp(-x)))


def _dot(a, b):
    return jnp.dot(a, b, preferred_element_type=F32)


def _rotary(x, cosf, sinf):
    return x * cosf + pltpu.roll(x, HEAD_DIM // 2, 1) * sinf


def _head(h):
    return slice(h * HEAD_DIM, (h + 1) * HEAD_DIM)


def _meta_kernel(meta_ref, cos_ref, sin_ref, normw_ref, win_ref, s_ref, u_ref, kd_ref, v_ref):
    hb = _rms(meta_ref[...], normw_ref[...]).astype(BF16)
    u_ref[...] = _dot(hb, win_ref[:, OFF_U:OFF_U + D_POOL])
    k = _dot(hb, win_ref[:, OFF_K:OFF_K + D_RET])
    v = _dot(hb, win_ref[:, OFF_V:OFF_V + D_RET])
    cosf, sinf = cos_ref[...], sin_ref[...]
    row = lax.broadcasted_iota(jnp.int32, (N_META, HEAD_DIM), 0).astype(F32)
    kd_ref[...] = jnp.zeros_like(kd_ref)
    v_ref[...] = jnp.zeros_like(v_ref)
    for h in range(N_HEADS):
        kr = _rotary(k[:, _head(h)], cosf, sinf) * K_SCALE
        kd_ref[0:N_META, _head(h)] = kr * jnp.exp(LOG_DECAY[h] * (N_META - 1.0 - row))
        v_ref[0:N_META, _head(h)] = v[:, _head(h)]
    for h in range(N_HEADS):
        s_ref[h] = lax.dot_general(
            kd_ref[:, _head(h)].astype(BF16), v_ref[:, _head(h)].astype(BF16),
            (((0,), (0,)), ((), ())), preferred_element_type=F32)


def _meta_call(meta, cosf, sinf, normw, win_bf):
    return pl.pallas_call(
        _meta_kernel,
        out_shape=(jax.ShapeDtypeStruct((N_HEADS, HEAD_DIM, HEAD_DIM), F32),
                   jax.ShapeDtypeStruct((N_META, D_POOL), F32)),
        scratch_shapes=[pltpu.VMEM((CHUNK, D_RET), F32), pltpu.VMEM((CHUNK, D_RET), F32)],
        compiler_params=pltpu.CompilerParams(vmem_limit_bytes=VMEM_LIMIT_BYTES),
        name="meta",
    )(meta, cosf, sinf, normw, win_bf)


def _prompt_kernel(x_ref, cos_ref, sin_ref, smeta_ref, umeta_ref, normw_ref, pscale_ref, rnw_ref,
                   fnw_ref, win_ref, wpool_ref, wout_ref,
                   y_ref, s_ref, pbuf_ref,
                   hist_ref):
    c = pl.program_id(1)
    nb = x_ref.shape[0]
    batches = range(nb)

    @pl.when(c == 0)
    def _():
        for b in batches:
            s_ref[b] = smeta_ref[...]
            hist_ref[b] = umeta_ref[...]

    x = x_ref[...].reshape(nb * CHUNK, D_MODEL)
    hb = _rms(x, normw_ref[...]).astype(BF16)

    def proj(off):
        return _dot(hb, win_ref[:, off:off + TILE_N])

    def rows(t, b):
        return t[b * CHUNK:(b + 1) * CHUNK]

    cosf, sinf = cos_ref[...], sin_ref[...]
    li = lax.broadcasted_iota(jnp.int32, (CHUNK, CHUNK), 0).astype(F32)
    mi = lax.broadcasted_iota(jnp.int32, (CHUNK, CHUNK), 1).astype(F32)
    diff = li - mi

    qkv, part, outs = {}, {}, {}

    def issue_qkv(pair):
        qkv[pair] = tuple(proj(off + pair * TILE_N) for off in (OFF_Q, OFF_K, OFF_V))

    def stage_a(pair):
        q2, k2, v2 = qkv.pop(pair)
        for i in range(2):
            h = 2 * pair + i
            lg = LOG_DECAY[h]
            sub = slice(i * HEAD_DIM, (i + 1) * HEAD_DIM)
            dmask = jnp.where(diff >= 0.0, jnp.exp(lg * jnp.maximum(diff, 0.0)), 0.0)
            q_decay = jnp.exp(lg * (li + 1.0))
            k_decay = jnp.exp(lg * (CHUNK - 1.0 - li))
            for b in batches:
                qr = _rotary(rows(q2, b)[:, sub], cosf, sinf)
                kr = _rotary(rows(k2, b)[:, sub], cosf, sinf) * K_SCALE
                vb = rows(v2, b)[:, sub].astype(BF16)
                state = s_ref[b, h]
                scores = lax.dot_general(qr.astype(BF16), kr.astype(BF16), (((1,), (1,)), ((), ())),
                                         preferred_element_type=F32)
                cross = _dot((qr * q_decay).astype(BF16), state.astype(BF16))
                s_ref[b, h] = math.exp(lg * CHUNK) * state + lax.dot_general(
                    (kr * k_decay).astype(BF16), vb, (((0,), (0,)), ((), ())),
                    preferred_element_type=F32)
                part[b, h] = ((scores * dmask).astype(BF16), vb, cross)

    def stage_b(pair):
        for i in range(2):
            h = 2 * pair + i
            for b in batches:
                p, vb, cross = part.pop((b, h))
                outs[b, h] = _dot(p, vb) + cross

    def pool_windows(u_tiles):
        pooled = []
        for g, w in enumerate(POOL_WINDOWS):
            cols = slice(g * POOL_GROUP, (g + 1) * POOL_GROUP)
            per_batch = []
            for b in batches:
                u = rows(u_tiles[g], b)
                ext = jnp.concatenate([hist_ref[b, :, cols], u], axis=0)
                win_sum = ext
                shift = 1
                while shift < w:
                    win_sum = win_sum + pltpu.roll(win_sum, shift, 0)
                    shift *= 2
                per_batch.append((win_sum[HIST:] / float(w) - u).astype(BF16))
                hist_ref[b, :, cols] = u[CHUNK - HIST:]
            pooled.append(jnp.concatenate(per_batch, axis=0))
        return pooled

    def pool_dots(pooled):
        return [_dot(pooled[g], wpool_ref[g]) * pscale_ref[:, g * POOL_GROUP:(g + 1) * POOL_GROUP]
                for g in range(len(POOL_WINDOWS))]

    issue_qkv(0)
    issue_qkv(1)
    u_tiles = [proj(OFF_U + g * POOL_GROUP) for g in range(len(POOL_WINDOWS))]
    stage_a(0)
    issue_qkv(2)
    pooled = pool_windows(u_tiles)
    stage_a(1)
    stage_b(0)
    gp_tiles = [proj(OFF_GP + g * POOL_GROUP) for g in range(len(POOL_WINDOWS))]
    issue_qkv(3)
    stage_a(2)
    stage_b(1)
    mixed = pool_dots(pooled)
    stage_a(3)
    stage_b(2)
    gr_tiles = [proj(OFF_GR + pair * TILE_N) for pair in range(N_HEADS // 2)]
    stage_b(3)
    pool_mix = jnp.concatenate(
        [(mixed[g] * _silu(gp_tiles[g])).astype(BF16) for g in range(len(POOL_WINDOWS))], axis=-1)
    acc = x + _dot(pool_mix, wout_ref[0:D_POOL, :])

    ret_y = [[None] * N_HEADS for _ in batches]
    for h in range(N_HEADS):
        sub = slice((h % 2) * HEAD_DIM, (h % 2 + 1) * HEAD_DIM)
        for b in batches:
            o = outs.pop((b, h))
            rn = o * lax.rsqrt(jnp.mean(o * o, axis=-1, keepdims=True) + EPS)
            gate = _silu(rows(gr_tiles[h // 2], b)[:, sub])
            ret_y[b][h] = (rn * rnw_ref[:, _head(h)] * gate).astype(BF16)
    ret_mix = jnp.concatenate([jnp.concatenate(ret_y[b], axis=-1) for b in batches], axis=0)
    acc = acc + _dot(ret_mix, wout_ref[D_POOL:D_MIX, :])
    y_ref[...] = _rms(acc, fnw_ref[...]).reshape(nb, CHUNK, D_MODEL)

    @pl.when(c == pl.num_programs(1) - 1)
    def _():
        for b in batches:
            pbuf_ref[b] = hist_ref[b, HIST - POOL_BUF:HIST, :]


def _const_spec(shape):
    zeros = (0,) * len(shape)
    return pl.BlockSpec(shape, lambda b, c: zeros, pipeline_mode=pl.Buffered(1))


def _prompt_call(x, cosf, sinf, smeta, umeta, normw, pscale, rnw, fnw, win_bf, wpool_bf, wout_bf):
    batch, seq, _ = x.shape
    nb = PROMPT_BLOCK
    return pl.pallas_call(
        _prompt_kernel,
        grid=(batch // nb, seq // CHUNK),
        in_specs=[
            pl.BlockSpec((nb, CHUNK, D_MODEL), lambda b, c: (b, c, 0)),
            pl.BlockSpec((CHUNK, HEAD_DIM), lambda b, c: (c, 0)),
            pl.BlockSpec((CHUNK, HEAD_DIM), lambda b, c: (c, 0)),
            _const_spec((N_HEADS, HEAD_DIM, HEAD_DIM)),
            _const_spec((N_META, D_POOL)),
            _const_spec((1, D_MODEL)),
            _const_spec((1, D_POOL)),
            _const_spec((1, D_RET)),
            _const_spec((1, D_MODEL)),
            _const_spec((D_MODEL, D_IN_PROJ)),
            _const_spec((len(POOL_WINDOWS), POOL_GROUP, POOL_GROUP)),
            _const_spec((D_MIX, D_MODEL)),
        ],
        out_specs=[
            pl.BlockSpec((nb, CHUNK, D_MODEL), lambda b, c: (b, c, 0)),
            pl.BlockSpec((nb, N_HEADS, HEAD_DIM, HEAD_DIM), lambda b, c: (b, 0, 0, 0)),
            pl.BlockSpec((nb, POOL_BUF, D_POOL), lambda b, c: (b, 0, 0)),
        ],
        out_shape=(jax.ShapeDtypeStruct((batch, seq, D_MODEL), F32),
                   jax.ShapeDtypeStruct((batch, N_HEADS, HEAD_DIM, HEAD_DIM), F32),
                   jax.ShapeDtypeStruct((batch, POOL_BUF, D_POOL), F32)),
        scratch_shapes=[pltpu.VMEM((nb, HIST, D_POOL), F32)],
        compiler_params=pltpu.CompilerParams(
            dimension_semantics=("arbitrary", "arbitrary"), vmem_limit_bytes=VMEM_LIMIT_BYTES),
        name="prompt",
    )(x, cosf, sinf, smeta, umeta, normw, pscale, rnw, fnw, win_bf, wpool_bf, wout_bf)


def _sample_kernel(x_ref, cos_ref, sin_ref, state_ref, pin_ref, normw_ref, pscale_ref, rnw_ref,
                   fnw_ref, win_ref, wpool_ref, wout_ref,
                   y_ref, sout_ref, pout_ref,
                   proj_ref, pooled_ref, o_ref):
    i = pl.program_id(0)
    nb = SAMPLE_BLOCK

    @pl.when(i == 0)
    def _():
        hb = _rms(x_ref[...], normw_ref[...]).astype(BF16)
        proj_ref[...] = _dot(hb, win_ref[...])

    rows = pl.ds(pl.multiple_of(i * nb, nb), nb)
    u8 = proj_ref[rows, OFF_U:OFF_U + D_POOL]

    pooled = []
    for g, w in enumerate(POOL_WINDOWS):
        cols = slice(g * POOL_GROUP, (g + 1) * POOL_GROUP)
        win_sum = u8[:, cols]
        for r in range(POOL_BUF - (w - 1), POOL_BUF):
            win_sum = win_sum + pin_ref[r, :, cols]
        pooled.append(win_sum / float(w) - u8[:, cols])
    pooled_ref[rows, :] = jnp.concatenate(pooled, axis=-1)
    pout_ref[0:POOL_BUF - 1] = pin_ref[1:POOL_BUF]
    pout_ref[POOL_BUF - 1] = u8

    cosf, sinf = cos_ref[...], sin_ref[...]
    seq_of_row = lax.broadcasted_iota(jnp.int32, (nb, nb * HEAD_DIM), 0)
    seq_of_col = lax.broadcasted_iota(jnp.int32, (nb, nb * HEAD_DIM), 1) // HEAD_DIM
    own_block = seq_of_row == seq_of_col
    row_id = lax.broadcasted_iota(jnp.int32, (nb, HEAD_DIM), 0)
    for h in range(N_HEADS):
        g1 = math.exp(LOG_DECAY[h])
        q8 = _rotary(proj_ref[rows, OFF_Q + h * HEAD_DIM:OFF_Q + (h + 1) * HEAD_DIM], cosf, sinf)
        k8 = _rotary(proj_ref[rows, OFF_K + h * HEAD_DIM:OFF_K + (h + 1) * HEAD_DIM], cosf, sinf) * K_SCALE
        v8 = proj_ref[rows, OFF_V + h * HEAD_DIM:OFF_V + (h + 1) * HEAD_DIM]
        state = state_ref[:, h]
        q_blocks = jnp.where(own_block, jnp.concatenate([q8 * g1] * nb, axis=-1), 0.0)
        cross8 = _dot(q_blocks.astype(BF16), state.reshape(nb * HEAD_DIM, HEAD_DIM).astype(BF16))
        o_ref[rows, _head(h)] = jnp.sum(q8 * k8, axis=-1, keepdims=True) * v8 + cross8
        kb = k8.astype(BF16)
        for j in range(0, nb, 2):
            v_pair = jnp.concatenate([jnp.where(row_id == j, v8, 0.0),
                                      jnp.where(row_id == j + 1, v8, 0.0)], axis=-1).astype(BF16)
            upd = lax.dot_general(kb, v_pair, (((0,), (0,)), ((), ())), preferred_element_type=F32)
            sout_ref[j, h] = g1 * state[j] + upd[:, 0:HEAD_DIM]
            sout_ref[j + 1, h] = g1 * state[j + 1] + upd[:, HEAD_DIM:2 * HEAD_DIM]

    @pl.when(i == pl.num_programs(0) - 1)
    def _():
        pool_parts = []
        for g in range(len(POOL_WINDOWS)):
            cols = slice(g * POOL_GROUP, (g + 1) * POOL_GROUP)
            mixed = _dot(pooled_ref[:, cols].astype(BF16), wpool_ref[g]) * pscale_ref[:, cols]
            gate = _silu(proj_ref[:, OFF_GP + g * POOL_GROUP:OFF_GP + (g + 1) * POOL_GROUP])
            pool_parts.append((mixed * gate).astype(BF16))
        ret_parts = []
        for h in range(N_HEADS):
            o = o_ref[:, _head(h)]
            rn = o * lax.rsqrt(jnp.mean(o * o, axis=-1, keepdims=True) + EPS)
            gr = proj_ref[:, OFF_GR + h * HEAD_DIM:OFF_GR + (h + 1) * HEAD_DIM]
            ret_parts.append((rn * rnw_ref[:, _head(h)] * _silu(gr)).astype(BF16))
        mix = jnp.concatenate(pool_parts + ret_parts, axis=-1)
        y_ref[...] = _rms(x_ref[...] + _dot(mix, wout_ref[...]), fnw_ref[...])


def _sample_call(x, cosf, sinf, state, pool_t, normw, pscale, rnw, fnw, win_bf, wpool_bf, wout_bf):
    n = x.shape[0]
    nb = SAMPLE_BLOCK

    def const(shape):
        zeros = (0,) * len(shape)
        return pl.BlockSpec(shape, lambda i: zeros, pipeline_mode=pl.Buffered(1))

    return pl.pallas_call(
        _sample_kernel,
        grid=(n // nb,),
        in_specs=[
            const((n, D_MODEL)),
            const((1, HEAD_DIM)),
            const((1, HEAD_DIM)),
            pl.BlockSpec((nb, N_HEADS, HEAD_DIM, HEAD_DIM), lambda i: (i, 0, 0, 0)),
            pl.BlockSpec((POOL_BUF, nb, D_POOL), lambda i: (0, i, 0)),
            const((1, D_MODEL)),
            const((1, D_POOL)),
            const((1, D_RET)),
            const((1, D_MODEL)),
            const((D_MODEL, D_IN_PROJ)),
            const((len(POOL_WINDOWS), POOL_GROUP, POOL_GROUP)),
            const((D_MIX, D_MODEL)),
        ],
        out_specs=[
            pl.BlockSpec((n, D_MODEL), lambda i: (0, 0)),
            pl.BlockSpec((nb, N_HEADS, HEAD_DIM, HEAD_DIM), lambda i: (i, 0, 0, 0)),
            pl.BlockSpec((POOL_BUF, nb, D_POOL), lambda i: (0, i, 0)),
        ],
        out_shape=(jax.ShapeDtypeStruct((n, D_MODEL), F32),
                   jax.ShapeDtypeStruct((n, N_HEADS, HEAD_DIM, HEAD_DIM), F32),
                   jax.ShapeDtypeStruct((POOL_BUF, n, D_POOL), F32)),
        scratch_shapes=[pltpu.VMEM((n, D_IN_PROJ), F32),
                        pltpu.VMEM((n, D_POOL), F32),
                        pltpu.VMEM((n, D_RET), F32)],
        compiler_params=pltpu.CompilerParams(
            dimension_semantics=("arbitrary",), vmem_limit_bytes=VMEM_LIMIT_BYTES),
        name="sample",
    )(x, cosf, sinf, state, pool_t, normw, pscale, rnw, fnw, win_bf, wpool_bf, wout_bf)


def _rotary_tables(pos):
    half = HEAD_DIM // 2
    inv = ROPE_BASE ** (-jnp.arange(half, dtype=F32) / half)
    ang = pos.astype(F32)[:, None] * inv[None, :]
    cos, sin = jnp.cos(ang), jnp.sin(ang)
    return jnp.concatenate([cos, cos], axis=-1), jnp.concatenate([-sin, sin], axis=-1)


def kernel(x_prompt, x_sample, state_ret, state_pool, meta_tokens, norm_w, w_in, w_pool,
           pool_scale, ret_norm_w, w_out, final_norm_w):
    assert norm_w.shape[0] == 1, "single-layer stack"
    seq = x_prompt.shape[1]
    win_bf = w_in[0].astype(BF16)
    wpool_bf = w_pool[0].astype(BF16)
    wout_bf = w_out[0].astype(BF16)
    normw, pscale, rnw = norm_w, pool_scale, ret_norm_w
    fnw = final_norm_w[None, :]

    cos_p, sin_p = _rotary_tables(jnp.arange(N_META + seq, dtype=jnp.int32))
    cos_s, sin_s = _rotary_tables(PAST_LEN + jnp.arange(1, dtype=jnp.int32))

    smeta, umeta = _meta_call(meta_tokens.astype(x_prompt.dtype), cos_p[:N_META], sin_p[:N_META],
                              normw, win_bf)
    y_p, s_p, buf_p = _prompt_call(x_prompt, cos_p[N_META:], sin_p[N_META:], smeta, umeta,
                                   normw, pscale, rnw, fnw, win_bf, wpool_bf, wout_bf)
    y_s, s_s, buf_s = _sample_call(x_sample[:, 0, :], cos_s, sin_s, state_ret[0],
                                   jnp.transpose(state_pool[0], (1, 0, 2)),
                                   normw, pscale, rnw, fnw, win_bf, wpool_bf, wout_bf)
    return (y_p, y_s[:, None, :], s_p[None], s_s[None], buf_p[None],
            jnp.transpose(buf_s, (1, 0, 2))[None])
```

```python
import math

import jax
import jax.numpy as jnp
import numpy as np
from jax import lax
from jax.experimental import pallas as pl
from jax.experimental.pallas import tpu as pltpu

D_MODEL = 1024
D_POOL = 1024
D_RET = 1024
D_MIX = D_POOL + D_RET
POOL_WINDOWS = (2, 4, 8, 16)
POOL_GROUP = D_POOL // len(POOL_WINDOWS)
POOL_BUF = max(POOL_WINDOWS) - 1
N_HEADS = 8
HEAD_DIM = D_RET // N_HEADS
D_IN_PROJ = 2 * D_POOL + 4 * D_RET
N_META = 16
PAST_LEN = 16384
CHUNK = 128
ROPE_BASE = 10000.0
EPS = 1e-6
K_SCALE = HEAD_DIM ** -0.5

OFF_U, OFF_GP, OFF_Q, OFF_K, OFF_V, OFF_GR = (i * 1024 for i in range(6))

LOG_DECAY = tuple(math.log(1.0 - 2.0 ** (-5.0 - h)) for h in range(N_HEADS))

TILE_N = 256
PROMPT_BLOCK = 4
OUT_PIECES = 2
N_OUT_TILES = 8
HIST = 16
SAMPLE_BLOCK = 8

VMEM_LIMIT_BYTES = 56 * 1024 * 1024

F32 = jnp.float32
BF16 = jnp.bfloat16


def _rms(x, w):
    return x * lax.rsqrt(jnp.mean(x * x, axis=-1, keepdims=True) + EPS) * w


def _silu(x):
    return x * (1.0 / (1.0 + jnp.exp(-x)))


def _dot(a, b):
    return jnp.dot(a, b, preferred_element_type=F32)


def _rotary(x, cosf, sinf):
    return x * cosf + pltpu.roll(x, HEAD_DIM // 2, 1) * sinf


def _head(h):
    return slice(h * HEAD_DIM, (h + 1) * HEAD_DIM)


def _prep_kernel(meta_ref, cos_ref, sin_ref, normw_ref, win_ref, wout_ref, wpool_ref,
                 winb_ref, woutb_ref, wpoolb_ref, s_ref, u_ref,
                 hb_ref, proj_ref, kd_ref, v_ref):
    j = pl.program_id(0)

    @pl.when(j == 0)
    def _():
        hb_ref[...] = _rms(meta_ref[...], normw_ref[...]).astype(BF16)
        wpoolb_ref[...] = wpool_ref[...].astype(BF16)

    tile = win_ref[...].astype(BF16)
    winb_ref[...] = tile
    proj_ref[j] = _dot(hb_ref[...], tile)

    @pl.when(j < N_OUT_TILES)
    def _():
        woutb_ref[...] = wout_ref[...].astype(BF16)

    @pl.when(j == pl.num_programs(0) - 1)
    def _():
        cosf, sinf = cos_ref[...], sin_ref[...]
        row = lax.broadcasted_iota(jnp.int32, (N_META, HEAD_DIM), 0).astype(F32)
        kd_ref[...] = jnp.zeros_like(kd_ref)
        v_ref[...] = jnp.zeros_like(v_ref)
        for g in range(D_POOL // TILE_N):
            u_ref[:, g * TILE_N:(g + 1) * TILE_N] = proj_ref[OFF_U // TILE_N + g]
        for pair in range(N_HEADS // 2):
            k2 = proj_ref[OFF_K // TILE_N + pair]
            v2 = proj_ref[OFF_V // TILE_N + pair]
            for i in range(2):
                h = 2 * pair + i
                sub = slice(i * HEAD_DIM, (i + 1) * HEAD_DIM)
                kr = _rotary(k2[:, sub], cosf, sinf) * K_SCALE
                kd_ref[0:N_META, _head(h)] = kr * jnp.exp(LOG_DECAY[h] * (N_META - 1.0 - row))
                v_ref[0:N_META, _head(h)] = v2[:, sub]
        for h in range(N_HEADS):
            s_ref[h] = lax.dot_general(
                kd_ref[:, _head(h)].astype(BF16), v_ref[:, _head(h)].astype(BF16),
                (((0,), (0,)), ((), ())), preferred_element_type=F32)


def _prep_call(meta, cosf, sinf, normw, w_in, w_out, w_pool):
    n_tiles = D_IN_PROJ // TILE_N

    def const(shape):
        zeros = (0,) * len(shape)
        return pl.BlockSpec(shape, lambda j: zeros)

    out_tile = lambda j: (jnp.minimum(j, N_OUT_TILES - 1), 0)
    return pl.pallas_call(
        _prep_kernel,
        grid=(n_tiles,),
        in_specs=[
            const((N_META, D_MODEL)),
            const((N_META, HEAD_DIM)),
            const((N_META, HEAD_DIM)),
            const((1, D_MODEL)),
            pl.BlockSpec((D_MODEL, TILE_N), lambda j: (0, j)),
            pl.BlockSpec((D_MIX // N_OUT_TILES, D_MODEL), out_tile),
            const((len(POOL_WINDOWS), POOL_GROUP, POOL_GROUP)),
        ],
        out_specs=[
            pl.BlockSpec((D_MODEL, TILE_N), lambda j: (0, j)),
            pl.BlockSpec((D_MIX // N_OUT_TILES, D_MODEL), out_tile),
            const((len(POOL_WINDOWS), POOL_GROUP, POOL_GROUP)),
            const((N_HEADS, HEAD_DIM, HEAD_DIM)),
            const((N_META, D_POOL)),
        ],
        out_shape=(jax.ShapeDtypeStruct((D_MODEL, D_IN_PROJ), BF16),
                   jax.ShapeDtypeStruct((D_MIX, D_MODEL), BF16),
                   jax.ShapeDtypeStruct((len(POOL_WINDOWS), POOL_GROUP, POOL_GROUP), BF16),
                   jax.ShapeDtypeStruct((N_HEADS, HEAD_DIM, HEAD_DIM), F32),
                   jax.ShapeDtypeStruct((N_META, D_POOL), F32)),
        scratch_shapes=[pltpu.VMEM((N_META, D_MODEL), BF16),
                        pltpu.VMEM((n_tiles, N_META, TILE_N), F32),
                        pltpu.VMEM((CHUNK, D_RET), F32),
                        pltpu.VMEM((CHUNK, D_RET), F32)],
        compiler_params=pltpu.CompilerParams(
            dimension_semantics=("arbitrary",), vmem_limit_bytes=VMEM_LIMIT_BYTES),
        name="prep",
    )(meta, cosf, sinf, normw, w_in, w_out, w_pool)


def _prompt_kernel(x_ref, cos_ref, sin_ref, smeta_ref, umeta_ref, normw_ref, pscale_ref, rnw_ref,
                   fnw_ref, win_ref, wpool_ref, wout_ref,
                   y_ref, s_ref, pbuf_ref,
                   hist_ref):
    c = pl.program_id(1)
    nb = x_ref.shape[0]
    batches = range(nb)

    @pl.when(c == 0)
    def _():
        for b in batches:
            s_ref[b] = smeta_ref[...]
            hist_ref[b] = umeta_ref[...]

    x = x_ref[...].reshape(nb * CHUNK, D_MODEL)
    hb = _rms(x, normw_ref[...]).astype(BF16)

    def proj(off):
        return _dot(hb, win_ref[:, off:off + TILE_N])

    def rows(t, b):
        return t[b * CHUNK:(b + 1) * CHUNK]

    cosf, sinf = cos_ref[...], sin_ref[...]
    li = lax.broadcasted_iota(jnp.int32, (CHUNK, CHUNK), 0).astype(F32)
    mi = lax.broadcasted_iota(jnp.int32, (CHUNK, CHUNK), 1).astype(F32)
    diff = li - mi

    qkv, part, outs = {}, {}, {}

    def issue_qkv(pair):
        qkv[pair] = tuple(proj(off + pair * TILE_N) for off in (OFF_Q, OFF_K, OFF_V))

    def stage_a(pair):
        q2, k2, v2 = qkv.pop(pair)
        for i in range(2):
            h = 2 * pair + i
            lg = LOG_DECAY[h]
            sub = slice(i * HEAD_DIM, (i + 1) * HEAD_DIM)
            dmask = jnp.where(diff >= 0.0, jnp.exp(lg * jnp.maximum(diff, 0.0)), 0.0)
            q_decay = jnp.exp(lg * (li + 1.0))
            k_decay = jnp.exp(lg * (CHUNK - 1.0 - li))
            for b in batches:
                qr = _rotary(rows(q2, b)[:, sub], cosf, sinf)
                kr = _rotary(rows(k2, b)[:, sub], cosf, sinf) * K_SCALE
                vb = rows(v2, b)[:, sub].astype(BF16)
                state = s_ref[b, h]
                scores = lax.dot_general(qr.astype(BF16), kr.astype(BF16), (((1,), (1,)), ((), ())),
                                         preferred_element_type=F32)
                cross = _dot((qr * q_decay).astype(BF16), state.astype(BF16))
                s_ref[b, h] = math.exp(lg * CHUNK) * state + lax.dot_general(
                    (kr * k_decay).astype(BF16), vb, (((0,), (0,)), ((), ())),
                    preferred_element_type=F32)
                part[b, h] = ((scores * dmask).astype(BF16), vb, cross)

    def stage_b(pair):
        for i in range(2):
            h = 2 * pair + i
            for b in batches:
                p, vb, cross = part.pop((b, h))
                outs[b, h] = _dot(p, vb) + cross

    def pool_windows(u_tiles):
        pooled = []
        for g, w in enumerate(POOL_WINDOWS):
            cols = slice(g * POOL_GROUP, (g + 1) * POOL_GROUP)
            per_batch = []
            for b in batches:
                u = rows(u_tiles[g], b)
                ext = jnp.concatenate([hist_ref[b, :, cols], u], axis=0)
                win_sum = ext
                shift = 1
                while shift < w:
                    win_sum = win_sum + pltpu.roll(win_sum, shift, 0)
                    shift *= 2
                per_batch.append((win_sum[HIST:] / float(w) - u).astype(BF16))
                hist_ref[b, :, cols] = u[CHUNK - HIST:]
            pooled.append(jnp.concatenate(per_batch, axis=0))
        return pooled

    def pool_dots(pooled):
        return [_dot(pooled[g], wpool_ref[g]) * pscale_ref[:, g * POOL_GROUP:(g + 1) * POOL_GROUP]
                for g in range(len(POOL_WINDOWS))]

    issue_qkv(0)
    issue_qkv(1)
    u_tiles = [proj(OFF_U + g * POOL_GROUP) for g in range(len(POOL_WINDOWS))]
    stage_a(0)
    issue_qkv(2)
    pooled = pool_windows(u_tiles)
    stage_a(1)
    stage_b(0)
    gp_tiles = [proj(OFF_GP + g * POOL_GROUP) for g in range(len(POOL_WINDOWS))]
    issue_qkv(3)
    stage_a(2)
    stage_b(1)
    mixed = pool_dots(pooled)
    stage_a(3)
    stage_b(2)
    gr_tiles = [proj(OFF_GR + pair * TILE_N) for pair in range(N_HEADS // 2)]
    stage_b(3)
    pool_mix = jnp.concatenate(
        [(mixed[g] * _silu(gp_tiles[g])).astype(BF16) for g in range(len(POOL_WINDOWS))], axis=-1)
    acc = x + _dot(pool_mix, wout_ref[0:D_POOL, :])

    per_piece = max(nb // OUT_PIECES, 1)
    for first in range(0, nb, per_piece):
        piece = range(first, first + per_piece)
        ret_y = []
        for b in piece:
            heads = []
            for h in range(N_HEADS):
                sub = slice((h % 2) * HEAD_DIM, (h % 2 + 1) * HEAD_DIM)
                o = outs.pop((b, h))
                rn = o * lax.rsqrt(jnp.mean(o * o, axis=-1, keepdims=True) + EPS)
                gate = _silu(rows(gr_tiles[h // 2], b)[:, sub])
                heads.append((rn * rnw_ref[:, _head(h)] * gate).astype(BF16))
            ret_y.append(jnp.concatenate(heads, axis=-1))
        ret_mix = jnp.concatenate(ret_y, axis=0)
        acc_piece = acc[first * CHUNK:(first + per_piece) * CHUNK]
        acc_piece = acc_piece + _dot(ret_mix, wout_ref[D_POOL:D_MIX, :])
        y_ref[first:first + per_piece] = _rms(acc_piece, fnw_ref[...]).reshape(
            per_piece, CHUNK, D_MODEL)

    @pl.when(c == pl.num_programs(1) - 1)
    def _():
        for b in batches:
            pbuf_ref[b] = hist_ref[b, HIST - POOL_BUF:HIST, :]


def _const_spec(shape):
    zeros = (0,) * len(shape)
    return pl.BlockSpec(shape, lambda b, c: zeros, pipeline_mode=pl.Buffered(1))


def _prompt_call(x, cosf, sinf, smeta, umeta, normw, pscale, rnw, fnw, win_bf, wpool_bf, wout_bf):
    batch, seq, _ = x.shape
    nb = PROMPT_BLOCK
    return pl.pallas_call(
        _prompt_kernel,
        grid=(batch // nb, seq // CHUNK),
        in_specs=[
            pl.BlockSpec((nb, CHUNK, D_MODEL), lambda b, c: (b, c, 0)),
            pl.BlockSpec((CHUNK, HEAD_DIM), lambda b, c: (c, 0)),
            pl.BlockSpec((CHUNK, HEAD_DIM), lambda b, c: (c, 0)),
            _const_spec((N_HEADS, HEAD_DIM, HEAD_DIM)),
            _const_spec((N_META, D_POOL)),
            _const_spec((1, D_MODEL)),
            _const_spec((1, D_POOL)),
            _const_spec((1, D_RET)),
            _const_spec((1, D_MODEL)),
            _const_spec((D_MODEL, D_IN_PROJ)),
            _const_spec((len(POOL_WINDOWS), POOL_GROUP, POOL_GROUP)),
            _const_spec((D_MIX, D_MODEL)),
        ],
        out_specs=[
            pl.BlockSpec((nb, CHUNK, D_MODEL), lambda b, c: (b, c, 0)),
            pl.BlockSpec((nb, N_HEADS, HEAD_DIM, HEAD_DIM), lambda b, c: (b, 0, 0, 0)),
            pl.BlockSpec((nb, POOL_BUF, D_POOL), lambda b, c: (b, 0, 0)),
        ],
        out_shape=(jax.ShapeDtypeStruct((batch, seq, D_MODEL), F32),
                   jax.ShapeDtypeStruct((batch, N_HEADS, HEAD_DIM, HEAD_DIM), F32),
                   jax.ShapeDtypeStruct((batch, POOL_BUF, D_POOL), F32)),
        scratch_shapes=[pltpu.VMEM((nb, HIST, D_POOL), F32)],
        compiler_params=pltpu.CompilerParams(
            dimension_semantics=("arbitrary", "arbitrary"), vmem_limit_bytes=VMEM_LIMIT_BYTES),
        name="prompt",
    )(x, cosf, sinf, smeta, umeta, normw, pscale, rnw, fnw, win_bf, wpool_bf, wout_bf)


def _sample_kernel(x_ref, cos_ref, sin_ref, state_ref, pin_ref, normw_ref, pscale_ref, rnw_ref,
                   fnw_ref, win_ref, wpool_ref, wout_ref,
                   y_ref, sout_ref, pout_ref,
                   proj_ref, pooled_ref, o_ref):
    i = pl.program_id(0)
    nb = SAMPLE_BLOCK

    @pl.when(i == 0)
    def _():
        hb = _rms(x_ref[...], normw_ref[...]).astype(BF16)
        proj_ref[...] = _dot(hb, win_ref[...])

    rows = pl.ds(pl.multiple_of(i * nb, nb), nb)
    u8 = proj_ref[rows, OFF_U:OFF_U + D_POOL]

    pooled = []
    for g, w in enumerate(POOL_WINDOWS):
        cols = slice(g * POOL_GROUP, (g + 1) * POOL_GROUP)
        win_sum = u8[:, cols]
        for r in range(POOL_BUF - (w - 1), POOL_BUF):
            win_sum = win_sum + pin_ref[r, :, cols]
        pooled.append(win_sum / float(w) - u8[:, cols])
    pooled_ref[rows, :] = jnp.concatenate(pooled, axis=-1)
    pout_ref[0:POOL_BUF - 1] = pin_ref[1:POOL_BUF]
    pout_ref[POOL_BUF - 1] = u8

    cosf, sinf = cos_ref[...], sin_ref[...]
    seq_of_row = lax.broadcasted_iota(jnp.int32, (nb, nb * HEAD_DIM), 0)
    seq_of_col = lax.broadcasted_iota(jnp.int32, (nb, nb * HEAD_DIM), 1) // HEAD_DIM
    own_block = seq_of_row == seq_of_col
    row_id = lax.broadcasted_iota(jnp.int32, (nb, HEAD_DIM), 0)
    for h in range(N_HEADS):
        g1 = math.exp(LOG_DECAY[h])
        q8 = _rotary(proj_ref[rows, OFF_Q + h * HEAD_DIM:OFF_Q + (h + 1) * HEAD_DIM], cosf, sinf)
        k8 = _rotary(proj_ref[rows, OFF_K + h * HEAD_DIM:OFF_K + (h + 1) * HEAD_DIM], cosf, sinf) * K_SCALE
        v8 = proj_ref[rows, OFF_V + h * HEAD_DIM:OFF_V + (h + 1) * HEAD_DIM]
        state = state_ref[:, h]
        q_blocks = jnp.where(own_block, jnp.concatenate([q8 * g1] * nb, axis=-1), 0.0)
        cross8 = _dot(q_blocks.astype(BF16), state.reshape(nb * HEAD_DIM, HEAD_DIM).astype(BF16))
        o_ref[rows, _head(h)] = jnp.sum(q8 * k8, axis=-1, keepdims=True) * v8 + cross8
        kb = k8.astype(BF16)
        for j in range(0, nb, 2):
            v_pair = jnp.concatenate([jnp.where(row_id == j, v8, 0.0),
                                      jnp.where(row_id == j + 1, v8, 0.0)], axis=-1).astype(BF16)
            upd = lax.dot_general(kb, v_pair, (((0,), (0,)), ((), ())), preferred_element_type=F32)
            sout_ref[j, h] = g1 * state[j] + upd[:, 0:HEAD_DIM]
            sout_ref[j + 1, h] = g1 * state[j + 1] + upd[:, HEAD_DIM:2 * HEAD_DIM]

    @pl.when(i == pl.num_programs(0) - 1)
    def _():
        pool_parts = []
        for g in range(len(POOL_WINDOWS)):
            cols = slice(g * POOL_GROUP, (g + 1) * POOL_GROUP)
            mixed = _dot(pooled_ref[:, cols].astype(BF16), wpool_ref[g]) * pscale_ref[:, cols]
            gate = _silu(proj_ref[:, OFF_GP + g * POOL_GROUP:OFF_GP + (g + 1) * POOL_GROUP])
            pool_parts.append((mixed * gate).astype(BF16))
        ret_parts = []
        for h in range(N_HEADS):
            o = o_ref[:, _head(h)]
            rn = o * lax.rsqrt(jnp.mean(o * o, axis=-1, keepdims=True) + EPS)
            gr = proj_ref[:, OFF_GR + h * HEAD_DIM:OFF_GR + (h + 1) * HEAD_DIM]
            ret_parts.append((rn * rnw_ref[:, _head(h)] * _silu(gr)).astype(BF16))
        mix = jnp.concatenate(pool_parts + ret_parts, axis=-1)
        y_ref[...] = _rms(x_ref[...] + _dot(mix, wout_ref[...]), fnw_ref[...])


def _sample_call(x, cosf, sinf, state, pool_t, normw, pscale, rnw, fnw, win_bf, wpool_bf, wout_bf):
    n = x.shape[0]
    nb = SAMPLE_BLOCK

    def const(shape):
        zeros = (0,) * len(shape)
        return pl.BlockSpec(shape, lambda i: zeros, pipeline_mode=pl.Buffered(1))

    return pl.pallas_call(
        _sample_kernel,
        grid=(n // nb,),
        in_specs=[
            const((n, D_MODEL)),
            const((1, HEAD_DIM)),
            const((1, HEAD_DIM)),
            pl.BlockSpec((nb, N_HEADS, HEAD_DIM, HEAD_DIM), lambda i: (i, 0, 0, 0)),
            pl.BlockSpec((POOL_BUF, nb, D_POOL), lambda i: (0, i, 0)),
            const((1, D_MODEL)),
            const((1, D_POOL)),
            const((1, D_RET)),
            const((1, D_MODEL)),
            const((D_MODEL, D_IN_PROJ)),
            const((len(POOL_WINDOWS), POOL_GROUP, POOL_GROUP)),
            const((D_MIX, D_MODEL)),
        ],
        out_specs=[
            pl.BlockSpec((n, D_MODEL), lambda i: (0, 0)),
            pl.BlockSpec((nb, N_HEADS, HEAD_DIM, HEAD_DIM), lambda i: (i, 0, 0, 0)),
            pl.BlockSpec((POOL_BUF, nb, D_POOL), lambda i: (0, i, 0)),
        ],
        out_shape=(jax.ShapeDtypeStruct((n, D_MODEL), F32),
                   jax.ShapeDtypeStruct((n, N_HEADS, HEAD_DIM, HEAD_DIM), F32),
                   jax.ShapeDtypeStruct((POOL_BUF, n, D_POOL), F32)),
        scratch_shapes=[pltpu.VMEM((n, D_IN_PROJ), F32),
                        pltpu.VMEM((n, D_POOL), F32),
                        pltpu.VMEM((n, D_RET), F32)],
        compiler_params=pltpu.CompilerParams(
            dimension_semantics=("arbitrary",), vmem_limit_bytes=VMEM_LIMIT_BYTES),
        name="sample",
    )(x, cosf, sinf, state, pool_t, normw, pscale, rnw, fnw, win_bf, wpool_bf, wout_bf)


def _rotary_tables(pos):
    half = HEAD_DIM // 2
    inv = ROPE_BASE ** (-np.arange(half, dtype=np.float64) / half)
    ang = np.asarray(pos, np.float64)[:, None] * inv[None, :]
    cos, sin = np.cos(ang), np.sin(ang)
    return (jnp.asarray(np.concatenate([cos, cos], axis=-1), F32),
            jnp.asarray(np.concatenate([-sin, sin], axis=-1), F32))


def kernel(x_prompt, x_sample, state_ret, state_pool, meta_tokens, norm_w, w_in, w_pool,
           pool_scale, ret_norm_w, w_out, final_norm_w):
    assert norm_w.shape[0] == 1, "single-layer stack"
    seq = x_prompt.shape[1]
    normw, pscale, rnw = norm_w, pool_scale, ret_norm_w
    fnw = final_norm_w[None, :]

    cos_p, sin_p = _rotary_tables(np.arange(N_META + seq))
    cos_s, sin_s = _rotary_tables(PAST_LEN + np.arange(1))

    win_bf, wout_bf, wpool_bf, smeta, umeta = _prep_call(
        meta_tokens.astype(x_prompt.dtype), cos_p[:N_META], sin_p[:N_META], normw,
        w_in[0], w_out[0], w_pool[0])
    y_p, s_p, buf_p = _prompt_call(x_prompt, cos_p[N_META:], sin_p[N_META:], smeta, umeta,
                                   normw, pscale, rnw, fnw, win_bf, wpool_bf, wout_bf)
    y_s, s_s, buf_s = _sample_call(x_sample[:, 0, :], cos_s, sin_s, state_ret[0],
                                   jnp.transpose(state_pool[0], (1, 0, 2)),
                                   normw, pscale, rnw, fnw, win_bf, wpool_bf, wout_bf)
    return (y_p, y_s[:, None, :], s_p[None], s_s[None], buf_p[None],
            jnp.transpose(buf_s, (1, 0, 2))[None])
```

```python
import math

import jax
import jax.numpy as jnp
import numpy as np
from jax import lax
from jax.experimental import pallas as pl
from jax.experimental.pallas import tpu as pltpu

D_MODEL = 1024
D_POOL = 1024
D_RET = 1024
D_MIX = D_POOL + D_RET
POOL_WINDOWS = (2, 4, 8, 16)
POOL_GROUP = D_POOL // len(POOL_WINDOWS)
POOL_BUF = max(POOL_WINDOWS) - 1
N_HEADS = 8
HEAD_DIM = D_RET // N_HEADS
D_IN_PROJ = 2 * D_POOL + 4 * D_RET
N_META = 16
PAST_LEN = 16384
CHUNK = 128
ROPE_BASE = 10000.0
EPS = 1e-6
K_SCALE = HEAD_DIM ** -0.5

OFF_U, OFF_GP, OFF_Q, OFF_K, OFF_V, OFF_GR = (i * 1024 for i in range(6))

LOG_DECAY = tuple(math.log(1.0 - 2.0 ** (-5.0 - h)) for h in range(N_HEADS))

TILE_N = 256
PROMPT_BLOCK = 4
OUT_PIECES = 2
PREP_TILES = 8
HIST = 16
SAMPLE_BLOCK = 8

VMEM_LIMIT_BYTES = 56 * 1024 * 1024

F32 = jnp.float32
BF16 = jnp.bfloat16


def _rms(x, w):
    return x * lax.rsqrt(jnp.mean(x * x, axis=-1, keepdims=True) + EPS) * w


def _silu(x):
    return x * (1.0 / (1.0 + jnp.exp(-x)))


def _dot(a, b):
    return jnp.dot(a, b, preferred_element_type=F32)


def _rotary(x, cosf, sinf):
    return x * cosf + pltpu.roll(x, HEAD_DIM // 2, 1) * sinf


def _head(h):
    return slice(h * HEAD_DIM, (h + 1) * HEAD_DIM)


def _prep_kernel(meta_ref, cos_ref, sin_ref, normw_ref, win_ref, wout_ref, wpool_ref,
                 winb_ref, woutb_ref, wpoolb_ref, s_ref, u_ref,
                 hb_ref, proj_ref, kd_ref, v_ref):
    j = pl.program_id(0)
    k_rows = D_MODEL // PREP_TILES

    @pl.when(j == 0)
    def _():
        hb = _rms(meta_ref[...], normw_ref[...]).astype(BF16)
        for t in range(PREP_TILES):
            hb_ref[t] = hb[:, t * k_rows:(t + 1) * k_rows]
        wpoolb_ref[...] = wpool_ref[...].astype(BF16)
        proj_ref[...] = jnp.zeros_like(proj_ref)

    tile = win_ref[...].astype(BF16)
    winb_ref[...] = tile
    woutb_ref[...] = wout_ref[...].astype(BF16)
    proj_ref[...] += _dot(hb_ref[j], tile)

    @pl.when(j == pl.num_programs(0) - 1)
    def _():
        cosf, sinf = cos_ref[...], sin_ref[...]
        row = lax.broadcasted_iota(jnp.int32, (N_META, HEAD_DIM), 0).astype(F32)
        kd_ref[...] = jnp.zeros_like(kd_ref)
        v_ref[...] = jnp.zeros_like(v_ref)
        u_ref[...] = proj_ref[:, OFF_U:OFF_U + D_POOL]
        for h in range(N_HEADS):
            kr = _rotary(proj_ref[:, OFF_K + h * HEAD_DIM:OFF_K + (h + 1) * HEAD_DIM], cosf, sinf)
            kd_ref[0:N_META, _head(h)] = kr * K_SCALE * jnp.exp(LOG_DECAY[h] * (N_META - 1.0 - row))
            v_ref[0:N_META, _head(h)] = proj_ref[:, OFF_V + h * HEAD_DIM:OFF_V + (h + 1) * HEAD_DIM]
        for h in range(N_HEADS):
            s_ref[h] = lax.dot_general(
                kd_ref[:, _head(h)].astype(BF16), v_ref[:, _head(h)].astype(BF16),
                (((0,), (0,)), ((), ())), preferred_element_type=F32)


def _prep_call(meta, cosf, sinf, normw, w_in, w_out, w_pool):
    def const(shape):
        zeros = (0,) * len(shape)
        return pl.BlockSpec(shape, lambda j: zeros)

    row_tile = lambda j: (j, 0)
    return pl.pallas_call(
        _prep_kernel,
        grid=(PREP_TILES,),
        in_specs=[
            const((N_META, D_MODEL)),
            const((N_META, HEAD_DIM)),
            const((N_META, HEAD_DIM)),
            const((1, D_MODEL)),
            pl.BlockSpec((D_MODEL // PREP_TILES, D_IN_PROJ), row_tile),
            pl.BlockSpec((D_MIX // PREP_TILES, D_MODEL), row_tile),
            const((len(POOL_WINDOWS), POOL_GROUP, POOL_GROUP)),
        ],
        out_specs=[
            pl.BlockSpec((D_MODEL // PREP_TILES, D_IN_PROJ), row_tile),
            pl.BlockSpec((D_MIX // PREP_TILES, D_MODEL), row_tile),
            const((len(POOL_WINDOWS), POOL_GROUP, POOL_GROUP)),
            const((N_HEADS, HEAD_DIM, HEAD_DIM)),
            const((N_META, D_POOL)),
        ],
        out_shape=(jax.ShapeDtypeStruct((D_MODEL, D_IN_PROJ), BF16),
                   jax.ShapeDtypeStruct((D_MIX, D_MODEL), BF16),
                   jax.ShapeDtypeStruct((len(POOL_WINDOWS), POOL_GROUP, POOL_GROUP), BF16),
                   jax.ShapeDtypeStruct((N_HEADS, HEAD_DIM, HEAD_DIM), F32),
                   jax.ShapeDtypeStruct((N_META, D_POOL), F32)),
        scratch_shapes=[pltpu.VMEM((PREP_TILES, N_META, D_MODEL // PREP_TILES), BF16),
                        pltpu.VMEM((N_META, D_IN_PROJ), F32),
                        pltpu.VMEM((CHUNK, D_RET), F32),
                        pltpu.VMEM((CHUNK, D_RET), F32)],
        compiler_params=pltpu.CompilerParams(
            dimension_semantics=("arbitrary",), vmem_limit_bytes=VMEM_LIMIT_BYTES),
        name="prep",
    )(meta, cosf, sinf, normw, w_in, w_out, w_pool)


def _prompt_kernel(x_ref, cos_ref, sin_ref, smeta_ref, umeta_ref, normw_ref, pscale_ref, rnw_ref,
                   fnw_ref, win_ref, wpool_ref, wout_ref,
                   y_ref, s_ref, pbuf_ref,
                   hist_ref):
    c = pl.program_id(1)
    nb = x_ref.shape[0]
    batches = range(nb)

    @pl.when(c == 0)
    def _():
        for b in batches:
            s_ref[b] = smeta_ref[...]
            hist_ref[b] = umeta_ref[...]

    x = x_ref[...].reshape(nb * CHUNK, D_MODEL)
    hb = _rms(x, normw_ref[...]).astype(BF16)

    def proj(off):
        return _dot(hb, win_ref[:, off:off + TILE_N])

    def rows(t, b):
        return t[b * CHUNK:(b + 1) * CHUNK]

    cosf, sinf = cos_ref[...], sin_ref[...]
    li = lax.broadcasted_iota(jnp.int32, (CHUNK, CHUNK), 0).astype(F32)
    mi = lax.broadcasted_iota(jnp.int32, (CHUNK, CHUNK), 1).astype(F32)
    diff = li - mi

    qkv, part, outs = {}, {}, {}

    def issue_qkv(pair):
        qkv[pair] = tuple(proj(off + pair * TILE_N) for off in (OFF_Q, OFF_K, OFF_V))

    def stage_a(pair):
        q2, k2, v2 = qkv.pop(pair)
        for i in range(2):
            h = 2 * pair + i
            lg = LOG_DECAY[h]
            sub = slice(i * HEAD_DIM, (i + 1) * HEAD_DIM)
            dmask = jnp.where(diff >= 0.0, jnp.exp(lg * jnp.maximum(diff, 0.0)), 0.0)
            q_decay = jnp.exp(lg * (li + 1.0))
            k_decay = jnp.exp(lg * (CHUNK - 1.0 - li))
            for b in batches:
                qr = _rotary(rows(q2, b)[:, sub], cosf, sinf)
                kr = _rotary(rows(k2, b)[:, sub], cosf, sinf) * K_SCALE
                vb = rows(v2, b)[:, sub].astype(BF16)
                state = s_ref[b, h]
                scores = lax.dot_general(qr.astype(BF16), kr.astype(BF16), (((1,), (1,)), ((), ())),
                                         preferred_element_type=F32)
                cross = _dot((qr * q_decay).astype(BF16), state.astype(BF16))
                s_ref[b, h] = math.exp(lg * CHUNK) * state + lax.dot_general(
                    (kr * k_decay).astype(BF16), vb, (((0,), (0,)), ((), ())),
                    preferred_element_type=F32)
                part[b, h] = ((scores * dmask).astype(BF16), vb, cross)

    def stage_b(pair):
        for i in range(2):
            h = 2 * pair + i
            for b in batches:
                p, vb, cross = part.pop((b, h))
                outs[b, h] = _dot(p, vb) + cross

    def pool_windows(u_tiles):
        pooled = []
        for g, w in enumerate(POOL_WINDOWS):
            cols = slice(g * POOL_GROUP, (g + 1) * POOL_GROUP)
            per_batch = []
            for b in batches:
                u = rows(u_tiles[g], b)
                ext = jnp.concatenate([hist_ref[b, :, cols], u], axis=0)
                win_sum = ext
                shift = 1
                while shift < w:
                    win_sum = win_sum + pltpu.roll(win_sum, shift, 0)
                    shift *= 2
                per_batch.append((win_sum[HIST:] / float(w) - u).astype(BF16))
                hist_ref[b, :, cols] = u[CHUNK - HIST:]
            pooled.append(jnp.concatenate(per_batch, axis=0))
        return pooled

    def pool_dots(pooled):
        return [_dot(pooled[g], wpool_ref[g]) * pscale_ref[:, g * POOL_GROUP:(g + 1) * POOL_GROUP]
                for g in range(len(POOL_WINDOWS))]

    issue_qkv(0)
    issue_qkv(1)
    u_tiles = [proj(OFF_U + g * POOL_GROUP) for g in range(len(POOL_WINDOWS))]
    stage_a(0)
    issue_qkv(2)
    pooled = pool_windows(u_tiles)
    stage_a(1)
    stage_b(0)
    gp_tiles = [proj(OFF_GP + g * POOL_GROUP) for g in range(len(POOL_WINDOWS))]
    issue_qkv(3)
    stage_a(2)
    stage_b(1)
    mixed = pool_dots(pooled)
    stage_a(3)
    stage_b(2)
    gr_tiles = [proj(OFF_GR + pair * TILE_N) for pair in range(N_HEADS // 2)]
    stage_b(3)
    pool_mix = jnp.concatenate(
        [(mixed[g] * _silu(gp_tiles[g])).astype(BF16) for g in range(len(POOL_WINDOWS))], axis=-1)
    acc = x + _dot(pool_mix, wout_ref[0:D_POOL, :])

    per_piece = max(nb // OUT_PIECES, 1)
    for first in range(0, nb, per_piece):
        piece = range(first, first + per_piece)
        ret_y = []
        for b in piece:
            heads = []
            for h in range(N_HEADS):
                sub = slice((h % 2) * HEAD_DIM, (h % 2 + 1) * HEAD_DIM)
                o = outs.pop((b, h))
                rn = o * lax.rsqrt(jnp.mean(o * o, axis=-1, keepdims=True) + EPS)
                gate = _silu(rows(gr_tiles[h // 2], b)[:, sub])
                heads.append((rn * rnw_ref[:, _head(h)] * gate).astype(BF16))
            ret_y.append(jnp.concatenate(heads, axis=-1))
        ret_mix = jnp.concatenate(ret_y, axis=0)
        acc_piece = acc[first * CHUNK:(first + per_piece) * CHUNK]
        acc_piece = acc_piece + _dot(ret_mix, wout_ref[D_POOL:D_MIX, :])
        y_ref[first:first + per_piece] = _rms(acc_piece, fnw_ref[...]).reshape(
            per_piece, CHUNK, D_MODEL)

    @pl.when(c == pl.num_programs(1) - 1)
    def _():
        for b in batches:
            pbuf_ref[b] = hist_ref[b, HIST - POOL_BUF:HIST, :]


def _const_spec(shape):
    zeros = (0,) * len(shape)
    return pl.BlockSpec(shape, lambda b, c: zeros, pipeline_mode=pl.Buffered(1))


def _prompt_call(x, cosf, sinf, smeta, umeta, normw, pscale, rnw, fnw, win_bf, wpool_bf, wout_bf):
    batch, seq, _ = x.shape
    nb = PROMPT_BLOCK
    return pl.pallas_call(
        _prompt_kernel,
        grid=(batch // nb, seq // CHUNK),
        in_specs=[
            pl.BlockSpec((nb, CHUNK, D_MODEL), lambda b, c: (b, c, 0)),
            pl.BlockSpec((CHUNK, HEAD_DIM), lambda b, c: (c, 0)),
            pl.BlockSpec((CHUNK, HEAD_DIM), lambda b, c: (c, 0)),
            _const_spec((N_HEADS, HEAD_DIM, HEAD_DIM)),
            _const_spec((N_META, D_POOL)),
            _const_spec((1, D_MODEL)),
            _const_spec((1, D_POOL)),
            _const_spec((1, D_RET)),
            _const_spec((1, D_MODEL)),
            _const_spec((D_MODEL, D_IN_PROJ)),
            _const_spec((len(POOL_WINDOWS), POOL_GROUP, POOL_GROUP)),
            _const_spec((D_MIX, D_MODEL)),
        ],
        out_specs=[
            pl.BlockSpec((nb, CHUNK, D_MODEL), lambda b, c: (b, c, 0)),
            pl.BlockSpec((nb, N_HEADS, HEAD_DIM, HEAD_DIM), lambda b, c: (b, 0, 0, 0)),
            pl.BlockSpec((nb, POOL_BUF, D_POOL), lambda b, c: (b, 0, 0)),
        ],
        out_shape=(jax.ShapeDtypeStruct((batch, seq, D_MODEL), F32),
                   jax.ShapeDtypeStruct((batch, N_HEADS, HEAD_DIM, HEAD_DIM), F32),
                   jax.ShapeDtypeStruct((batch, POOL_BUF, D_POOL), F32)),
        scratch_shapes=[pltpu.VMEM((nb, HIST, D_POOL), F32)],
        compiler_params=pltpu.CompilerParams(
            dimension_semantics=("arbitrary", "arbitrary"), vmem_limit_bytes=VMEM_LIMIT_BYTES),
        name="prompt",
    )(x, cosf, sinf, smeta, umeta, normw, pscale, rnw, fnw, win_bf, wpool_bf, wout_bf)


def _sample_kernel(x_ref, cos_ref, sin_ref, state_ref, pin_ref, normw_ref, pscale_ref, rnw_ref,
                   fnw_ref, win_ref, wpool_ref, wout_ref,
                   y_ref, sout_ref, pout_ref,
                   proj_ref, pooled_ref, o_ref):
    i = pl.program_id(0)
    nb = SAMPLE_BLOCK

    @pl.when(i == 0)
    def _():
        hb = _rms(x_ref[...], normw_ref[...]).astype(BF16)
        proj_ref[...] = _dot(hb, win_ref[...])

    rows = pl.ds(pl.multiple_of(i * nb, nb), nb)
    u8 = proj_ref[rows, OFF_U:OFF_U + D_POOL]

    pooled = []
    for g, w in enumerate(POOL_WINDOWS):
        cols = slice(g * POOL_GROUP, (g + 1) * POOL_GROUP)
        win_sum = u8[:, cols]
        for r in range(POOL_BUF - (w - 1), POOL_BUF):
            win_sum = win_sum + pin_ref[r, :, cols]
        pooled.append(win_sum / float(w) - u8[:, cols])
    pooled_ref[rows, :] = jnp.concatenate(pooled, axis=-1)
    pout_ref[0:POOL_BUF - 1] = pin_ref[1:POOL_BUF]
    pout_ref[POOL_BUF - 1] = u8

    cosf, sinf = cos_ref[...], sin_ref[...]
    seq_of_row = lax.broadcasted_iota(jnp.int32, (nb, nb * HEAD_DIM), 0)
    seq_of_col = lax.broadcasted_iota(jnp.int32, (nb, nb * HEAD_DIM), 1) // HEAD_DIM
    own_block = seq_of_row == seq_of_col
    row_id = lax.broadcasted_iota(jnp.int32, (nb, HEAD_DIM), 0)
    for h in range(N_HEADS):
        g1 = math.exp(LOG_DECAY[h])
        q8 = _rotary(proj_ref[rows, OFF_Q + h * HEAD_DIM:OFF_Q + (h + 1) * HEAD_DIM], cosf, sinf)
        k8 = _rotary(proj_ref[rows, OFF_K + h * HEAD_DIM:OFF_K + (h + 1) * HEAD_DIM], cosf, sinf) * K_SCALE
        v8 = proj_ref[rows, OFF_V + h * HEAD_DIM:OFF_V + (h + 1) * HEAD_DIM]
        state = state_ref[:, h]
        q_blocks = jnp.where(own_block, jnp.concatenate([q8 * g1] * nb, axis=-1), 0.0)
        cross8 = _dot(q_blocks.astype(BF16), state.reshape(nb * HEAD_DIM, HEAD_DIM).astype(BF16))
        o_ref[rows, _head(h)] = jnp.sum(q8 * k8, axis=-1, keepdims=True) * v8 + cross8
        kb = k8.astype(BF16)
        for j in range(0, nb, 2):
            v_pair = jnp.concatenate([jnp.where(row_id == j, v8, 0.0),
                                      jnp.where(row_id == j + 1, v8, 0.0)], axis=-1).astype(BF16)
            upd = lax.dot_general(kb, v_pair, (((0,), (0,)), ((), ())), preferred_element_type=F32)
            sout_ref[j, h] = g1 * state[j] + upd[:, 0:HEAD_DIM]
            sout_ref[j + 1, h] = g1 * state[j + 1] + upd[:, HEAD_DIM:2 * HEAD_DIM]

    @pl.when(i == pl.num_programs(0) - 1)
    def _():
        pool_parts = []
        for g in range(len(POOL_WINDOWS)):
            cols = slice(g * POOL_GROUP, (g + 1) * POOL_GROUP)
            mixed = _dot(pooled_ref[:, cols].astype(BF16), wpool_ref[g]) * pscale_ref[:, cols]
            gate = _silu(proj_ref[:, OFF_GP + g * POOL_GROUP:OFF_GP + (g + 1) * POOL_GROUP])
            pool_parts.append((mixed * gate).astype(BF16))
        ret_parts = []
        for h in range(N_HEADS):
            o = o_ref[:, _head(h)]
            rn = o * lax.rsqrt(jnp.mean(o * o, axis=-1, keepdims=True) + EPS)
            gr = proj_ref[:, OFF_GR + h * HEAD_DIM:OFF_GR + (h + 1) * HEAD_DIM]
            ret_parts.append((rn * rnw_ref[:, _head(h)] * _silu(gr)).astype(BF16))
        mix = jnp.concatenate(pool_parts + ret_parts, axis=-1)
        y_ref[...] = _rms(x_ref[...] + _dot(mix, wout_ref[...]), fnw_ref[...])


def _sample_call(x, cosf, sinf, state, pool_t, normw, pscale, rnw, fnw, win_bf, wpool_bf, wout_bf):
    n = x.shape[0]
    nb = SAMPLE_BLOCK

    def const(shape):
        zeros = (0,) * len(shape)
        return pl.BlockSpec(shape, lambda i: zeros, pipeline_mode=pl.Buffered(1))

    return pl.pallas_call(
        _sample_kernel,
        grid=(n // nb,),
        in_specs=[
            const((n, D_MODEL)),
            const((1, HEAD_DIM)),
            const((1, HEAD_DIM)),
            pl.BlockSpec((nb, N_HEADS, HEAD_DIM, HEAD_DIM), lambda i: (i, 0, 0, 0)),
            pl.BlockSpec((POOL_BUF, nb, D_POOL), lambda i: (0, i, 0)),
            const((1, D_MODEL)),
            const((1, D_POOL)),
            const((1, D_RET)),
            const((1, D_MODEL)),
            const((D_MODEL, D_IN_PROJ)),
            const((len(POOL_WINDOWS), POOL_GROUP, POOL_GROUP)),
            const((D_MIX, D_MODEL)),
        ],
        out_specs=[
            pl.BlockSpec((n, D_MODEL), lambda i: (0, 0)),
            pl.BlockSpec((nb, N_HEADS, HEAD_DIM, HEAD_DIM), lambda i: (i, 0, 0, 0)),
            pl.BlockSpec((POOL_BUF, nb, D_POOL), lambda i: (0, i, 0)),
        ],
        out_shape=(jax.ShapeDtypeStruct((n, D_MODEL), F32),
                   jax.ShapeDtypeStruct((n, N_HEADS, HEAD_DIM, HEAD_DIM), F32),
                   jax.ShapeDtypeStruct((POOL_BUF, n, D_POOL), F32)),
        scratch_shapes=[pltpu.VMEM((n, D_IN_PROJ), F32),
                        pltpu.VMEM((n, D_POOL), F32),
                        pltpu.VMEM((n, D_RET), F32)],
        compiler_params=pltpu.CompilerParams(
            dimension_semantics=("arbitrary",), vmem_limit_bytes=VMEM_LIMIT_BYTES),
        name="sample",
    )(x, cosf, sinf, state, pool_t, normw, pscale, rnw, fnw, win_bf, wpool_bf, wout_bf)


def _rotary_tables(pos):
    half = HEAD_DIM // 2
    inv = ROPE_BASE ** (-np.arange(half, dtype=np.float64) / half)
    ang = np.asarray(pos, np.float64)[:, None] * inv[None, :]
    cos, sin = np.cos(ang), np.sin(ang)
    return (jnp.asarray(np.concatenate([cos, cos], axis=-1), F32),
            jnp.asarray(np.concatenate([-sin, sin], axis=-1), F32))


def kernel(x_prompt, x_sample, state_ret, state_pool, meta_tokens, norm_w, w_in, w_pool,
           pool_scale, ret_norm_w, w_out, final_norm_w):
    assert norm_w.shape[0] == 1, "single-layer stack"
    seq = x_prompt.shape[1]
    normw, pscale, rnw = norm_w, pool_scale, ret_norm_w
    fnw = final_norm_w[None, :]

    cos_p, sin_p = _rotary_tables(np.arange(N_META + seq))
    cos_s, sin_s = _rotary_tables(PAST_LEN + np.arange(1))

    win_bf, wout_bf, wpool_bf, smeta, umeta = _prep_call(
        meta_tokens.astype(x_prompt.dtype), cos_p[:N_META], sin_p[:N_META], normw,
        w_in[0], w_out[0], w_pool[0])
    y_p, s_p, buf_p = _prompt_call(x_prompt, cos_p[N_META:], sin_p[N_META:], smeta, umeta,
                                   normw, pscale, rnw, fnw, win_bf, wpool_bf, wout_bf)
    y_s, s_s, buf_s = _sample_call(x_sample[:, 0, :], cos_s, sin_s, state_ret[0],
                                   jnp.transpose(state_pool[0], (1, 0, 2)),
                                   normw, pscale, rnw, fnw, win_bf, wpool_bf, wout_bf)
    return (y_p, y_s[:, None, :], s_p[None], s_s[None], buf_p[None],
            jnp.transpose(buf_s, (1, 0, 2))[None])
```

```python
import math

import jax
import jax.numpy as jnp
import numpy as np
from jax import lax
from jax.experimental import pallas as pl
from jax.experimental.pallas import tpu as pltpu

D_MODEL = 1024
D_POOL = 1024
D_RET = 1024
D_MIX = D_POOL + D_RET
POOL_WINDOWS = (2, 4, 8, 16)
POOL_GROUP = D_POOL // len(POOL_WINDOWS)
POOL_BUF = max(POOL_WINDOWS) - 1
N_HEADS = 8
HEAD_DIM = D_RET // N_HEADS
D_IN_PROJ = 2 * D_POOL + 4 * D_RET
N_META = 16
PAST_LEN = 16384
CHUNK = 128
ROPE_BASE = 10000.0
EPS = 1e-6
K_SCALE = HEAD_DIM ** -0.5

OFF_U, OFF_GP, OFF_Q, OFF_K, OFF_V, OFF_GR = (i * 1024 for i in range(6))

LOG_DECAY = tuple(math.log(1.0 - 2.0 ** (-5.0 - h)) for h in range(N_HEADS))

SUBLANES = 8
TILE_N = 256
PROMPT_BLOCK = 4
OUT_PIECES = 2
PREP_TILES = 8
HIST = 16

VMEM_LIMIT_BYTES = 56 * 1024 * 1024

F32 = jnp.float32
BF16 = jnp.bfloat16


def _rms(x, w):
    return x * lax.rsqrt(jnp.mean(x * x, axis=-1, keepdims=True) + EPS) * w


def _silu(x):
    return x * (1.0 / (1.0 + jnp.exp(-x)))


def _dot(a, b):
    return jnp.dot(a, b, preferred_element_type=F32)


def _dot_t(a, b):
    return lax.dot_general(a, b, (((0,), (0,)), ((), ())), preferred_element_type=F32)


def _rotary(x, cosf, sinf):
    return x * cosf + pltpu.roll(x, HEAD_DIM // 2, 1) * sinf


def _head(h):
    return slice(h * HEAD_DIM, (h + 1) * HEAD_DIM)


def _prep_kernel(meta_ref, xs_ref, cos_ref, sin_ref, coss_ref, sins_ref, normw_ref,
                 win_ref, wout_ref, wpool_ref,
                 winb_ref, woutb_ref, wpoolb_ref, s_ref, u_ref, qkv_ref, rest_ref,
                 hb_ref, hs_ref, proj_ref, projs_ref, kd_ref, v_ref, stage_ref):
    j = pl.program_id(0)
    k_rows = D_MODEL // PREP_TILES

    @pl.when(j == 0)
    def _():
        hb = _rms(meta_ref[...], normw_ref[...]).astype(BF16)
        hs = _rms(xs_ref[...], normw_ref[...]).astype(BF16)
        for t in range(PREP_TILES):
            hb_ref[t] = hb[:, t * k_rows:(t + 1) * k_rows]
            hs_ref[t] = hs[:, t * k_rows:(t + 1) * k_rows]
        wpoolb_ref[...] = wpool_ref[...].astype(BF16)
        proj_ref[...] = jnp.zeros_like(proj_ref)
        projs_ref[...] = jnp.zeros_like(projs_ref)

    tile = win_ref[...].astype(BF16)
    winb_ref[...] = tile
    woutb_ref[...] = wout_ref[...].astype(BF16)
    proj_ref[...] += _dot(hb_ref[j], tile)
    projs_ref[...] += _dot(hs_ref[j], tile)

    @pl.when(j == pl.num_programs(0) - 1)
    def _():
        cosf, sinf = cos_ref[...], sin_ref[...]
        row = lax.broadcasted_iota(jnp.int32, (N_META, HEAD_DIM), 0).astype(F32)
        kd_ref[...] = jnp.zeros_like(kd_ref)
        v_ref[...] = jnp.zeros_like(v_ref)
        u_ref[...] = proj_ref[:, OFF_U:OFF_U + D_POOL]
        for h in range(N_HEADS):
            kr = _rotary(proj_ref[:, OFF_K + h * HEAD_DIM:OFF_K + (h + 1) * HEAD_DIM], cosf, sinf)
            kd_ref[0:N_META, _head(h)] = kr * K_SCALE * jnp.exp(LOG_DECAY[h] * (N_META - 1.0 - row))
            v_ref[0:N_META, _head(h)] = proj_ref[:, OFF_V + h * HEAD_DIM:OFF_V + (h + 1) * HEAD_DIM]
        for h in range(N_HEADS):
            s_ref[h] = _dot_t(kd_ref[:, _head(h)].astype(BF16), v_ref[:, _head(h)].astype(BF16))

        coss, sins = coss_ref[...], sins_ref[...]
        rest_ref[:, 0:D_POOL] = projs_ref[:, OFF_U:OFF_U + D_POOL]
        rest_ref[:, D_POOL:2 * D_POOL] = projs_ref[:, OFF_GP:OFF_GP + D_POOL]
        rest_ref[:, 2 * D_POOL:2 * D_POOL + D_RET] = projs_ref[:, OFF_GR:OFF_GR + D_RET]
        for h in range(N_HEADS):
            qr = _rotary(projs_ref[:, OFF_Q + h * HEAD_DIM:OFF_Q + (h + 1) * HEAD_DIM], coss, sins)
            kr = _rotary(projs_ref[:, OFF_K + h * HEAD_DIM:OFF_K + (h + 1) * HEAD_DIM], coss, sins)
            kr = kr * K_SCALE
            v = projs_ref[:, OFF_V + h * HEAD_DIM:OFF_V + (h + 1) * HEAD_DIM]
            stage_ref[:, _head(h)] = qr * math.exp(LOG_DECAY[h])
            stage_ref[:, D_RET + h * HEAD_DIM:D_RET + (h + 1) * HEAD_DIM] = kr
            stage_ref[:, 2 * D_RET + h * HEAD_DIM:2 * D_RET + (h + 1) * HEAD_DIM] = v
            rest_ref[:, 2 * D_POOL + D_RET + h * HEAD_DIM:2 * D_POOL + D_RET + (h + 1) * HEAD_DIM] = (
                jnp.sum(qr * kr, axis=-1, keepdims=True) * v)
        n_steps, per = qkv_ref.shape[0], stage_ref.shape[0] // qkv_ref.shape[0]
        qkv_ref[...] = jnp.zeros_like(qkv_ref)
        for i in range(n_steps):
            qkv_ref[i, 0:per, :] = stage_ref[i * per:(i + 1) * per, :]


def _prep_call(meta, xs, cosf, sinf, coss, sins, normw, w_in, w_out, w_pool, n_steps):
    n = xs.shape[0]

    def const(shape):
        zeros = (0,) * len(shape)
        return pl.BlockSpec(shape, lambda j: zeros)

    row_tile = lambda j: (j, 0)
    return pl.pallas_call(
        _prep_kernel,
        grid=(PREP_TILES,),
        in_specs=[
            const((N_META, D_MODEL)),
            const((n, D_MODEL)),
            const((N_META, HEAD_DIM)),
            const((N_META, HEAD_DIM)),
            const((1, HEAD_DIM)),
            const((1, HEAD_DIM)),
            const((1, D_MODEL)),
            pl.BlockSpec((D_MODEL // PREP_TILES, D_IN_PROJ), row_tile),
            pl.BlockSpec((D_MIX // PREP_TILES, D_MODEL), row_tile),
            const((len(POOL_WINDOWS), POOL_GROUP, POOL_GROUP)),
        ],
        out_specs=[
            pl.BlockSpec((D_MODEL // PREP_TILES, D_IN_PROJ), row_tile),
            pl.BlockSpec((D_MIX // PREP_TILES, D_MODEL), row_tile),
            const((len(POOL_WINDOWS), POOL_GROUP, POOL_GROUP)),
            const((N_HEADS, HEAD_DIM, HEAD_DIM)),
            const((N_META, D_POOL)),
            const((n_steps, SUBLANES, 3 * D_RET)),
            const((n, 2 * D_POOL + 2 * D_RET)),
        ],
        out_shape=(jax.ShapeDtypeStruct((D_MODEL, D_IN_PROJ), BF16),
                   jax.ShapeDtypeStruct((D_MIX, D_MODEL), BF16),
                   jax.ShapeDtypeStruct((len(POOL_WINDOWS), POOL_GROUP, POOL_GROUP), BF16),
                   jax.ShapeDtypeStruct((N_HEADS, HEAD_DIM, HEAD_DIM), F32),
                   jax.ShapeDtypeStruct((N_META, D_POOL), F32),
                   jax.ShapeDtypeStruct((n_steps, SUBLANES, 3 * D_RET), F32),
                   jax.ShapeDtypeStruct((n, 2 * D_POOL + 2 * D_RET), F32)),
        scratch_shapes=[pltpu.VMEM((PREP_TILES, N_META, D_MODEL // PREP_TILES), BF16),
                        pltpu.VMEM((PREP_TILES, n, D_MODEL // PREP_TILES), BF16),
                        pltpu.VMEM((N_META, D_IN_PROJ), F32),
                        pltpu.VMEM((n, D_IN_PROJ), F32),
                        pltpu.VMEM((CHUNK, D_RET), F32),
                        pltpu.VMEM((CHUNK, D_RET), F32),
                        pltpu.VMEM((n, 3 * D_RET), F32)],
        compiler_params=pltpu.CompilerParams(
            dimension_semantics=("arbitrary",), vmem_limit_bytes=VMEM_LIMIT_BYTES),
        name="prep",
    )(meta, xs, cosf, sinf, coss, sins, normw, w_in, w_out, w_pool)


def _sample_state_update(qkv_ref, state_ref, sout_ref, o_ref):
    per = state_ref.shape[0]
    seq_of_row = lax.broadcasted_iota(jnp.int32, (SUBLANES, per * HEAD_DIM), 0)
    seq_of_col = lax.broadcasted_iota(jnp.int32, (SUBLANES, per * HEAD_DIM), 1) // HEAD_DIM
    own_block = seq_of_row == seq_of_col
    row_id = lax.broadcasted_iota(jnp.int32, (SUBLANES, HEAD_DIM), 0)
    for h in range(N_HEADS):
        g1 = math.exp(LOG_DECAY[h])
        qd = qkv_ref[:, _head(h)]
        kb = qkv_ref[:, D_RET + h * HEAD_DIM:D_RET + (h + 1) * HEAD_DIM].astype(BF16)
        v8 = qkv_ref[:, 2 * D_RET + h * HEAD_DIM:2 * D_RET + (h + 1) * HEAD_DIM]
        state = state_ref[:, h]
        q_blocks = jnp.where(own_block, jnp.concatenate([qd] * per, axis=-1), 0.0)
        o_ref[:, _head(h)] = _dot(q_blocks.astype(BF16),
                                  state.reshape(per * HEAD_DIM, HEAD_DIM).astype(BF16))
        for j in range(0, per, 2):
            v_pair = jnp.concatenate([jnp.where(row_id == j, v8, 0.0),
                                      jnp.where(row_id == j + 1, v8, 0.0)], axis=-1).astype(BF16)
            upd = _dot_t(kb, v_pair)
            sout_ref[j, h] = g1 * state[j] + upd[:, 0:HEAD_DIM]
            sout_ref[j + 1, h] = g1 * state[j + 1] + upd[:, HEAD_DIM:2 * HEAD_DIM]


def _main_kernel(x_ref, cos_ref, sin_ref, smeta_ref, umeta_ref, normw_ref, pscale_ref, rnw_ref,
                 fnw_ref, win_ref, wpool_ref, wout_ref, qkv_ref, sstate_ref,
                 y_ref, s_ref, pbuf_ref, sout_ref, so_ref,
                 hist_ref):
    c = pl.program_id(1)
    nb = x_ref.shape[0]
    batches = range(nb)

    @pl.when(c == 0)
    def _():
        for b in batches:
            s_ref[b] = smeta_ref[...]
            hist_ref[b] = umeta_ref[...]

    x = x_ref[...].reshape(nb * CHUNK, D_MODEL)
    hb = _rms(x, normw_ref[...]).astype(BF16)

    def proj(off):
        return _dot(hb, win_ref[:, off:off + TILE_N])

    def rows(t, b):
        return t[b * CHUNK:(b + 1) * CHUNK]

    cosf, sinf = cos_ref[...], sin_ref[...]
    li = lax.broadcasted_iota(jnp.int32, (CHUNK, CHUNK), 0).astype(F32)
    mi = lax.broadcasted_iota(jnp.int32, (CHUNK, CHUNK), 1).astype(F32)
    diff = li - mi

    qkv, part, outs = {}, {}, {}

    def issue_qkv(pair):
        qkv[pair] = tuple(proj(off + pair * TILE_N) for off in (OFF_Q, OFF_K, OFF_V))

    def stage_a(pair):
        q2, k2, v2 = qkv.pop(pair)
        for i in range(2):
            h = 2 * pair + i
            lg = LOG_DECAY[h]
            sub = slice(i * HEAD_DIM, (i + 1) * HEAD_DIM)
            dmask = jnp.where(diff >= 0.0, jnp.exp(lg * jnp.maximum(diff, 0.0)), 0.0)
            q_decay = jnp.exp(lg * (li + 1.0))
            k_decay = jnp.exp(lg * (CHUNK - 1.0 - li))
            for b in batches:
                qr = _rotary(rows(q2, b)[:, sub], cosf, sinf)
                kr = _rotary(rows(k2, b)[:, sub], cosf, sinf) * K_SCALE
                vb = rows(v2, b)[:, sub].astype(BF16)
                state = s_ref[b, h]
                scores = lax.dot_general(qr.astype(BF16), kr.astype(BF16), (((1,), (1,)), ((), ())),
                                         preferred_element_type=F32)
                cross = _dot((qr * q_decay).astype(BF16), state.astype(BF16))
                s_ref[b, h] = math.exp(lg * CHUNK) * state + _dot_t((kr * k_decay).astype(BF16), vb)
                part[b, h] = ((scores * dmask).astype(BF16), vb, cross)

    def stage_b(pair):
        for i in range(2):
            h = 2 * pair + i
            for b in batches:
                p, vb, cross = part.pop((b, h))
                outs[b, h] = _dot(p, vb) + cross

    def pool_windows(u_tiles):
        pooled = []
        for g, w in enumerate(POOL_WINDOWS):
            cols = slice(g * POOL_GROUP, (g + 1) * POOL_GROUP)
            per_batch = []
            for b in batches:
                u = rows(u_tiles[g], b)
                ext = jnp.concatenate([hist_ref[b, :, cols], u], axis=0)
                win_sum = ext
                shift = 1
                while shift < w:
                    win_sum = win_sum + pltpu.roll(win_sum, shift, 0)
                    shift *= 2
                per_batch.append((win_sum[HIST:] / float(w) - u).astype(BF16))
                hist_ref[b, :, cols] = u[CHUNK - HIST:]
            pooled.append(jnp.concatenate(per_batch, axis=0))
        return pooled

    def pool_dots(pooled):
        return [_dot(pooled[g], wpool_ref[g]) * pscale_ref[:, g * POOL_GROUP:(g + 1) * POOL_GROUP]
                for g in range(len(POOL_WINDOWS))]

    issue_qkv(0)
    issue_qkv(1)
    u_tiles = [proj(OFF_U + g * POOL_GROUP) for g in range(len(POOL_WINDOWS))]
    stage_a(0)
    issue_qkv(2)
    pooled = pool_windows(u_tiles)
    stage_a(1)
    stage_b(0)
    gp_tiles = [proj(OFF_GP + g * POOL_GROUP) for g in range(len(POOL_WINDOWS))]
    issue_qkv(3)
    stage_a(2)
    stage_b(1)
    mixed = pool_dots(pooled)
    _sample_state_update(qkv_ref, sstate_ref, sout_ref, so_ref)
    stage_a(3)
    stage_b(2)
    gr_tiles = [proj(OFF_GR + pair * TILE_N) for pair in range(N_HEADS // 2)]
    stage_b(3)
    pool_mix = jnp.concatenate(
        [(mixed[g] * _silu(gp_tiles[g])).astype(BF16) for g in range(len(POOL_WINDOWS))], axis=-1)
    acc = x + _dot(pool_mix, wout_ref[0:D_POOL, :])

    per_piece = max(nb // OUT_PIECES, 1)
    for first in range(0, nb, per_piece):
        piece = range(first, first + per_piece)
        ret_y = []
        for b in piece:
            heads = []
            for h in range(N_HEADS):
                sub = slice((h % 2) * HEAD_DIM, (h % 2 + 1) * HEAD_DIM)
                o = outs.pop((b, h))
                rn = o * lax.rsqrt(jnp.mean(o * o, axis=-1, keepdims=True) + EPS)
                gate = _silu(rows(gr_tiles[h // 2], b)[:, sub])
                heads.append((rn * rnw_ref[:, _head(h)] * gate).astype(BF16))
            ret_y.append(jnp.concatenate(heads, axis=-1))
        ret_mix = jnp.concatenate(ret_y, axis=0)
        acc_piece = acc[first * CHUNK:(first + per_piece) * CHUNK]
        acc_piece = acc_piece + _dot(ret_mix, wout_ref[D_POOL:D_MIX, :])
        y_ref[first:first + per_piece] = _rms(acc_piece, fnw_ref[...]).reshape(
            per_piece, CHUNK, D_MODEL)

    @pl.when(c == pl.num_programs(1) - 1)
    def _():
        for b in batches:
            pbuf_ref[b] = hist_ref[b, HIST - POOL_BUF:HIST, :]


def _main_call(x, cosf, sinf, smeta, umeta, normw, pscale, rnw, fnw, win_bf, wpool_bf, wout_bf,
               qkv_s, state_s):
    batch, seq, _ = x.shape
    nb = PROMPT_BLOCK
    n_chunks = seq // CHUNK
    n_sample = state_s.shape[0]
    per = n_sample // ((batch // nb) * n_chunks)

    def const(shape):
        zeros = (0,) * len(shape)
        return pl.BlockSpec(shape, lambda b, c: zeros, pipeline_mode=pl.Buffered(1))

    step = lambda b, c: b * n_chunks + c
    return pl.pallas_call(
        _main_kernel,
        grid=(batch // nb, n_chunks),
        in_specs=[
            pl.BlockSpec((nb, CHUNK, D_MODEL), lambda b, c: (b, c, 0)),
            pl.BlockSpec((CHUNK, HEAD_DIM), lambda b, c: (c, 0)),
            pl.BlockSpec((CHUNK, HEAD_DIM), lambda b, c: (c, 0)),
            const((N_HEADS, HEAD_DIM, HEAD_DIM)),
            const((N_META, D_POOL)),
            const((1, D_MODEL)),
            const((1, D_POOL)),
            const((1, D_RET)),
            const((1, D_MODEL)),
            const((D_MODEL, D_IN_PROJ)),
            const((len(POOL_WINDOWS), POOL_GROUP, POOL_GROUP)),
            const((D_MIX, D_MODEL)),
            pl.BlockSpec((None, SUBLANES, 3 * D_RET), lambda b, c: (step(b, c), 0, 0)),
            pl.BlockSpec((per, N_HEADS, HEAD_DIM, HEAD_DIM), lambda b, c: (step(b, c), 0, 0, 0)),
        ],
        out_specs=[
            pl.BlockSpec((nb, CHUNK, D_MODEL), lambda b, c: (b, c, 0)),
            pl.BlockSpec((nb, N_HEADS, HEAD_DIM, HEAD_DIM), lambda b, c: (b, 0, 0, 0)),
            pl.BlockSpec((nb, POOL_BUF, D_POOL), lambda b, c: (b, 0, 0)),
            pl.BlockSpec((per, N_HEADS, HEAD_DIM, HEAD_DIM), lambda b, c: (step(b, c), 0, 0, 0)),
            pl.BlockSpec((None, SUBLANES, D_RET), lambda b, c: (step(b, c), 0, 0)),
        ],
        out_shape=(jax.ShapeDtypeStruct((batch, seq, D_MODEL), F32),
                   jax.ShapeDtypeStruct((batch, N_HEADS, HEAD_DIM, HEAD_DIM), F32),
                   jax.ShapeDtypeStruct((batch, POOL_BUF, D_POOL), F32),
                   jax.ShapeDtypeStruct((n_sample, N_HEADS, HEAD_DIM, HEAD_DIM), F32),
                   jax.ShapeDtypeStruct((qkv_s.shape[0], SUBLANES, D_RET), F32)),
        scratch_shapes=[pltpu.VMEM((nb, HIST, D_POOL), F32)],
        compiler_params=pltpu.CompilerParams(
            dimension_semantics=("arbitrary", "arbitrary"), vmem_limit_bytes=VMEM_LIMIT_BYTES),
        name="main",
    )(x, cosf, sinf, smeta, umeta, normw, pscale, rnw, fnw, win_bf, wpool_bf, wout_bf,
      qkv_s, state_s)


def _tail_kernel(xs_ref, rest_ref, so_ref, pin_ref, pscale_ref, rnw_ref, fnw_ref, wpool_ref,
                 wout_ref, y_ref, pout_ref, o_ref):
    n_steps = so_ref.shape[0]
    per = xs_ref.shape[0] // n_steps
    for i in range(n_steps):
        o_ref[i * per:(i + 1) * per, :] = so_ref[i, 0:per, :]

    u = rest_ref[:, 0:D_POOL]
    pout_ref[0:POOL_BUF - 1] = pin_ref[1:POOL_BUF]
    pout_ref[POOL_BUF - 1] = u

    parts = []
    for g, w in enumerate(POOL_WINDOWS):
        cols = slice(g * POOL_GROUP, (g + 1) * POOL_GROUP)
        win_sum = u[:, cols]
        for r in range(POOL_BUF - (w - 1), POOL_BUF):
            win_sum = win_sum + pin_ref[r, :, cols]
        pooled = win_sum / float(w) - u[:, cols]
        mixed = _dot(pooled.astype(BF16), wpool_ref[g]) * pscale_ref[:, cols]
        gate = _silu(rest_ref[:, D_POOL + g * POOL_GROUP:D_POOL + (g + 1) * POOL_GROUP])
        parts.append((mixed * gate).astype(BF16))
    for h in range(N_HEADS):
        o = o_ref[:, _head(h)] + rest_ref[:, 2 * D_POOL + D_RET + h * HEAD_DIM:
                                          2 * D_POOL + D_RET + (h + 1) * HEAD_DIM]
        rn = o * lax.rsqrt(jnp.mean(o * o, axis=-1, keepdims=True) + EPS)
        gr = rest_ref[:, 2 * D_POOL + h * HEAD_DIM:2 * D_POOL + (h + 1) * HEAD_DIM]
        parts.append((rn * rnw_ref[:, _head(h)] * _silu(gr)).astype(BF16))
    mix = jnp.concatenate(parts, axis=-1)
    y_ref[...] = _rms(xs_ref[...] + _dot(mix, wout_ref[...]), fnw_ref[...])


def _tail_call(xs, rest_s, so_s, pool_t, pscale, rnw, fnw, wpool_bf, wout_bf):
    n = xs.shape[0]
    return pl.pallas_call(
        _tail_kernel,
        out_shape=(jax.ShapeDtypeStruct((n, D_MODEL), F32),
                   jax.ShapeDtypeStruct((POOL_BUF, n, D_POOL), F32)),
        scratch_shapes=[pltpu.VMEM((n, D_RET), F32)],
        compiler_params=pltpu.CompilerParams(vmem_limit_bytes=VMEM_LIMIT_BYTES),
        name="tail",
    )(xs, rest_s, so_s, pool_t, pscale, rnw, fnw, wpool_bf, wout_bf)


def _rotary_tables(pos):
    half = HEAD_DIM // 2
    inv = ROPE_BASE ** (-np.arange(half, dtype=np.float64) / half)
    ang = np.asarray(pos, np.float64)[:, None] * inv[None, :]
    cos, sin = np.cos(ang), np.sin(ang)
    return (jnp.asarray(np.concatenate([cos, cos], axis=-1), F32),
            jnp.asarray(np.concatenate([-sin, sin], axis=-1), F32))


def kernel(x_prompt, x_sample, state_ret, state_pool, meta_tokens, norm_w, w_in, w_pool,
           pool_scale, ret_norm_w, w_out, final_norm_w):
    assert norm_w.shape[0] == 1, "single-layer stack"
    batch, seq, _ = x_prompt.shape
    n_sample = x_sample.shape[0]
    n_steps = (batch // PROMPT_BLOCK) * (seq // CHUNK)
    per = n_sample // n_steps
    assert per * n_steps == n_sample and per % 2 == 0 and per <= SUBLANES
    normw, pscale, rnw = norm_w, pool_scale, ret_norm_w
    fnw = final_norm_w[None, :]
    xs = x_sample[:, 0, :]

    cos_p, sin_p = _rotary_tables(np.arange(N_META + seq))
    cos_s, sin_s = _rotary_tables(PAST_LEN + np.arange(1))

    win_bf, wout_bf, wpool_bf, smeta, umeta, qkv_s, rest_s = _prep_call(
        meta_tokens.astype(x_prompt.dtype), xs, cos_p[:N_META], sin_p[:N_META], cos_s, sin_s,
        normw, w_in[0], w_out[0], w_pool[0], n_steps)
    y_p, s_p, buf_p, s_s, so_s = _main_call(
        x_prompt, cos_p[N_META:], sin_p[N_META:], smeta, umeta, normw, pscale, rnw, fnw,
        win_bf, wpool_bf, wout_bf, qkv_s, state_ret[0])
    y_s, buf_s = _tail_call(xs, rest_s, so_s, jnp.transpose(state_pool[0], (1, 0, 2)),
                            pscale, rnw, fnw, wpool_bf, wout_bf)
    return (y_p, y_s[:, None, :], s_p[None], s_s[None], buf_p[None],
            jnp.transpose(buf_s, (1, 0, 2))[None])
```

```python
import math

import jax
import jax.numpy as jnp
import numpy as np
from jax import lax
from jax.experimental import pallas as pl
from jax.experimental.pallas import tpu as pltpu

D_MODEL = 1024
D_POOL = 1024
D_RET = 1024
D_MIX = D_POOL + D_RET
POOL_WINDOWS = (2, 4, 8, 16)
POOL_GROUP = D_POOL // len(POOL_WINDOWS)
POOL_BUF = max(POOL_WINDOWS) - 1
N_HEADS = 8
HEAD_DIM = D_RET // N_HEADS
D_IN_PROJ = 2 * D_POOL + 4 * D_RET
N_META = 16
PAST_LEN = 16384
CHUNK = 128
ROPE_BASE = 10000.0
EPS = 1e-6
K_SCALE = HEAD_DIM ** -0.5

OFF_U, OFF_GP, OFF_Q, OFF_K, OFF_V, OFF_GR = (i * 1024 for i in range(6))

LOG_DECAY = tuple(math.log(1.0 - 2.0 ** (-5.0 - h)) for h in range(N_HEADS))

SUBLANES = 8
TILE_N = 256
PROMPT_BLOCK = 4
OUT_PIECES = 2
SAMPLE_HEADS_FIRST = 4
PREP_TILES = 8
HIST = 16

VMEM_LIMIT_BYTES = 56 * 1024 * 1024

F32 = jnp.float32
BF16 = jnp.bfloat16


def _rms(x, w):
    return x * lax.rsqrt(jnp.mean(x * x, axis=-1, keepdims=True) + EPS) * w


def _silu(x):
    return x * (1.0 / (1.0 + jnp.exp(-x)))


def _dot(a, b):
    return jnp.dot(a, b, preferred_element_type=F32)


def _dot_t(a, b):
    return lax.dot_general(a, b, (((0,), (0,)), ((), ())), preferred_element_type=F32)


def _rotary(x, cosf, sinf):
    return x * cosf + pltpu.roll(x, HEAD_DIM // 2, 1) * sinf


def _head(h):
    return slice(h * HEAD_DIM, (h + 1) * HEAD_DIM)


def _prep_kernel(meta_ref, xs_ref, cos_ref, sin_ref, coss_ref, sins_ref, normw_ref,
                 win_ref, wout_ref, wpool_ref,
                 winb_ref, woutb_ref, wpoolb_ref, s_ref, u_ref, qkv_ref, rest_ref,
                 hb_ref, hs_ref, proj_ref, projs_ref, kd_ref, v_ref, stage_ref):
    j = pl.program_id(0)
    k_rows = D_MODEL // PREP_TILES

    @pl.when(j == 0)
    def _():
        hb = _rms(meta_ref[...], normw_ref[...]).astype(BF16)
        hs = _rms(xs_ref[...], normw_ref[...]).astype(BF16)
        for t in range(PREP_TILES):
            hb_ref[t] = hb[:, t * k_rows:(t + 1) * k_rows]
            hs_ref[t] = hs[:, t * k_rows:(t + 1) * k_rows]
        wpoolb_ref[...] = wpool_ref[...].astype(BF16)
        proj_ref[...] = jnp.zeros_like(proj_ref)
        projs_ref[...] = jnp.zeros_like(projs_ref)

    tile = win_ref[...].astype(BF16)
    winb_ref[...] = tile
    woutb_ref[...] = wout_ref[...].astype(BF16)
    proj_ref[...] += _dot(hb_ref[j], tile)
    projs_ref[...] += _dot(hs_ref[j], tile)

    @pl.when(j == pl.num_programs(0) - 1)
    def _():
        cosf, sinf = cos_ref[...], sin_ref[...]
        row = lax.broadcasted_iota(jnp.int32, (N_META, HEAD_DIM), 0).astype(F32)
        kd_ref[...] = jnp.zeros_like(kd_ref)
        v_ref[...] = jnp.zeros_like(v_ref)
        u_ref[...] = proj_ref[:, OFF_U:OFF_U + D_POOL]
        for h in range(N_HEADS):
            kr = _rotary(proj_ref[:, OFF_K + h * HEAD_DIM:OFF_K + (h + 1) * HEAD_DIM], cosf, sinf)
            kd_ref[0:N_META, _head(h)] = kr * K_SCALE * jnp.exp(LOG_DECAY[h] * (N_META - 1.0 - row))
            v_ref[0:N_META, _head(h)] = proj_ref[:, OFF_V + h * HEAD_DIM:OFF_V + (h + 1) * HEAD_DIM]
        for h in range(N_HEADS):
            s_ref[h] = _dot_t(kd_ref[:, _head(h)].astype(BF16), v_ref[:, _head(h)].astype(BF16))

        coss, sins = coss_ref[...], sins_ref[...]
        rest_ref[:, 0:D_POOL] = projs_ref[:, OFF_U:OFF_U + D_POOL]
        rest_ref[:, D_POOL:2 * D_POOL] = projs_ref[:, OFF_GP:OFF_GP + D_POOL]
        rest_ref[:, 2 * D_POOL:2 * D_POOL + D_RET] = projs_ref[:, OFF_GR:OFF_GR + D_RET]
        for h in range(N_HEADS):
            qr = _rotary(projs_ref[:, OFF_Q + h * HEAD_DIM:OFF_Q + (h + 1) * HEAD_DIM], coss, sins)
            kr = _rotary(projs_ref[:, OFF_K + h * HEAD_DIM:OFF_K + (h + 1) * HEAD_DIM], coss, sins)
            kr = kr * K_SCALE
            v = projs_ref[:, OFF_V + h * HEAD_DIM:OFF_V + (h + 1) * HEAD_DIM]
            stage_ref[:, _head(h)] = qr * math.exp(LOG_DECAY[h])
            stage_ref[:, D_RET + h * HEAD_DIM:D_RET + (h + 1) * HEAD_DIM] = kr
            stage_ref[:, 2 * D_RET + h * HEAD_DIM:2 * D_RET + (h + 1) * HEAD_DIM] = v
            rest_ref[:, 2 * D_POOL + D_RET + h * HEAD_DIM:2 * D_POOL + D_RET + (h + 1) * HEAD_DIM] = (
                jnp.sum(qr * kr, axis=-1, keepdims=True) * v)
        n_steps, per = qkv_ref.shape[0], stage_ref.shape[0] // qkv_ref.shape[0]
        qkv_ref[...] = jnp.zeros_like(qkv_ref)
        for i in range(n_steps):
            qkv_ref[i, 0:per, :] = stage_ref[i * per:(i + 1) * per, :]


def _prep_call(meta, xs, cosf, sinf, coss, sins, normw, w_in, w_out, w_pool, n_steps):
    n = xs.shape[0]

    def const(shape):
        zeros = (0,) * len(shape)
        return pl.BlockSpec(shape, lambda j: zeros)

    row_tile = lambda j: (j, 0)
    return pl.pallas_call(
        _prep_kernel,
        grid=(PREP_TILES,),
        in_specs=[
            const((N_META, D_MODEL)),
            const((n, D_MODEL)),
            const((N_META, HEAD_DIM)),
            const((N_META, HEAD_DIM)),
            const((1, HEAD_DIM)),
            const((1, HEAD_DIM)),
            const((1, D_MODEL)),
            pl.BlockSpec((D_MODEL // PREP_TILES, D_IN_PROJ), row_tile),
            pl.BlockSpec((D_MIX // PREP_TILES, D_MODEL), row_tile),
            const((len(POOL_WINDOWS), POOL_GROUP, POOL_GROUP)),
        ],
        out_specs=[
            pl.BlockSpec((D_MODEL // PREP_TILES, D_IN_PROJ), row_tile),
            pl.BlockSpec((D_MIX // PREP_TILES, D_MODEL), row_tile),
            const((len(POOL_WINDOWS), POOL_GROUP, POOL_GROUP)),
            const((N_HEADS, HEAD_DIM, HEAD_DIM)),
            const((N_META, D_POOL)),
            const((n_steps, SUBLANES, 3 * D_RET)),
            const((n, 2 * D_POOL + 2 * D_RET)),
        ],
        out_shape=(jax.ShapeDtypeStruct((D_MODEL, D_IN_PROJ), BF16),
                   jax.ShapeDtypeStruct((D_MIX, D_MODEL), BF16),
                   jax.ShapeDtypeStruct((len(POOL_WINDOWS), POOL_GROUP, POOL_GROUP), BF16),
                   jax.ShapeDtypeStruct((N_HEADS, HEAD_DIM, HEAD_DIM), F32),
                   jax.ShapeDtypeStruct((N_META, D_POOL), F32),
                   jax.ShapeDtypeStruct((n_steps, SUBLANES, 3 * D_RET), F32),
                   jax.ShapeDtypeStruct((n, 2 * D_POOL + 2 * D_RET), F32)),
        scratch_shapes=[pltpu.VMEM((PREP_TILES, N_META, D_MODEL // PREP_TILES), BF16),
                        pltpu.VMEM((PREP_TILES, n, D_MODEL // PREP_TILES), BF16),
                        pltpu.VMEM((N_META, D_IN_PROJ), F32),
                        pltpu.VMEM((n, D_IN_PROJ), F32),
                        pltpu.VMEM((CHUNK, D_RET), F32),
                        pltpu.VMEM((CHUNK, D_RET), F32),
                        pltpu.VMEM((n, 3 * D_RET), F32)],
        compiler_params=pltpu.CompilerParams(
            dimension_semantics=("arbitrary",), vmem_limit_bytes=VMEM_LIMIT_BYTES),
        name="prep",
    )(meta, xs, cosf, sinf, coss, sins, normw, w_in, w_out, w_pool)


def _sample_state_update(qkv_ref, state_ref, sout_ref, o_ref, heads):
    per = state_ref.shape[0]
    seq_of_row = lax.broadcasted_iota(jnp.int32, (SUBLANES, per * HEAD_DIM), 0)
    seq_of_col = lax.broadcasted_iota(jnp.int32, (SUBLANES, per * HEAD_DIM), 1) // HEAD_DIM
    own_block = seq_of_row == seq_of_col
    row_id = lax.broadcasted_iota(jnp.int32, (SUBLANES, HEAD_DIM), 0)
    for h in heads:
        g1 = math.exp(LOG_DECAY[h])
        qd = qkv_ref[:, _head(h)]
        kb = qkv_ref[:, D_RET + h * HEAD_DIM:D_RET + (h + 1) * HEAD_DIM].astype(BF16)
        v8 = qkv_ref[:, 2 * D_RET + h * HEAD_DIM:2 * D_RET + (h + 1) * HEAD_DIM]
        state = state_ref[:, h]
        q_blocks = jnp.where(own_block, jnp.concatenate([qd] * per, axis=-1), 0.0)
        o_ref[:, _head(h)] = _dot(q_blocks.astype(BF16),
                                  state.reshape(per * HEAD_DIM, HEAD_DIM).astype(BF16))
        for j in range(0, per, 2):
            v_pair = jnp.concatenate([jnp.where(row_id == j, v8, 0.0),
                                      jnp.where(row_id == j + 1, v8, 0.0)], axis=-1).astype(BF16)
            upd = _dot_t(kb, v_pair)
            sout_ref[j, h] = g1 * state[j] + upd[:, 0:HEAD_DIM]
            sout_ref[j + 1, h] = g1 * state[j + 1] + upd[:, HEAD_DIM:2 * HEAD_DIM]


def _main_kernel(x_ref, cos_ref, sin_ref, smeta_ref, umeta_ref, normw_ref, pscale_ref, rnw_ref,
                 fnw_ref, win_ref, wpool_ref, wout_ref, qkv_ref, sstate_ref,
                 y_ref, s_ref, pbuf_ref, sout_ref, so_ref,
                 hist_ref):
    c = pl.program_id(1)
    nb = x_ref.shape[0]
    batches = range(nb)

    @pl.when(c == 0)
    def _():
        for b in batches:
            s_ref[b] = smeta_ref[...]
            hist_ref[b] = umeta_ref[...]

    x = x_ref[...].reshape(nb * CHUNK, D_MODEL)
    hb = _rms(x, normw_ref[...]).astype(BF16)

    def proj(off):
        return _dot(hb, win_ref[:, off:off + TILE_N])

    def rows(t, b):
        return t[b * CHUNK:(b + 1) * CHUNK]

    cosf, sinf = cos_ref[...], sin_ref[...]
    li = lax.broadcasted_iota(jnp.int32, (CHUNK, CHUNK), 0).astype(F32)
    mi = lax.broadcasted_iota(jnp.int32, (CHUNK, CHUNK), 1).astype(F32)
    diff = li - mi

    qkv, part, outs = {}, {}, {}

    def issue_qkv(pair):
        qkv[pair] = tuple(proj(off + pair * TILE_N) for off in (OFF_Q, OFF_K, OFF_V))

    def stage_a(pair):
        q2, k2, v2 = qkv.pop(pair)
        for i in range(2):
            h = 2 * pair + i
            lg = LOG_DECAY[h]
            sub = slice(i * HEAD_DIM, (i + 1) * HEAD_DIM)
            dmask = jnp.where(diff >= 0.0, jnp.exp(lg * jnp.maximum(diff, 0.0)), 0.0)
            q_decay = jnp.exp(lg * (li + 1.0))
            k_decay = jnp.exp(lg * (CHUNK - 1.0 - li))
            for b in batches:
                qr = _rotary(rows(q2, b)[:, sub], cosf, sinf)
                kr = _rotary(rows(k2, b)[:, sub], cosf, sinf) * K_SCALE
                vb = rows(v2, b)[:, sub].astype(BF16)
                state = s_ref[b, h]
                scores = lax.dot_general(qr.astype(BF16), kr.astype(BF16), (((1,), (1,)), ((), ())),
                                         preferred_element_type=F32)
                cross = _dot((qr * q_decay).astype(BF16), state.astype(BF16))
                s_ref[b, h] = math.exp(lg * CHUNK) * state + _dot_t((kr * k_decay).astype(BF16), vb)
                part[b, h] = ((scores * dmask).astype(BF16), vb, cross)

    def stage_b(pair):
        for i in range(2):
            h = 2 * pair + i
            for b in batches:
                p, vb, cross = part.pop((b, h))
                outs[b, h] = _dot(p, vb) + cross

    def pool_windows(u_tiles):
        pooled = []
        for g, w in enumerate(POOL_WINDOWS):
            cols = slice(g * POOL_GROUP, (g + 1) * POOL_GROUP)
            per_batch = []
            for b in batches:
                u = rows(u_tiles[g], b)
                ext = jnp.concatenate([hist_ref[b, :, cols], u], axis=0)
                win_sum = ext
                shift = 1
                while shift < w:
                    win_sum = win_sum + pltpu.roll(win_sum, shift, 0)
                    shift *= 2
                per_batch.append((win_sum[HIST:] / float(w) - u).astype(BF16))
                hist_ref[b, :, cols] = u[CHUNK - HIST:]
            pooled.append(jnp.concatenate(per_batch, axis=0))
        return pooled

    def pool_dots(pooled):
        return [_dot(pooled[g], wpool_ref[g]) * pscale_ref[:, g * POOL_GROUP:(g + 1) * POOL_GROUP]
                for g in range(len(POOL_WINDOWS))]

    _sample_state_update(qkv_ref, sstate_ref, sout_ref, so_ref, range(0, SAMPLE_HEADS_FIRST))
    issue_qkv(0)
    issue_qkv(1)
    stage_a(0)
    issue_qkv(2)
    stage_a(1)
    stage_b(0)
    issue_qkv(3)
    u_tiles = [proj(OFF_U + g * POOL_GROUP) for g in range(len(POOL_WINDOWS))]
    stage_a(2)
    stage_b(1)
    gp_tiles = [proj(OFF_GP + g * POOL_GROUP) for g in range(len(POOL_WINDOWS))]
    pooled = pool_windows(u_tiles)
    stage_a(3)
    stage_b(2)
    gr_tiles = [proj(OFF_GR + pair * TILE_N) for pair in range(N_HEADS // 2)]
    mixed = pool_dots(pooled)
    stage_b(3)
    pool_mix = jnp.concatenate(
        [(mixed[g] * _silu(gp_tiles[g])).astype(BF16) for g in range(len(POOL_WINDOWS))], axis=-1)
    acc = x + _dot(pool_mix, wout_ref[0:D_POOL, :])

    per_piece = max(nb // OUT_PIECES, 1)
    for first in range(0, nb, per_piece):
        piece = range(first, first + per_piece)
        ret_y = []
        for b in piece:
            heads = []
            for h in range(N_HEADS):
                sub = slice((h % 2) * HEAD_DIM, (h % 2 + 1) * HEAD_DIM)
                o = outs.pop((b, h))
                rn = o * lax.rsqrt(jnp.mean(o * o, axis=-1, keepdims=True) + EPS)
                gate = _silu(rows(gr_tiles[h // 2], b)[:, sub])
                heads.append((rn * rnw_ref[:, _head(h)] * gate).astype(BF16))
            ret_y.append(jnp.concatenate(heads, axis=-1))
        ret_mix = jnp.concatenate(ret_y, axis=0)
        acc_piece = acc[first * CHUNK:(first + per_piece) * CHUNK]
        acc_piece = acc_piece + _dot(ret_mix, wout_ref[D_POOL:D_MIX, :])
        y_ref[first:first + per_piece] = _rms(acc_piece, fnw_ref[...]).reshape(
            per_piece, CHUNK, D_MODEL)

    _sample_state_update(qkv_ref, sstate_ref, sout_ref, so_ref, range(SAMPLE_HEADS_FIRST, N_HEADS))

    @pl.when(c == pl.num_programs(1) - 1)
    def _():
        for b in batches:
            pbuf_ref[b] = hist_ref[b, HIST - POOL_BUF:HIST, :]


def _main_call(x, cosf, sinf, smeta, umeta, normw, pscale, rnw, fnw, win_bf, wpool_bf, wout_bf,
               qkv_s, state_s):
    batch, seq, _ = x.shape
    nb = PROMPT_BLOCK
    n_chunks = seq // CHUNK
    n_sample = state_s.shape[0]
    per = n_sample // ((batch // nb) * n_chunks)

    def const(shape):
        zeros = (0,) * len(shape)
        return pl.BlockSpec(shape, lambda b, c: zeros, pipeline_mode=pl.Buffered(1))

    step = lambda b, c: b * n_chunks + c
    return pl.pallas_call(
        _main_kernel,
        grid=(batch // nb, n_chunks),
        in_specs=[
            pl.BlockSpec((nb, CHUNK, D_MODEL), lambda b, c: (b, c, 0)),
            pl.BlockSpec((CHUNK, HEAD_DIM), lambda b, c: (c, 0)),
            pl.BlockSpec((CHUNK, HEAD_DIM), lambda b, c: (c, 0)),
            const((N_HEADS, HEAD_DIM, HEAD_DIM)),
            const((N_META, D_POOL)),
            const((1, D_MODEL)),
            const((1, D_POOL)),
            const((1, D_RET)),
            const((1, D_MODEL)),
            const((D_MODEL, D_IN_PROJ)),
            const((len(POOL_WINDOWS), POOL_GROUP, POOL_GROUP)),
            const((D_MIX, D_MODEL)),
            pl.BlockSpec((None, SUBLANES, 3 * D_RET), lambda b, c: (step(b, c), 0, 0)),
            pl.BlockSpec((per, N_HEADS, HEAD_DIM, HEAD_DIM), lambda b, c: (step(b, c), 0, 0, 0)),
        ],
        out_specs=[
            pl.BlockSpec((nb, CHUNK, D_MODEL), lambda b, c: (b, c, 0)),
            pl.BlockSpec((nb, N_HEADS, HEAD_DIM, HEAD_DIM), lambda b, c: (b, 0, 0, 0)),
            pl.BlockSpec((nb, POOL_BUF, D_POOL), lambda b, c: (b, 0, 0)),
            pl.BlockSpec((per, N_HEADS, HEAD_DIM, HEAD_DIM), lambda b, c: (step(b, c), 0, 0, 0)),
            pl.BlockSpec((None, SUBLANES, D_RET), lambda b, c: (step(b, c), 0, 0)),
        ],
        out_shape=(jax.ShapeDtypeStruct((batch, seq, D_MODEL), F32),
                   jax.ShapeDtypeStruct((batch, N_HEADS, HEAD_DIM, HEAD_DIM), F32),
                   jax.ShapeDtypeStruct((batch, POOL_BUF, D_POOL), F32),
                   jax.ShapeDtypeStruct((n_sample, N_HEADS, HEAD_DIM, HEAD_DIM), F32),
                   jax.ShapeDtypeStruct((qkv_s.shape[0], SUBLANES, D_RET), F32)),
        scratch_shapes=[pltpu.VMEM((nb, HIST, D_POOL), F32)],
        compiler_params=pltpu.CompilerParams(
            dimension_semantics=("arbitrary", "arbitrary"), vmem_limit_bytes=VMEM_LIMIT_BYTES),
        name="main",
    )(x, cosf, sinf, smeta, umeta, normw, pscale, rnw, fnw, win_bf, wpool_bf, wout_bf,
      qkv_s, state_s)


def _tail_kernel(xs_ref, rest_ref, so_ref, pin_ref, pscale_ref, rnw_ref, fnw_ref, wpool_ref,
                 wout_ref, y_ref, pout_ref, o_ref):
    n_steps = so_ref.shape[0]
    per = xs_ref.shape[0] // n_steps
    for i in range(n_steps):
        o_ref[i * per:(i + 1) * per, :] = so_ref[i, 0:per, :]

    u = rest_ref[:, 0:D_POOL]
    pout_ref[0:POOL_BUF - 1] = pin_ref[1:POOL_BUF]
    pout_ref[POOL_BUF - 1] = u

    parts = []
    for g, w in enumerate(POOL_WINDOWS):
        cols = slice(g * POOL_GROUP, (g + 1) * POOL_GROUP)
        win_sum = u[:, cols]
        for r in range(POOL_BUF - (w - 1), POOL_BUF):
            win_sum = win_sum + pin_ref[r, :, cols]
        pooled = win_sum / float(w) - u[:, cols]
        mixed = _dot(pooled.astype(BF16), wpool_ref[g]) * pscale_ref[:, cols]
        gate = _silu(rest_ref[:, D_POOL + g * POOL_GROUP:D_POOL + (g + 1) * POOL_GROUP])
        parts.append((mixed * gate).astype(BF16))
    for h in range(N_HEADS):
        o = o_ref[:, _head(h)] + rest_ref[:, 2 * D_POOL + D_RET + h * HEAD_DIM:
                                          2 * D_POOL + D_RET + (h + 1) * HEAD_DIM]
        rn = o * lax.rsqrt(jnp.mean(o * o, axis=-1, keepdims=True) + EPS)
        gr = rest_ref[:, 2 * D_POOL + h * HEAD_DIM:2 * D_POOL + (h + 1) * HEAD_DIM]
        parts.append((rn * rnw_ref[:, _head(h)] * _silu(gr)).astype(BF16))
    mix = jnp.concatenate(parts, axis=-1)
    y_ref[...] = _rms(xs_ref[...] + _dot(mix, wout_ref[...]), fnw_ref[...])


def _tail_call(xs, rest_s, so_s, pool_t, pscale, rnw, fnw, wpool_bf, wout_bf):
    n = xs.shape[0]
    return pl.pallas_call(
        _tail_kernel,
        out_shape=(jax.ShapeDtypeStruct((n, D_MODEL), F32),
                   jax.ShapeDtypeStruct((POOL_BUF, n, D_POOL), F32)),
        scratch_shapes=[pltpu.VMEM((n, D_RET), F32)],
        compiler_params=pltpu.CompilerParams(vmem_limit_bytes=VMEM_LIMIT_BYTES),
        name="tail",
    )(xs, rest_s, so_s, pool_t, pscale, rnw, fnw, wpool_bf, wout_bf)


def _rotary_tables(pos):
    half = HEAD_DIM // 2
    inv = ROPE_BASE ** (-np.arange(half, dtype=np.float64) / half)
    ang = np.asarray(pos, np.float64)[:, None] * inv[None, :]
    cos, sin = np.cos(ang), np.sin(ang)
    return (jnp.asarray(np.concatenate([cos, cos], axis=-1), F32),
            jnp.asarray(np.concatenate([-sin, sin], axis=-1), F32))


def kernel(x_prompt, x_sample, state_ret, state_pool, meta_tokens, norm_w, w_in, w_pool,
           pool_scale, ret_norm_w, w_out, final_norm_w):
    assert norm_w.shape[0] == 1, "single-layer stack"
    batch, seq, _ = x_prompt.shape
    n_sample = x_sample.shape[0]
    n_steps = (batch // PROMPT_BLOCK) * (seq // CHUNK)
    per = n_sample // n_steps
    assert per * n_steps == n_sample and per % 2 == 0 and per <= SUBLANES
    normw, pscale, rnw = norm_w, pool_scale, ret_norm_w
    fnw = final_norm_w[None, :]
    xs = x_sample[:, 0, :]

    cos_p, sin_p = _rotary_tables(np.arange(N_META + seq))
    cos_s, sin_s = _rotary_tables(PAST_LEN + np.arange(1))

    win_bf, wout_bf, wpool_bf, smeta, umeta, qkv_s, rest_s = _prep_call(
        meta_tokens.astype(x_prompt.dtype), xs, cos_p[:N_META], sin_p[:N_META], cos_s, sin_s,
        normw, w_in[0], w_out[0], w_pool[0], n_steps)
    y_p, s_p, buf_p, s_s, so_s = _main_call(
        x_prompt, cos_p[N_META:], sin_p[N_META:], smeta, umeta, normw, pscale, rnw, fnw,
        win_bf, wpool_bf, wout_bf, qkv_s, state_ret[0])
    y_s, buf_s = _tail_call(xs, rest_s, so_s, jnp.transpose(state_pool[0], (1, 0, 2)),
                            pscale, rnw, fnw, wpool_bf, wout_bf)
    return (y_p, y_s[:, None, :], s_p[None], s_s[None], buf_p[None],
            jnp.transpose(buf_s, (1, 0, 2))[None])
```

```python
import math

import jax
import jax.numpy as jnp
import numpy as np
from jax import lax
from jax.experimental import pallas as pl
from jax.experimental.pallas import tpu as pltpu

D_MODEL = 1024
D_POOL = 1024
D_RET = 1024
D_MIX = D_POOL + D_RET
POOL_WINDOWS = (2, 4, 8, 16)
POOL_GROUP = D_POOL // len(POOL_WINDOWS)
POOL_BUF = max(POOL_WINDOWS) - 1
N_HEADS = 8
HEAD_DIM = D_RET // N_HEADS
D_IN_PROJ = 2 * D_POOL + 4 * D_RET
N_META = 16
PAST_LEN = 16384
CHUNK = 128
ROPE_BASE = 10000.0
EPS = 1e-6
K_SCALE = HEAD_DIM ** -0.5

OFF_U, OFF_GP, OFF_Q, OFF_K, OFF_V, OFF_GR = (i * 1024 for i in range(6))

LOG_DECAY = tuple(math.log(1.0 - 2.0 ** (-5.0 - h)) for h in range(N_HEADS))

SUBLANES = 8
TILE_N = 256
PROMPT_BLOCK = 4
OUT_PIECES = 2
SAMPLE_HEADS_FIRST = 4
PREP_TILES = 8
HIST = 16

VMEM_LIMIT_BYTES = 56 * 1024 * 1024

F32 = jnp.float32
BF16 = jnp.bfloat16
PACKED = jnp.uint32


def _rms(x, w):
    return x * lax.rsqrt(jnp.mean(x * x, axis=-1, keepdims=True) + EPS) * w


def _silu(x):
    return x * (1.0 / (1.0 + jnp.exp(-x)))


def _dot(a, b):
    return jnp.dot(a, b, preferred_element_type=F32)


def _dot_t(a, b):
    return lax.dot_general(a, b, (((0,), (0,)), ((), ())), preferred_element_type=F32)


def _pack_rows(w):
    return pltpu.bitcast(w, PACKED)


def _unpack_rows(w):
    return pltpu.bitcast(w, BF16)


def _rotary(x, cosf, sinf):
    return x * cosf + pltpu.roll(x, HEAD_DIM // 2, 1) * sinf


def _head(h):
    return slice(h * HEAD_DIM, (h + 1) * HEAD_DIM)


def _prep_kernel(meta_ref, xs_ref, cos_ref, sin_ref, coss_ref, sins_ref, normw_ref,
                 win_ref, wout_ref, wpool_ref,
                 winb_ref, woutb_ref, wpoolb_ref, s_ref, u_ref, qkv_ref, rest_ref,
                 hb_ref, hs_ref, proj_ref, projs_ref, kd_ref, v_ref, stage_ref):
    j = pl.program_id(0)
    k_rows = D_MODEL // PREP_TILES

    @pl.when(j == 0)
    def _():
        hb = _rms(meta_ref[...], normw_ref[...]).astype(BF16)
        hs = _rms(xs_ref[...], normw_ref[...]).astype(BF16)
        for t in range(PREP_TILES):
            hb_ref[t] = hb[:, t * k_rows:(t + 1) * k_rows]
            hs_ref[t] = hs[:, t * k_rows:(t + 1) * k_rows]
        wpoolb_ref[...] = _pack_rows(wpool_ref[...].astype(BF16))
        proj_ref[...] = jnp.zeros_like(proj_ref)
        projs_ref[...] = jnp.zeros_like(projs_ref)

    tile = win_ref[...].astype(BF16)
    winb_ref[...] = _pack_rows(tile)
    woutb_ref[...] = _pack_rows(wout_ref[...].astype(BF16))
    proj_ref[...] += _dot(hb_ref[j], tile)
    projs_ref[...] += _dot(hs_ref[j], tile)

    @pl.when(j == pl.num_programs(0) - 1)
    def _():
        cosf, sinf = cos_ref[...], sin_ref[...]
        row = lax.broadcasted_iota(jnp.int32, (N_META, HEAD_DIM), 0).astype(F32)
        kd_ref[...] = jnp.zeros_like(kd_ref)
        v_ref[...] = jnp.zeros_like(v_ref)
        u_ref[...] = proj_ref[:, OFF_U:OFF_U + D_POOL]
        for h in range(N_HEADS):
            kr = _rotary(proj_ref[:, OFF_K + h * HEAD_DIM:OFF_K + (h + 1) * HEAD_DIM], cosf, sinf)
            kd_ref[0:N_META, _head(h)] = kr * K_SCALE * jnp.exp(LOG_DECAY[h] * (N_META - 1.0 - row))
            v_ref[0:N_META, _head(h)] = proj_ref[:, OFF_V + h * HEAD_DIM:OFF_V + (h + 1) * HEAD_DIM]
        for h in range(N_HEADS):
            s_ref[h] = _dot_t(kd_ref[:, _head(h)].astype(BF16), v_ref[:, _head(h)].astype(BF16))

        coss, sins = coss_ref[...], sins_ref[...]
        rest_ref[:, 0:D_POOL] = projs_ref[:, OFF_U:OFF_U + D_POOL]
        rest_ref[:, D_POOL:2 * D_POOL] = projs_ref[:, OFF_GP:OFF_GP + D_POOL]
        rest_ref[:, 2 * D_POOL:2 * D_POOL + D_RET] = projs_ref[:, OFF_GR:OFF_GR + D_RET]
        for h in range(N_HEADS):
            qr = _rotary(projs_ref[:, OFF_Q + h * HEAD_DIM:OFF_Q + (h + 1) * HEAD_DIM], coss, sins)
            kr = _rotary(projs_ref[:, OFF_K + h * HEAD_DIM:OFF_K + (h + 1) * HEAD_DIM], coss, sins)
            kr = kr * K_SCALE
            v = projs_ref[:, OFF_V + h * HEAD_DIM:OFF_V + (h + 1) * HEAD_DIM]
            stage_ref[:, _head(h)] = qr * math.exp(LOG_DECAY[h])
            stage_ref[:, D_RET + h * HEAD_DIM:D_RET + (h + 1) * HEAD_DIM] = kr
            stage_ref[:, 2 * D_RET + h * HEAD_DIM:2 * D_RET + (h + 1) * HEAD_DIM] = v
            rest_ref[:, 2 * D_POOL + D_RET + h * HEAD_DIM:2 * D_POOL + D_RET + (h + 1) * HEAD_DIM] = (
                jnp.sum(qr * kr, axis=-1, keepdims=True) * v)
        n_steps, per = qkv_ref.shape[0], stage_ref.shape[0] // qkv_ref.shape[0]
        qkv_ref[...] = jnp.zeros_like(qkv_ref)
        for i in range(n_steps):
            qkv_ref[i, 0:per, :] = stage_ref[i * per:(i + 1) * per, :]


def _prep_call(meta, xs, cosf, sinf, coss, sins, normw, w_in, w_out, w_pool, n_steps):
    n = xs.shape[0]

    def const(shape):
        zeros = (0,) * len(shape)
        return pl.BlockSpec(shape, lambda j: zeros)

    row_tile = lambda j: (j, 0)
    return pl.pallas_call(
        _prep_kernel,
        grid=(PREP_TILES,),
        in_specs=[
            const((N_META, D_MODEL)),
            const((n, D_MODEL)),
            const((N_META, HEAD_DIM)),
            const((N_META, HEAD_DIM)),
            const((1, HEAD_DIM)),
            const((1, HEAD_DIM)),
            const((1, D_MODEL)),
            pl.BlockSpec((D_MODEL // PREP_TILES, D_IN_PROJ), row_tile),
            pl.BlockSpec((D_MIX // PREP_TILES, D_MODEL), row_tile),
            const((len(POOL_WINDOWS), POOL_GROUP, POOL_GROUP)),
        ],
        out_specs=[
            pl.BlockSpec((D_MODEL // PREP_TILES // 2, D_IN_PROJ), row_tile),
            pl.BlockSpec((D_MIX // PREP_TILES // 2, D_MODEL), row_tile),
            const((len(POOL_WINDOWS), POOL_GROUP // 2, POOL_GROUP)),
            const((N_HEADS, HEAD_DIM, HEAD_DIM)),
            const((N_META, D_POOL)),
            const((n_steps, SUBLANES, 3 * D_RET)),
            const((n, 2 * D_POOL + 2 * D_RET)),
        ],
        out_shape=(jax.ShapeDtypeStruct((D_MODEL // 2, D_IN_PROJ), PACKED),
                   jax.ShapeDtypeStruct((D_MIX // 2, D_MODEL), PACKED),
                   jax.ShapeDtypeStruct((len(POOL_WINDOWS), POOL_GROUP // 2, POOL_GROUP), PACKED),
                   jax.ShapeDtypeStruct((N_HEADS, HEAD_DIM, HEAD_DIM), F32),
                   jax.ShapeDtypeStruct((N_META, D_POOL), F32),
                   jax.ShapeDtypeStruct((n_steps, SUBLANES, 3 * D_RET), F32),
                   jax.ShapeDtypeStruct((n, 2 * D_POOL + 2 * D_RET), F32)),
        scratch_shapes=[pltpu.VMEM((PREP_TILES, N_META, D_MODEL // PREP_TILES), BF16),
                        pltpu.VMEM((PREP_TILES, n, D_MODEL // PREP_TILES), BF16),
                        pltpu.VMEM((N_META, D_IN_PROJ), F32),
                        pltpu.VMEM((n, D_IN_PROJ), F32),
                        pltpu.VMEM((CHUNK, D_RET), F32),
                        pltpu.VMEM((CHUNK, D_RET), F32),
                        pltpu.VMEM((n, 3 * D_RET), F32)],
        compiler_params=pltpu.CompilerParams(
            dimension_semantics=("arbitrary",), vmem_limit_bytes=VMEM_LIMIT_BYTES),
        name="prep",
    )(meta, xs, cosf, sinf, coss, sins, normw, w_in, w_out, w_pool)


def _sample_state_update(qkv_ref, state_ref, sout_ref, o_ref, heads):
    per = state_ref.shape[0]
    seq_of_row = lax.broadcasted_iota(jnp.int32, (SUBLANES, per * HEAD_DIM), 0)
    seq_of_col = lax.broadcasted_iota(jnp.int32, (SUBLANES, per * HEAD_DIM), 1) // HEAD_DIM
    own_block = seq_of_row == seq_of_col
    row_id = lax.broadcasted_iota(jnp.int32, (SUBLANES, HEAD_DIM), 0)
    for h in heads:
        g1 = math.exp(LOG_DECAY[h])
        qd = qkv_ref[:, _head(h)]
        kb = qkv_ref[:, D_RET + h * HEAD_DIM:D_RET + (h + 1) * HEAD_DIM].astype(BF16)
        v8 = qkv_ref[:, 2 * D_RET + h * HEAD_DIM:2 * D_RET + (h + 1) * HEAD_DIM]
        state = state_ref[:, h]
        q_blocks = jnp.where(own_block, jnp.concatenate([qd] * per, axis=-1), 0.0)
        o_ref[:, _head(h)] = _dot(q_blocks.astype(BF16),
                                  state.reshape(per * HEAD_DIM, HEAD_DIM).astype(BF16))
        for j in range(0, per, 2):
            v_pair = jnp.concatenate([jnp.where(row_id == j, v8, 0.0),
                                      jnp.where(row_id == j + 1, v8, 0.0)], axis=-1).astype(BF16)
            upd = _dot_t(kb, v_pair)
            sout_ref[j, h] = g1 * state[j] + upd[:, 0:HEAD_DIM]
            sout_ref[j + 1, h] = g1 * state[j + 1] + upd[:, HEAD_DIM:2 * HEAD_DIM]


def _main_kernel(x_ref, cos_ref, sin_ref, smeta_ref, umeta_ref, normw_ref, pscale_ref, rnw_ref,
                 fnw_ref, win_ref, wpool_ref, wout_ref, qkv_ref, sstate_ref,
                 y_ref, s_ref, pbuf_ref, sout_ref, so_ref,
                 hist_ref):
    c = pl.program_id(1)
    nb = x_ref.shape[0]
    batches = range(nb)

    @pl.when(c == 0)
    def _():
        for b in batches:
            s_ref[b] = smeta_ref[...]
            hist_ref[b] = umeta_ref[...]

    x = x_ref[...].reshape(nb * CHUNK, D_MODEL)
    hb = _rms(x, normw_ref[...]).astype(BF16)

    def proj(off):
        return _dot(hb, _unpack_rows(win_ref[:, off:off + TILE_N]))

    def rows(t, b):
        return t[b * CHUNK:(b + 1) * CHUNK]

    cosf, sinf = cos_ref[...], sin_ref[...]
    li = lax.broadcasted_iota(jnp.int32, (CHUNK, CHUNK), 0).astype(F32)
    mi = lax.broadcasted_iota(jnp.int32, (CHUNK, CHUNK), 1).astype(F32)
    diff = li - mi

    qkv, part, outs = {}, {}, {}

    def issue_qkv(pair):
        qkv[pair] = tuple(proj(off + pair * TILE_N) for off in (OFF_Q, OFF_K, OFF_V))

    def stage_a(pair):
        q2, k2, v2 = qkv.pop(pair)
        for i in range(2):
            h = 2 * pair + i
            lg = LOG_DECAY[h]
            sub = slice(i * HEAD_DIM, (i + 1) * HEAD_DIM)
            dmask = jnp.where(diff >= 0.0, jnp.exp(lg * jnp.maximum(diff, 0.0)), 0.0)
            q_decay = jnp.exp(lg * (li + 1.0))
            k_decay = jnp.exp(lg * (CHUNK - 1.0 - li))
            for b in batches:
                qr = _rotary(rows(q2, b)[:, sub], cosf, sinf)
                kr = _rotary(rows(k2, b)[:, sub], cosf, sinf) * K_SCALE
                vb = rows(v2, b)[:, sub].astype(BF16)
                state = s_ref[b, h]
                scores = lax.dot_general(qr.astype(BF16), kr.astype(BF16), (((1,), (1,)), ((), ())),
                                         preferred_element_type=F32)
                cross = _dot((qr * q_decay).astype(BF16), state.astype(BF16))
                s_ref[b, h] = math.exp(lg * CHUNK) * state + _dot_t((kr * k_decay).astype(BF16), vb)
                part[b, h] = ((scores * dmask).astype(BF16), vb, cross)

    def stage_b(pair):
        for i in range(2):
            h = 2 * pair + i
            for b in batches:
                p, vb, cross = part.pop((b, h))
                outs[b, h] = _dot(p, vb) + cross

    def pool_windows(u_tiles):
        pooled = []
        for g, w in enumerate(POOL_WINDOWS):
            cols = slice(g * POOL_GROUP, (g + 1) * POOL_GROUP)
            per_batch = []
            for b in batches:
                u = rows(u_tiles[g], b)
                ext = jnp.concatenate([hist_ref[b, :, cols], u], axis=0)
                win_sum = ext
                shift = 1
                while shift < w:
                    win_sum = win_sum + pltpu.roll(win_sum, shift, 0)
                    shift *= 2
                per_batch.append((win_sum[HIST:] / float(w) - u).astype(BF16))
                hist_ref[b, :, cols] = u[CHUNK - HIST:]
            pooled.append(jnp.concatenate(per_batch, axis=0))
        return pooled

    def pool_dots(pooled):
        return [_dot(pooled[g], _unpack_rows(wpool_ref[g])) * pscale_ref[:, g * POOL_GROUP:(g + 1) * POOL_GROUP]
                for g in range(len(POOL_WINDOWS))]

    _sample_state_update(qkv_ref, sstate_ref, sout_ref, so_ref, range(0, SAMPLE_HEADS_FIRST))
    issue_qkv(0)
    issue_qkv(1)
    stage_a(0)
    issue_qkv(2)
    stage_a(1)
    stage_b(0)
    issue_qkv(3)
    u_tiles = [proj(OFF_U + g * POOL_GROUP) for g in range(len(POOL_WINDOWS))]
    stage_a(2)
    stage_b(1)
    gp_tiles = [proj(OFF_GP + g * POOL_GROUP) for g in range(len(POOL_WINDOWS))]
    pooled = pool_windows(u_tiles)
    stage_a(3)
    stage_b(2)
    gr_tiles = [proj(OFF_GR + pair * TILE_N) for pair in range(N_HEADS // 2)]
    mixed = pool_dots(pooled)
    stage_b(3)
    pool_mix = jnp.concatenate(
        [(mixed[g] * _silu(gp_tiles[g])).astype(BF16) for g in range(len(POOL_WINDOWS))], axis=-1)
    acc = x + _dot(pool_mix, _unpack_rows(wout_ref[0:D_POOL // 2, :]))

    per_piece = max(nb // OUT_PIECES, 1)
    for first in range(0, nb, per_piece):
        piece = range(first, first + per_piece)
        ret_y = []
        for b in piece:
            heads = []
            for h in range(N_HEADS):
                sub = slice((h % 2) * HEAD_DIM, (h % 2 + 1) * HEAD_DIM)
                o = outs.pop((b, h))
                rn = o * lax.rsqrt(jnp.mean(o * o, axis=-1, keepdims=True) + EPS)
                gate = _silu(rows(gr_tiles[h // 2], b)[:, sub])
                heads.append((rn * rnw_ref[:, _head(h)] * gate).astype(BF16))
            ret_y.append(jnp.concatenate(heads, axis=-1))
        ret_mix = jnp.concatenate(ret_y, axis=0)
        acc_piece = acc[first * CHUNK:(first + per_piece) * CHUNK]
        acc_piece = acc_piece + _dot(ret_mix, _unpack_rows(wout_ref[D_POOL // 2:D_MIX // 2, :]))
        y_ref[first:first + per_piece] = _rms(acc_piece, fnw_ref[...]).reshape(
            per_piece, CHUNK, D_MODEL)

    _sample_state_update(qkv_ref, sstate_ref, sout_ref, so_ref, range(SAMPLE_HEADS_FIRST, N_HEADS))

    @pl.when(c == pl.num_programs(1) - 1)
    def _():
        for b in batches:
            pbuf_ref[b] = hist_ref[b, HIST - POOL_BUF:HIST, :]


def _main_call(x, cosf, sinf, smeta, umeta, normw, pscale, rnw, fnw, win_bf, wpool_bf, wout_bf,
               qkv_s, state_s):
    batch, seq, _ = x.shape
    nb = PROMPT_BLOCK
    n_chunks = seq // CHUNK
    n_sample = state_s.shape[0]
    per = n_sample // ((batch // nb) * n_chunks)

    def const(shape):
        zeros = (0,) * len(shape)
        return pl.BlockSpec(shape, lambda b, c: zeros, pipeline_mode=pl.Buffered(1))

    step = lambda b, c: b * n_chunks + c
    return pl.pallas_call(
        _main_kernel,
        grid=(batch // nb, n_chunks),
        in_specs=[
            pl.BlockSpec((nb, CHUNK, D_MODEL), lambda b, c: (b, c, 0)),
            pl.BlockSpec((CHUNK, HEAD_DIM), lambda b, c: (c, 0)),
            pl.BlockSpec((CHUNK, HEAD_DIM), lambda b, c: (c, 0)),
            const((N_HEADS, HEAD_DIM, HEAD_DIM)),
            const((N_META, D_POOL)),
            const((1, D_MODEL)),
            const((1, D_POOL)),
            const((1, D_RET)),
            const((1, D_MODEL)),
            const((D_MODEL // 2, D_IN_PROJ)),
            const((len(POOL_WINDOWS), POOL_GROUP // 2, POOL_GROUP)),
            const((D_MIX // 2, D_MODEL)),
            pl.BlockSpec((None, SUBLANES, 3 * D_RET), lambda b, c: (step(b, c), 0, 0)),
            pl.BlockSpec((per, N_HEADS, HEAD_DIM, HEAD_DIM), lambda b, c: (step(b, c), 0, 0, 0)),
        ],
        out_specs=[
            pl.BlockSpec((nb, CHUNK, D_MODEL), lambda b, c: (b, c, 0)),
            pl.BlockSpec((nb, N_HEADS, HEAD_DIM, HEAD_DIM), lambda b, c: (b, 0, 0, 0)),
            pl.BlockSpec((nb, POOL_BUF, D_POOL), lambda b, c: (b, 0, 0)),
            pl.BlockSpec((per, N_HEADS, HEAD_DIM, HEAD_DIM), lambda b, c: (step(b, c), 0, 0, 0)),
            pl.BlockSpec((None, SUBLANES, D_RET), lambda b, c: (step(b, c), 0, 0)),
        ],
        out_shape=(jax.ShapeDtypeStruct((batch, seq, D_MODEL), F32),
                   jax.ShapeDtypeStruct((batch, N_HEADS, HEAD_DIM, HEAD_DIM), F32),
                   jax.ShapeDtypeStruct((batch, POOL_BUF, D_POOL), F32),
                   jax.ShapeDtypeStruct((n_sample, N_HEADS, HEAD_DIM, HEAD_DIM), F32),
                   jax.ShapeDtypeStruct((qkv_s.shape[0], SUBLANES, D_RET), F32)),
        scratch_shapes=[pltpu.VMEM((nb, HIST, D_POOL), F32)],
        compiler_params=pltpu.CompilerParams(
            dimension_semantics=("arbitrary", "arbitrary"), vmem_limit_bytes=VMEM_LIMIT_BYTES),
        name="main",
    )(x, cosf, sinf, smeta, umeta, normw, pscale, rnw, fnw, win_bf, wpool_bf, wout_bf,
      qkv_s, state_s)


def _tail_kernel(xs_ref, rest_ref, so_ref, pin_ref, pscale_ref, rnw_ref, fnw_ref, wpool_ref,
                 wout_ref, y_ref, pout_ref, o_ref):
    n_steps = so_ref.shape[0]
    per = xs_ref.shape[0] // n_steps
    for i in range(n_steps):
        o_ref[i * per:(i + 1) * per, :] = so_ref[i, 0:per, :]

    u = rest_ref[:, 0:D_POOL]
    pout_ref[0:POOL_BUF - 1] = pin_ref[1:POOL_BUF]
    pout_ref[POOL_BUF - 1] = u

    parts = []
    for g, w in enumerate(POOL_WINDOWS):
        cols = slice(g * POOL_GROUP, (g + 1) * POOL_GROUP)
        win_sum = u[:, cols]
        for r in range(POOL_BUF - (w - 1), POOL_BUF):
            win_sum = win_sum + pin_ref[r, :, cols]
        pooled = win_sum / float(w) - u[:, cols]
        mixed = _dot(pooled.astype(BF16), _unpack_rows(wpool_ref[g])) * pscale_ref[:, cols]
        gate = _silu(rest_ref[:, D_POOL + g * POOL_GROUP:D_POOL + (g + 1) * POOL_GROUP])
        parts.append((mixed * gate).astype(BF16))
    for h in range(N_HEADS):
        o = o_ref[:, _head(h)] + rest_ref[:, 2 * D_POOL + D_RET + h * HEAD_DIM:
                                          2 * D_POOL + D_RET + (h + 1) * HEAD_DIM]
        rn = o * lax.rsqrt(jnp.mean(o * o, axis=-1, keepdims=True) + EPS)
        gr = rest_ref[:, 2 * D_POOL + h * HEAD_DIM:2 * D_POOL + (h + 1) * HEAD_DIM]
        parts.append((rn * rnw_ref[:, _head(h)] * _silu(gr)).astype(BF16))
    mix = jnp.concatenate(parts, axis=-1)
    y_ref[...] = _rms(xs_ref[...] + _dot(mix, _unpack_rows(wout_ref[...])), fnw_ref[...])


def _tail_call(xs, rest_s, so_s, pool_t, pscale, rnw, fnw, wpool_bf, wout_bf):
    n = xs.shape[0]
    return pl.pallas_call(
        _tail_kernel,
        out_shape=(jax.ShapeDtypeStruct((n, D_MODEL), F32),
                   jax.ShapeDtypeStruct((POOL_BUF, n, D_POOL), F32)),
        scratch_shapes=[pltpu.VMEM((n, D_RET), F32)],
        compiler_params=pltpu.CompilerParams(vmem_limit_bytes=VMEM_LIMIT_BYTES),
        name="tail",
    )(xs, rest_s, so_s, pool_t, pscale, rnw, fnw, wpool_bf, wout_bf)


def _rotary_tables(pos):
    half = HEAD_DIM // 2
    inv = ROPE_BASE ** (-np.arange(half, dtype=np.float64) / half)
    ang = np.asarray(pos, np.float64)[:, None] * inv[None, :]
    cos, sin = np.cos(ang), np.sin(ang)
    return (jnp.asarray(np.concatenate([cos, cos], axis=-1), F32),
            jnp.asarray(np.concatenate([-sin, sin], axis=-1), F32))


def kernel(x_prompt, x_sample, state_ret, state_pool, meta_tokens, norm_w, w_in, w_pool,
           pool_scale, ret_norm_w, w_out, final_norm_w):
    assert norm_w.shape[0] == 1, "single-layer stack"
    batch, seq, _ = x_prompt.shape
    n_sample = x_sample.shape[0]
    n_steps = (batch // PROMPT_BLOCK) * (seq // CHUNK)
    per = n_sample // n_steps
    assert per * n_steps == n_sample and per % 2 == 0 and per <= SUBLANES
    normw, pscale, rnw = norm_w, pool_scale, ret_norm_w
    fnw = final_norm_w[None, :]
    xs = x_sample[:, 0, :]

    cos_p, sin_p = _rotary_tables(np.arange(N_META + seq))
    cos_s, sin_s = _rotary_tables(PAST_LEN + np.arange(1))

    win_bf, wout_bf, wpool_bf, smeta, umeta, qkv_s, rest_s = _prep_call(
        meta_tokens.astype(x_prompt.dtype), xs, cos_p[:N_META], sin_p[:N_META], cos_s, sin_s,
        normw, w_in[0], w_out[0], w_pool[0], n_steps)
    y_p, s_p, buf_p, s_s, so_s = _main_call(
        x_prompt, cos_p[N_META:], sin_p[N_META:], smeta, umeta, normw, pscale, rnw, fnw,
        win_bf, wpool_bf, wout_bf, qkv_s, state_ret[0])
    y_s, buf_s = _tail_call(xs, rest_s, so_s, jnp.transpose(state_pool[0], (1, 0, 2)),
                            pscale, rnw, fnw, wpool_bf, wout_bf)
    return (y_p, y_s[:, None, :], s_p[None], s_s[None], buf_p[None],
            jnp.transpose(buf_s, (1, 0, 2))[None])
```

```python
import math

import jax
import jax.numpy as jnp
import numpy as np
from jax import lax
from jax.experimental import pallas as pl
from jax.experimental.pallas import tpu as pltpu

D_MODEL = 1024
D_POOL = 1024
D_RET = 1024
D_MIX = D_POOL + D_RET
POOL_WINDOWS = (2, 4, 8, 16)
POOL_GROUP = D_POOL // len(POOL_WINDOWS)
POOL_BUF = max(POOL_WINDOWS) - 1
N_HEADS = 8
HEAD_DIM = D_RET // N_HEADS
D_IN_PROJ = 2 * D_POOL + 4 * D_RET
N_META = 16
PAST_LEN = 16384
CHUNK = 128
ROPE_BASE = 10000.0
EPS = 1e-6
K_SCALE = HEAD_DIM ** -0.5

OFF_U, OFF_GP, OFF_Q, OFF_K, OFF_V, OFF_GR = (i * 1024 for i in range(6))

LOG_DECAY = tuple(math.log(1.0 - 2.0 ** (-5.0 - h)) for h in range(N_HEADS))

SUBLANES = 8
TILE_N = 256
PROMPT_BLOCK = 4
OUT_PIECES = 2
SAMPLE_HEADS_FIRST = 4
PREP_TILES = 8
HIST = 16

VMEM_LIMIT_BYTES = 56 * 1024 * 1024

F32 = jnp.float32
BF16 = jnp.bfloat16
PACKED = jnp.uint32


def _rms(x, w):
    return x * lax.rsqrt(jnp.mean(x * x, axis=-1, keepdims=True) + EPS) * w


def _silu(x):
    return x * (1.0 / (1.0 + jnp.exp(-x)))


def _dot(a, b):
    return jnp.dot(a, b, preferred_element_type=F32)


def _dot_t(a, b):
    return lax.dot_general(a, b, (((0,), (0,)), ((), ())), preferred_element_type=F32)


def _pack_rows(w):
    return pltpu.bitcast(w, PACKED)


def _unpack_rows(w):
    return pltpu.bitcast(w, BF16)


def _rotary(x, cosf, sinf):
    return x * cosf + pltpu.roll(x, HEAD_DIM // 2, 1) * sinf


def _head(h):
    return slice(h * HEAD_DIM, (h + 1) * HEAD_DIM)


def _prep_kernel(meta_ref, xs_ref, cos_ref, sin_ref, coss_ref, sins_ref, normw_ref,
                 win_ref, wout_ref, wpool_ref,
                 winb_ref, woutb_ref, wpoolb_ref, s_ref, u_ref, qkv_ref, rest_ref,
                 hb_ref, hs_ref, proj_ref, projs_ref, kd_ref, v_ref, stage_ref):
    j = pl.program_id(0)
    k_rows = D_MODEL // PREP_TILES

    @pl.when(j == 0)
    def _():
        hb = _rms(meta_ref[...], normw_ref[...]).astype(BF16)
        hs = _rms(xs_ref[...], normw_ref[...]).astype(BF16)
        for t in range(PREP_TILES):
            hb_ref[t] = hb[:, t * k_rows:(t + 1) * k_rows]
            hs_ref[t] = hs[:, t * k_rows:(t + 1) * k_rows]
        wpoolb_ref[...] = _pack_rows(wpool_ref[...].astype(BF16))
        proj_ref[...] = jnp.zeros_like(proj_ref)
        projs_ref[...] = jnp.zeros_like(projs_ref)

    tile = win_ref[...].astype(BF16)
    winb_ref[...] = _pack_rows(tile)
    woutb_ref[...] = _pack_rows(wout_ref[...].astype(BF16))
    proj_ref[...] += _dot(hb_ref[j], tile)
    projs_ref[...] += _dot(hs_ref[j], tile)

    @pl.when(j == pl.num_programs(0) - 1)
    def _():
        cosf, sinf = cos_ref[...], sin_ref[...]
        row = lax.broadcasted_iota(jnp.int32, (N_META, HEAD_DIM), 0).astype(F32)
        kd_ref[...] = jnp.zeros_like(kd_ref)
        v_ref[...] = jnp.zeros_like(v_ref)
        u_ref[...] = proj_ref[:, OFF_U:OFF_U + D_POOL]
        for h in range(N_HEADS):
            kr = _rotary(proj_ref[:, OFF_K + h * HEAD_DIM:OFF_K + (h + 1) * HEAD_DIM], cosf, sinf)
            kd_ref[0:N_META, _head(h)] = kr * K_SCALE * jnp.exp(LOG_DECAY[h] * (N_META - 1.0 - row))
            v_ref[0:N_META, _head(h)] = proj_ref[:, OFF_V + h * HEAD_DIM:OFF_V + (h + 1) * HEAD_DIM]
        for h in range(N_HEADS):
            s_ref[h] = _dot_t(kd_ref[:, _head(h)].astype(BF16), v_ref[:, _head(h)].astype(BF16))

        coss, sins = coss_ref[...], sins_ref[...]
        rest_ref[:, 0:D_POOL] = projs_ref[:, OFF_U:OFF_U + D_POOL]
        rest_ref[:, D_POOL:2 * D_POOL] = projs_ref[:, OFF_GP:OFF_GP + D_POOL]
        rest_ref[:, 2 * D_POOL:2 * D_POOL + D_RET] = projs_ref[:, OFF_GR:OFF_GR + D_RET]
        for h in range(N_HEADS):
            qr = _rotary(projs_ref[:, OFF_Q + h * HEAD_DIM:OFF_Q + (h + 1) * HEAD_DIM], coss, sins)
            kr = _rotary(projs_ref[:, OFF_K + h * HEAD_DIM:OFF_K + (h + 1) * HEAD_DIM], coss, sins)
            kr = kr * K_SCALE
            v = projs_ref[:, OFF_V + h * HEAD_DIM:OFF_V + (h + 1) * HEAD_DIM]
            stage_ref[:, _head(h)] = qr * math.exp(LOG_DECAY[h])
            stage_ref[:, D_RET + h * HEAD_DIM:D_RET + (h + 1) * HEAD_DIM] = kr
            stage_ref[:, 2 * D_RET + h * HEAD_DIM:2 * D_RET + (h + 1) * HEAD_DIM] = v
            rest_ref[:, 2 * D_POOL + D_RET + h * HEAD_DIM:2 * D_POOL + D_RET + (h + 1) * HEAD_DIM] = (
                jnp.sum(qr * kr, axis=-1, keepdims=True) * v)
        n_steps, per = qkv_ref.shape[0], stage_ref.shape[0] // qkv_ref.shape[0]
        qkv_ref[...] = jnp.zeros_like(qkv_ref)
        for i in range(n_steps):
            qkv_ref[i, 0:per, :] = stage_ref[i * per:(i + 1) * per, :]


def _prep_call(meta, xs, cosf, sinf, coss, sins, normw, w_in, w_out, w_pool, n_steps):
    n = xs.shape[0]

    def const(shape):
        zeros = (0,) * len(shape)
        return pl.BlockSpec(shape, lambda j: zeros)

    row_tile = lambda j: (j, 0)
    return pl.pallas_call(
        _prep_kernel,
        grid=(PREP_TILES,),
        in_specs=[
            const((N_META, D_MODEL)),
            const((n, D_MODEL)),
            const((N_META, HEAD_DIM)),
            const((N_META, HEAD_DIM)),
            const((1, HEAD_DIM)),
            const((1, HEAD_DIM)),
            const((1, D_MODEL)),
            pl.BlockSpec((D_MODEL // PREP_TILES, D_IN_PROJ), row_tile),
            pl.BlockSpec((D_MIX // PREP_TILES, D_MODEL), row_tile),
            const((len(POOL_WINDOWS), POOL_GROUP, POOL_GROUP)),
        ],
        out_specs=[
            pl.BlockSpec((D_MODEL // PREP_TILES // 2, D_IN_PROJ), row_tile),
            pl.BlockSpec((D_MIX // PREP_TILES // 2, D_MODEL), row_tile),
            const((len(POOL_WINDOWS), POOL_GROUP // 2, POOL_GROUP)),
            const((N_HEADS, HEAD_DIM, HEAD_DIM)),
            const((N_META, D_POOL)),
            const((n_steps, SUBLANES, 3 * D_RET)),
            const((n, 2 * D_POOL + 2 * D_RET)),
        ],
        out_shape=(jax.ShapeDtypeStruct((D_MODEL // 2, D_IN_PROJ), PACKED),
                   jax.ShapeDtypeStruct((D_MIX // 2, D_MODEL), PACKED),
                   jax.ShapeDtypeStruct((len(POOL_WINDOWS), POOL_GROUP // 2, POOL_GROUP), PACKED),
                   jax.ShapeDtypeStruct((N_HEADS, HEAD_DIM, HEAD_DIM), F32),
                   jax.ShapeDtypeStruct((N_META, D_POOL), F32),
                   jax.ShapeDtypeStruct((n_steps, SUBLANES, 3 * D_RET), F32),
                   jax.ShapeDtypeStruct((n, 2 * D_POOL + 2 * D_RET), F32)),
        scratch_shapes=[pltpu.VMEM((PREP_TILES, N_META, D_MODEL // PREP_TILES), BF16),
                        pltpu.VMEM((PREP_TILES, n, D_MODEL // PREP_TILES), BF16),
                        pltpu.VMEM((N_META, D_IN_PROJ), F32),
                        pltpu.VMEM((n, D_IN_PROJ), F32),
                        pltpu.VMEM((CHUNK, D_RET), F32),
                        pltpu.VMEM((CHUNK, D_RET), F32),
                        pltpu.VMEM((n, 3 * D_RET), F32)],
        compiler_params=pltpu.CompilerParams(
            dimension_semantics=("arbitrary",), vmem_limit_bytes=VMEM_LIMIT_BYTES),
        name="prep",
    )(meta, xs, cosf, sinf, coss, sins, normw, w_in, w_out, w_pool)


def _sample_state_update(qkv_ref, state_ref, sout_ref, o_ref, heads):
    per = state_ref.shape[0]
    seq_of_row = lax.broadcasted_iota(jnp.int32, (SUBLANES, per * HEAD_DIM), 0)
    seq_of_col = lax.broadcasted_iota(jnp.int32, (SUBLANES, per * HEAD_DIM), 1) // HEAD_DIM
    own_block = seq_of_row == seq_of_col
    for h in heads:
        g1 = math.exp(LOG_DECAY[h])
        qd = qkv_ref[:, _head(h)]
        kt = qkv_ref[:, D_RET + h * HEAD_DIM:D_RET + (h + 1) * HEAD_DIM].T
        v8 = qkv_ref[:, 2 * D_RET + h * HEAD_DIM:2 * D_RET + (h + 1) * HEAD_DIM]
        state = state_ref[:, h]
        q_blocks = jnp.where(own_block, jnp.concatenate([qd] * per, axis=-1), 0.0)
        o_ref[:, _head(h)] = _dot(q_blocks.astype(BF16),
                                  state.reshape(per * HEAD_DIM, HEAD_DIM).astype(BF16))
        for j in range(per):
            sout_ref[j, h] = g1 * state[j] + kt[:, j:j + 1] * v8[j:j + 1, :]


def _main_kernel(x_ref, cos_ref, sin_ref, smeta_ref, umeta_ref, normw_ref, pscale_ref, rnw_ref,
                 fnw_ref, win_ref, wpool_ref, wout_ref, qkv_ref, sstate_ref,
                 y_ref, s_ref, pbuf_ref, sout_ref, so_ref,
                 hist_ref):
    c = pl.program_id(1)
    nb = x_ref.shape[0]
    batches = range(nb)

    @pl.when(c == 0)
    def _():
        for b in batches:
            s_ref[b] = smeta_ref[...]
            hist_ref[b] = umeta_ref[...]

    x = x_ref[...].reshape(nb * CHUNK, D_MODEL)
    hb = _rms(x, normw_ref[...]).astype(BF16)

    def proj(off):
        return _dot(hb, _unpack_rows(win_ref[:, off:off + TILE_N]))

    def rows(t, b):
        return t[b * CHUNK:(b + 1) * CHUNK]

    cosf, sinf = cos_ref[...], sin_ref[...]
    li = lax.broadcasted_iota(jnp.int32, (CHUNK, CHUNK), 0).astype(F32)
    mi = lax.broadcasted_iota(jnp.int32, (CHUNK, CHUNK), 1).astype(F32)
    diff = li - mi

    qkv, part, outs = {}, {}, {}

    def issue_qkv(pair):
        qkv[pair] = tuple(proj(off + pair * TILE_N) for off in (OFF_Q, OFF_K, OFF_V))

    def stage_a(pair):
        q2, k2, v2 = qkv.pop(pair)
        for i in range(2):
            h = 2 * pair + i
            lg = LOG_DECAY[h]
            sub = slice(i * HEAD_DIM, (i + 1) * HEAD_DIM)
            dmask = jnp.where(diff >= 0.0, jnp.exp(lg * jnp.maximum(diff, 0.0)), 0.0)
            q_decay = jnp.exp(lg * (li + 1.0))
            k_decay = jnp.exp(lg * (CHUNK - 1.0 - li))
            for b in batches:
                qr = _rotary(rows(q2, b)[:, sub], cosf, sinf)
                kr = _rotary(rows(k2, b)[:, sub], cosf, sinf) * K_SCALE
                vb = rows(v2, b)[:, sub].astype(BF16)
                state = s_ref[b, h]
                scores = lax.dot_general(qr.astype(BF16), kr.astype(BF16), (((1,), (1,)), ((), ())),
                                         preferred_element_type=F32)
                cross = _dot((qr * q_decay).astype(BF16), state.astype(BF16))
                s_ref[b, h] = math.exp(lg * CHUNK) * state + _dot_t((kr * k_decay).astype(BF16), vb)
                part[b, h] = ((scores * dmask).astype(BF16), vb, cross)

    def stage_b(pair):
        for i in range(2):
            h = 2 * pair + i
            for b in batches:
                p, vb, cross = part.pop((b, h))
                outs[b, h] = _dot(p, vb) + cross

    def pool_windows(u_tiles):
        pooled = []
        for g, w in enumerate(POOL_WINDOWS):
            cols = slice(g * POOL_GROUP, (g + 1) * POOL_GROUP)
            per_batch = []
            for b in batches:
                u = rows(u_tiles[g], b)
                ext = jnp.concatenate([hist_ref[b, :, cols], u], axis=0)
                win_sum = ext
                shift = 1
                while shift < w:
                    win_sum = win_sum + pltpu.roll(win_sum, shift, 0)
                    shift *= 2
                per_batch.append((win_sum[HIST:] / float(w) - u).astype(BF16))
                hist_ref[b, :, cols] = u[CHUNK - HIST:]
            pooled.append(jnp.concatenate(per_batch, axis=0))
        return pooled

    def pool_dots(pooled):
        return [_dot(pooled[g], _unpack_rows(wpool_ref[g])) * pscale_ref[:, g * POOL_GROUP:(g + 1) * POOL_GROUP]
                for g in range(len(POOL_WINDOWS))]

    _sample_state_update(qkv_ref, sstate_ref, sout_ref, so_ref, range(0, SAMPLE_HEADS_FIRST))
    issue_qkv(0)
    issue_qkv(1)
    stage_a(0)
    issue_qkv(2)
    stage_a(1)
    stage_b(0)
    issue_qkv(3)
    u_tiles = [proj(OFF_U + g * POOL_GROUP) for g in range(len(POOL_WINDOWS))]
    stage_a(2)
    stage_b(1)
    gp_tiles = [proj(OFF_GP + g * POOL_GROUP) for g in range(len(POOL_WINDOWS))]
    pooled = pool_windows(u_tiles)
    stage_a(3)
    stage_b(2)
    gr_tiles = [proj(OFF_GR + pair * TILE_N) for pair in range(N_HEADS // 2)]
    mixed = pool_dots(pooled)
    stage_b(3)
    pool_mix = jnp.concatenate(
        [(mixed[g] * _silu(gp_tiles[g])).astype(BF16) for g in range(len(POOL_WINDOWS))], axis=-1)
    acc = x + _dot(pool_mix, _unpack_rows(wout_ref[0:D_POOL // 2, :]))

    per_piece = max(nb // OUT_PIECES, 1)
    for first in range(0, nb, per_piece):
        piece = range(first, first + per_piece)
        ret_y = []
        for b in piece:
            heads = []
            for h in range(N_HEADS):
                sub = slice((h % 2) * HEAD_DIM, (h % 2 + 1) * HEAD_DIM)
                o = outs.pop((b, h))
                rn = o * lax.rsqrt(jnp.mean(o * o, axis=-1, keepdims=True) + EPS)
                gate = _silu(rows(gr_tiles[h // 2], b)[:, sub])
                heads.append((rn * rnw_ref[:, _head(h)] * gate).astype(BF16))
            ret_y.append(jnp.concatenate(heads, axis=-1))
        ret_mix = jnp.concatenate(ret_y, axis=0)
        acc_piece = acc[first * CHUNK:(first + per_piece) * CHUNK]
        acc_piece = acc_piece + _dot(ret_mix, _unpack_rows(wout_ref[D_POOL // 2:D_MIX // 2, :]))
        y_ref[first:first + per_piece] = _rms(acc_piece, fnw_ref[...]).reshape(
            per_piece, CHUNK, D_MODEL)

    _sample_state_update(qkv_ref, sstate_ref, sout_ref, so_ref, range(SAMPLE_HEADS_FIRST, N_HEADS))

    @pl.when(c == pl.num_programs(1) - 1)
    def _():
        for b in batches:
            pbuf_ref[b] = hist_ref[b, HIST - POOL_BUF:HIST, :]


def _main_call(x, cosf, sinf, smeta, umeta, normw, pscale, rnw, fnw, win_bf, wpool_bf, wout_bf,
               qkv_s, state_s):
    batch, seq, _ = x.shape
    nb = PROMPT_BLOCK
    n_chunks = seq // CHUNK
    n_sample = state_s.shape[0]
    per = n_sample // ((batch // nb) * n_chunks)

    def const(shape):
        zeros = (0,) * len(shape)
        return pl.BlockSpec(shape, lambda b, c: zeros, pipeline_mode=pl.Buffered(1))

    step = lambda b, c: b * n_chunks + c
    return pl.pallas_call(
        _main_kernel,
        grid=(batch // nb, n_chunks),
        in_specs=[
            pl.BlockSpec((nb, CHUNK, D_MODEL), lambda b, c: (b, c, 0)),
            pl.BlockSpec((CHUNK, HEAD_DIM), lambda b, c: (c, 0)),
            pl.BlockSpec((CHUNK, HEAD_DIM), lambda b, c: (c, 0)),
            const((N_HEADS, HEAD_DIM, HEAD_DIM)),
            const((N_META, D_POOL)),
            const((1, D_MODEL)),
            const((1, D_POOL)),
            const((1, D_RET)),
            const((1, D_MODEL)),
            const((D_MODEL // 2, D_IN_PROJ)),
            const((len(POOL_WINDOWS), POOL_GROUP // 2, POOL_GROUP)),
            const((D_MIX // 2, D_MODEL)),
            pl.BlockSpec((None, SUBLANES, 3 * D_RET), lambda b, c: (step(b, c), 0, 0)),
            pl.BlockSpec((per, N_HEADS, HEAD_DIM, HEAD_DIM), lambda b, c: (step(b, c), 0, 0, 0)),
        ],
        out_specs=[
            pl.BlockSpec((nb, CHUNK, D_MODEL), lambda b, c: (b, c, 0)),
            pl.BlockSpec((nb, N_HEADS, HEAD_DIM, HEAD_DIM), lambda b, c: (b, 0, 0, 0)),
            pl.BlockSpec((nb, POOL_BUF, D_POOL), lambda b, c: (b, 0, 0)),
            pl.BlockSpec((per, N_HEADS, HEAD_DIM, HEAD_DIM), lambda b, c: (step(b, c), 0, 0, 0)),
            pl.BlockSpec((None, SUBLANES, D_RET), lambda b, c: (step(b, c), 0, 0)),
        ],
        out_shape=(jax.ShapeDtypeStruct((batch, seq, D_MODEL), F32),
                   jax.ShapeDtypeStruct((batch, N_HEADS, HEAD_DIM, HEAD_DIM), F32),
                   jax.ShapeDtypeStruct((batch, POOL_BUF, D_POOL), F32),
                   jax.ShapeDtypeStruct((n_sample, N_HEADS, HEAD_DIM, HEAD_DIM), F32),
                   jax.ShapeDtypeStruct((qkv_s.shape[0], SUBLANES, D_RET), F32)),
        scratch_shapes=[pltpu.VMEM((nb, HIST, D_POOL), F32)],
        compiler_params=pltpu.CompilerParams(
            dimension_semantics=("arbitrary", "arbitrary"), vmem_limit_bytes=VMEM_LIMIT_BYTES),
        name="main",
    )(x, cosf, sinf, smeta, umeta, normw, pscale, rnw, fnw, win_bf, wpool_bf, wout_bf,
      qkv_s, state_s)


def _tail_kernel(xs_ref, rest_ref, so_ref, pin_ref, pscale_ref, rnw_ref, fnw_ref, wpool_ref,
                 wout_ref, y_ref, pout_ref, o_ref):
    n_steps = so_ref.shape[0]
    per = xs_ref.shape[0] // n_steps
    for i in range(n_steps):
        o_ref[i * per:(i + 1) * per, :] = so_ref[i, 0:per, :]

    u = rest_ref[:, 0:D_POOL]
    pout_ref[0:POOL_BUF - 1] = pin_ref[1:POOL_BUF]
    pout_ref[POOL_BUF - 1] = u

    parts = []
    for g, w in enumerate(POOL_WINDOWS):
        cols = slice(g * POOL_GROUP, (g + 1) * POOL_GROUP)
        win_sum = u[:, cols]
        for r in range(POOL_BUF - (w - 1), POOL_BUF):
            win_sum = win_sum + pin_ref[r, :, cols]
        pooled = win_sum / float(w) - u[:, cols]
        mixed = _dot(pooled.astype(BF16), _unpack_rows(wpool_ref[g])) * pscale_ref[:, cols]
        gate = _silu(rest_ref[:, D_POOL + g * POOL_GROUP:D_POOL + (g + 1) * POOL_GROUP])
        parts.append((mixed * gate).astype(BF16))
    for h in range(N_HEADS):
        o = o_ref[:, _head(h)] + rest_ref[:, 2 * D_POOL + D_RET + h * HEAD_DIM:
                                          2 * D_POOL + D_RET + (h + 1) * HEAD_DIM]
        rn = o * lax.rsqrt(jnp.mean(o * o, axis=-1, keepdims=True) + EPS)
        gr = rest_ref[:, 2 * D_POOL + h * HEAD_DIM:2 * D_POOL + (h + 1) * HEAD_DIM]
        parts.append((rn * rnw_ref[:, _head(h)] * _silu(gr)).astype(BF16))
    mix = jnp.concatenate(parts, axis=-1)
    y_ref[...] = _rms(xs_ref[...] + _dot(mix, _unpack_rows(wout_ref[...])), fnw_ref[...])


def _tail_call(xs, rest_s, so_s, pool_t, pscale, rnw, fnw, wpool_bf, wout_bf):
    n = xs.shape[0]
    return pl.pallas_call(
        _tail_kernel,
        out_shape=(jax.ShapeDtypeStruct((n, D_MODEL), F32),
                   jax.ShapeDtypeStruct((POOL_BUF, n, D_POOL), F32)),
        scratch_shapes=[pltpu.VMEM((n, D_RET), F32)],
        compiler_params=pltpu.CompilerParams(vmem_limit_bytes=VMEM_LIMIT_BYTES),
        name="tail",
    )(xs, rest_s, so_s, pool_t, pscale, rnw, fnw, wpool_bf, wout_bf)


def _rotary_tables(pos):
    half = HEAD_DIM // 2
    inv = ROPE_BASE ** (-np.arange(half, dtype=np.float64) / half)
    ang = np.asarray(pos, np.float64)[:, None] * inv[None, :]
    cos, sin = np.cos(ang), np.sin(ang)
    return (jnp.asarray(np.concatenate([cos, cos], axis=-1), F32),
            jnp.asarray(np.concatenate([-sin, sin], axis=-1), F32))


def kernel(x_prompt, x_sample, state_ret, state_pool, meta_tokens, norm_w, w_in, w_pool,
           pool_scale, ret_norm_w, w_out, final_norm_w):
    assert norm_w.shape[0] == 1, "single-layer stack"
    batch, seq, _ = x_prompt.shape
    n_sample = x_sample.shape[0]
    n_steps = (batch // PROMPT_BLOCK) * (seq // CHUNK)
    per = n_sample // n_steps
    assert per * n_steps == n_sample and per % 2 == 0 and per <= SUBLANES
    normw, pscale, rnw = norm_w, pool_scale, ret_norm_w
    fnw = final_norm_w[None, :]
    xs = x_sample[:, 0, :]

    cos_p, sin_p = _rotary_tables(np.arange(N_META + seq))
    cos_s, sin_s = _rotary_tables(PAST_LEN + np.arange(1))

    win_bf, wout_bf, wpool_bf, smeta, umeta, qkv_s, rest_s = _prep_call(
        meta_tokens.astype(x_prompt.dtype), xs, cos_p[:N_META], sin_p[:N_META], cos_s, sin_s,
        normw, w_in[0], w_out[0], w_pool[0], n_steps)
    y_p, s_p, buf_p, s_s, so_s = _main_call(
        x_prompt, cos_p[N_META:], sin_p[N_META:], smeta, umeta, normw, pscale, rnw, fnw,
        win_bf, wpool_bf, wout_bf, qkv_s, state_ret[0])
    y_s, buf_s = _tail_call(xs, rest_s, so_s, jnp.transpose(state_pool[0], (1, 0, 2)),
                            pscale, rnw, fnw, wpool_bf, wout_bf)
    return (y_p, y_s[:, None, :], s_p[None], s_s[None], buf_p[None],
            jnp.transpose(buf_s, (1, 0, 2))[None])
```

```python
import math

import jax
import jax.numpy as jnp
import numpy as np
from jax import lax
from jax.experimental import pallas as pl
from jax.experimental.pallas import tpu as pltpu

D_MODEL = 1024
D_POOL = 1024
D_RET = 1024
D_MIX = D_POOL + D_RET
POOL_WINDOWS = (2, 4, 8, 16)
POOL_GROUP = D_POOL // len(POOL_WINDOWS)
POOL_BUF = max(POOL_WINDOWS) - 1
N_HEADS = 8
HEAD_DIM = D_RET // N_HEADS
D_IN_PROJ = 2 * D_POOL + 4 * D_RET
N_META = 16
PAST_LEN = 16384
CHUNK = 128
ROPE_BASE = 10000.0
EPS = 1e-6
K_SCALE = HEAD_DIM ** -0.5

OFF_U, OFF_GP, OFF_Q, OFF_K, OFF_V, OFF_GR = (i * 1024 for i in range(6))

LOG_DECAY = tuple(math.log(1.0 - 2.0 ** (-5.0 - h)) for h in range(N_HEADS))

SUBLANES = 8
TILE_N = 256
PROMPT_BLOCK = 4
OUT_PIECES = 2
SAMPLE_HEADS_FIRST = 4
PREP_TILES = 8
HIST = 16

VMEM_LIMIT_BYTES = 56 * 1024 * 1024

F32 = jnp.float32
BF16 = jnp.bfloat16
PACKED = jnp.uint32


def _rms(x, w):
    return x * lax.rsqrt(jnp.mean(x * x, axis=-1, keepdims=True) + EPS) * w


def _silu(x):
    return x * (1.0 / (1.0 + jnp.exp(-x)))


def _dot(a, b):
    return jnp.dot(a, b, preferred_element_type=F32)


def _dot_t(a, b):
    return lax.dot_general(a, b, (((0,), (0,)), ((), ())), preferred_element_type=F32)


def _pack_rows(w):
    return pltpu.bitcast(w, PACKED)


def _unpack_rows(w):
    return pltpu.bitcast(w, BF16)


def _rotary(x, cosf, sinf):
    return x * cosf + pltpu.roll(x, HEAD_DIM // 2, 1) * sinf


def _head(h):
    return slice(h * HEAD_DIM, (h + 1) * HEAD_DIM)


def _prep_kernel(meta_ref, xs_ref, cos_ref, sin_ref, coss_ref, sins_ref, normw_ref,
                 win_ref, wout_ref, wpool_ref,
                 winb_ref, wub_ref, woutb_ref, wpoolb_ref, s_ref, u_ref, qkv_ref, rest_ref,
                 hb_ref, hs_ref, proj_ref, projs_ref, kd_ref, v_ref, stage_ref, wphi_ref, wplo_ref):
    j = pl.program_id(0)
    k_rows = D_MODEL // PREP_TILES

    @pl.when(j == 0)
    def _():
        hb = _rms(meta_ref[...], normw_ref[...]).astype(BF16)
        hs = _rms(xs_ref[...], normw_ref[...]).astype(BF16)
        for t in range(PREP_TILES):
            hb_ref[t] = hb[:, t * k_rows:(t + 1) * k_rows]
            hs_ref[t] = hs[:, t * k_rows:(t + 1) * k_rows]
        wp = wpool_ref[...]
        wp_hi = wp.astype(BF16)
        wpoolb_ref[...] = _pack_rows(wp_hi)
        wphi_ref[...] = wp_hi
        wplo_ref[...] = (wp - wp_hi.astype(F32)).astype(BF16)
        proj_ref[...] = jnp.zeros_like(proj_ref)
        projs_ref[...] = jnp.zeros_like(projs_ref)

    tile = win_ref[...].astype(BF16)
    folded = []
    for g in range(len(POOL_WINDOWS)):
        cols = slice(OFF_U + g * POOL_GROUP, OFF_U + (g + 1) * POOL_GROUP)
        a_hi = tile[:, cols]
        a_lo = (win_ref[:, cols] - a_hi.astype(F32)).astype(BF16)
        folded.append((_dot(a_hi, wphi_ref[g]) + _dot(a_hi, wplo_ref[g])
                       + _dot(a_lo, wphi_ref[g])).astype(BF16))
    tile_main = jnp.concatenate(folded + [tile[:, OFF_U + D_POOL:]], axis=-1)
    winb_ref[...] = _pack_rows(tile_main)
    wub_ref[...] = _pack_rows(tile[:, OFF_U:OFF_U + D_POOL])
    woutb_ref[...] = _pack_rows(wout_ref[...].astype(BF16))
    proj_ref[...] += _dot(hb_ref[j], tile_main)
    projs_ref[...] += _dot(hs_ref[j], tile)

    @pl.when(j == pl.num_programs(0) - 1)
    def _():
        cosf, sinf = cos_ref[...], sin_ref[...]
        row = lax.broadcasted_iota(jnp.int32, (N_META, HEAD_DIM), 0).astype(F32)
        kd_ref[...] = jnp.zeros_like(kd_ref)
        v_ref[...] = jnp.zeros_like(v_ref)
        u_ref[...] = proj_ref[:, OFF_U:OFF_U + D_POOL]
        for h in range(N_HEADS):
            kr = _rotary(proj_ref[:, OFF_K + h * HEAD_DIM:OFF_K + (h + 1) * HEAD_DIM], cosf, sinf)
            kd_ref[0:N_META, _head(h)] = kr * K_SCALE * jnp.exp(LOG_DECAY[h] * (N_META - 1.0 - row))
            v_ref[0:N_META, _head(h)] = proj_ref[:, OFF_V + h * HEAD_DIM:OFF_V + (h + 1) * HEAD_DIM]
        for h in range(N_HEADS):
            s_ref[h] = _dot_t(kd_ref[:, _head(h)].astype(BF16), v_ref[:, _head(h)].astype(BF16))

        coss, sins = coss_ref[...], sins_ref[...]
        rest_ref[:, 0:D_POOL] = projs_ref[:, OFF_U:OFF_U + D_POOL]
        rest_ref[:, D_POOL:2 * D_POOL] = projs_ref[:, OFF_GP:OFF_GP + D_POOL]
        rest_ref[:, 2 * D_POOL:2 * D_POOL + D_RET] = projs_ref[:, OFF_GR:OFF_GR + D_RET]
        for h in range(N_HEADS):
            qr = _rotary(projs_ref[:, OFF_Q + h * HEAD_DIM:OFF_Q + (h + 1) * HEAD_DIM], coss, sins)
            kr = _rotary(projs_ref[:, OFF_K + h * HEAD_DIM:OFF_K + (h + 1) * HEAD_DIM], coss, sins)
            kr = kr * K_SCALE
            v = projs_ref[:, OFF_V + h * HEAD_DIM:OFF_V + (h + 1) * HEAD_DIM]
            stage_ref[:, _head(h)] = qr * math.exp(LOG_DECAY[h])
            stage_ref[:, D_RET + h * HEAD_DIM:D_RET + (h + 1) * HEAD_DIM] = kr
            stage_ref[:, 2 * D_RET + h * HEAD_DIM:2 * D_RET + (h + 1) * HEAD_DIM] = v
            rest_ref[:, 2 * D_POOL + D_RET + h * HEAD_DIM:2 * D_POOL + D_RET + (h + 1) * HEAD_DIM] = (
                jnp.sum(qr * kr, axis=-1, keepdims=True) * v)
        n_steps, per = qkv_ref.shape[0], stage_ref.shape[0] // qkv_ref.shape[0]
        qkv_ref[...] = jnp.zeros_like(qkv_ref)
        for i in range(n_steps):
            qkv_ref[i, 0:per, :] = stage_ref[i * per:(i + 1) * per, :]


def _prep_call(meta, xs, cosf, sinf, coss, sins, normw, w_in, w_out, w_pool, n_steps):
    n = xs.shape[0]

    def const(shape):
        zeros = (0,) * len(shape)
        return pl.BlockSpec(shape, lambda j: zeros)

    row_tile = lambda j: (j, 0)
    return pl.pallas_call(
        _prep_kernel,
        grid=(PREP_TILES,),
        in_specs=[
            const((N_META, D_MODEL)),
            const((n, D_MODEL)),
            const((N_META, HEAD_DIM)),
            const((N_META, HEAD_DIM)),
            const((1, HEAD_DIM)),
            const((1, HEAD_DIM)),
            const((1, D_MODEL)),
            pl.BlockSpec((D_MODEL // PREP_TILES, D_IN_PROJ), row_tile),
            pl.BlockSpec((D_MIX // PREP_TILES, D_MODEL), row_tile),
            const((len(POOL_WINDOWS), POOL_GROUP, POOL_GROUP)),
        ],
        out_specs=[
            pl.BlockSpec((D_MODEL // PREP_TILES // 2, D_IN_PROJ), row_tile),
            pl.BlockSpec((D_MODEL // PREP_TILES // 2, D_POOL), row_tile),
            pl.BlockSpec((D_MIX // PREP_TILES // 2, D_MODEL), row_tile),
            const((len(POOL_WINDOWS), POOL_GROUP // 2, POOL_GROUP)),
            const((N_HEADS, HEAD_DIM, HEAD_DIM)),
            const((N_META, D_POOL)),
            const((n_steps, SUBLANES, 3 * D_RET)),
            const((n, 2 * D_POOL + 2 * D_RET)),
        ],
        out_shape=(jax.ShapeDtypeStruct((D_MODEL // 2, D_IN_PROJ), PACKED),
                   jax.ShapeDtypeStruct((D_MODEL // 2, D_POOL), PACKED),
                   jax.ShapeDtypeStruct((D_MIX // 2, D_MODEL), PACKED),
                   jax.ShapeDtypeStruct((len(POOL_WINDOWS), POOL_GROUP // 2, POOL_GROUP), PACKED),
                   jax.ShapeDtypeStruct((N_HEADS, HEAD_DIM, HEAD_DIM), F32),
                   jax.ShapeDtypeStruct((N_META, D_POOL), F32),
                   jax.ShapeDtypeStruct((n_steps, SUBLANES, 3 * D_RET), F32),
                   jax.ShapeDtypeStruct((n, 2 * D_POOL + 2 * D_RET), F32)),
        scratch_shapes=[pltpu.VMEM((PREP_TILES, N_META, D_MODEL // PREP_TILES), BF16),
                        pltpu.VMEM((PREP_TILES, n, D_MODEL // PREP_TILES), BF16),
                        pltpu.VMEM((N_META, D_IN_PROJ), F32),
                        pltpu.VMEM((n, D_IN_PROJ), F32),
                        pltpu.VMEM((CHUNK, D_RET), F32),
                        pltpu.VMEM((CHUNK, D_RET), F32),
                        pltpu.VMEM((n, 3 * D_RET), F32),
                        pltpu.VMEM((len(POOL_WINDOWS), POOL_GROUP, POOL_GROUP), BF16),
                        pltpu.VMEM((len(POOL_WINDOWS), POOL_GROUP, POOL_GROUP), BF16)],
        compiler_params=pltpu.CompilerParams(
            dimension_semantics=("arbitrary",), vmem_limit_bytes=VMEM_LIMIT_BYTES),
        name="prep",
    )(meta, xs, cosf, sinf, coss, sins, normw, w_in, w_out, w_pool)


def _sample_state_update(qkv_ref, state_ref, sout_ref, o_ref, heads):
    per = state_ref.shape[0]
    seq_of_row = lax.broadcasted_iota(jnp.int32, (SUBLANES, per * HEAD_DIM), 0)
    seq_of_col = lax.broadcasted_iota(jnp.int32, (SUBLANES, per * HEAD_DIM), 1) // HEAD_DIM
    own_block = seq_of_row == seq_of_col
    for h in heads:
        g1 = math.exp(LOG_DECAY[h])
        qd = qkv_ref[:, _head(h)]
        kt = qkv_ref[:, D_RET + h * HEAD_DIM:D_RET + (h + 1) * HEAD_DIM].T
        v8 = qkv_ref[:, 2 * D_RET + h * HEAD_DIM:2 * D_RET + (h + 1) * HEAD_DIM]
        state = state_ref[:, h]
        q_blocks = jnp.where(own_block, jnp.concatenate([qd] * per, axis=-1), 0.0)
        o_ref[:, _head(h)] = _dot(q_blocks.astype(BF16),
                                  state.reshape(per * HEAD_DIM, HEAD_DIM).astype(BF16))
        for j in range(per):
            sout_ref[j, h] = g1 * state[j] + kt[:, j:j + 1] * v8[j:j + 1, :]


def _sample_pool_update(us_ref, pin_ref, pout_ref, pooled_ref):
    u = us_ref[...]
    pout_ref[0:POOL_BUF - 1] = pin_ref[1:POOL_BUF]
    pout_ref[POOL_BUF - 1] = u
    for g, w in enumerate(POOL_WINDOWS):
        cols = slice(g * POOL_GROUP, (g + 1) * POOL_GROUP)
        win_sum = u[:, cols]
        for r in range(POOL_BUF - (w - 1), POOL_BUF):
            win_sum = win_sum + pin_ref[r, :, cols]
        pooled_ref[:, cols] = win_sum / float(w) - u[:, cols]


def _main_kernel(x_ref, cos_ref, sin_ref, smeta_ref, umeta_ref, normw_ref, pscale_ref, rnw_ref,
                 fnw_ref, win_ref, wu_ref, wout_ref, qkv_ref, sstate_ref, us_ref, pin_ref,
                 y_ref, s_ref, pbuf_ref, sout_ref, so_ref, pout_ref, pooled_ref,
                 hist_ref):
    c = pl.program_id(1)
    nb = x_ref.shape[0]
    batches = range(nb)

    @pl.when(c == 0)
    def _():
        for b in batches:
            s_ref[b] = smeta_ref[...]
            hist_ref[b] = umeta_ref[...]

    x = x_ref[...].reshape(nb * CHUNK, D_MODEL)
    hb = _rms(x, normw_ref[...]).astype(BF16)

    def proj(off):
        return _dot(hb, _unpack_rows(win_ref[:, off:off + TILE_N]))

    def rows(t, b):
        return t[b * CHUNK:(b + 1) * CHUNK]

    chunk_rows = pl.ds(pl.multiple_of(c * CHUNK, CHUNK), CHUNK)
    cosf, sinf = cos_ref[chunk_rows, :], sin_ref[chunk_rows, :]
    li = lax.broadcasted_iota(jnp.int32, (CHUNK, CHUNK), 0).astype(F32)
    mi = lax.broadcasted_iota(jnp.int32, (CHUNK, CHUNK), 1).astype(F32)
    diff = li - mi

    qkv, part, outs = {}, {}, {}

    def issue_qkv(pair):
        qkv[pair] = tuple(proj(off + pair * TILE_N) for off in (OFF_Q, OFF_K, OFF_V))

    def stage_a(pair):
        q2, k2, v2 = qkv.pop(pair)
        for i in range(2):
            h = 2 * pair + i
            lg = LOG_DECAY[h]
            sub = slice(i * HEAD_DIM, (i + 1) * HEAD_DIM)
            dmask = jnp.where(diff >= 0.0, jnp.exp(lg * jnp.maximum(diff, 0.0)), 0.0)
            q_decay = jnp.exp(lg * (li + 1.0))
            k_decay = jnp.exp(lg * (CHUNK - 1.0 - li))
            for b in batches:
                qr = _rotary(rows(q2, b)[:, sub], cosf, sinf)
                kr = _rotary(rows(k2, b)[:, sub], cosf, sinf) * K_SCALE
                vb = rows(v2, b)[:, sub].astype(BF16)
                state = s_ref[b, h]
                scores = lax.dot_general(qr.astype(BF16), kr.astype(BF16), (((1,), (1,)), ((), ())),
                                         preferred_element_type=F32)
                cross = _dot((qr * q_decay).astype(BF16), state.astype(BF16))
                s_ref[b, h] = math.exp(lg * CHUNK) * state + _dot_t((kr * k_decay).astype(BF16), vb)
                part[b, h] = ((scores * dmask).astype(BF16), vb, cross)

    def stage_b(pair):
        for i in range(2):
            h = 2 * pair + i
            for b in batches:
                p, vb, cross = part.pop((b, h))
                outs[b, h] = _dot(p, vb) + cross

    def pool_windows(u_tiles):
        pooled = []
        for g, w in enumerate(POOL_WINDOWS):
            cols = slice(g * POOL_GROUP, (g + 1) * POOL_GROUP)
            per_batch = []
            for b in batches:
                u = rows(u_tiles[g], b)
                ext = jnp.concatenate([hist_ref[b, :, cols], u], axis=0)
                win_sum = ext
                shift = 1
                while shift < w:
                    win_sum = win_sum + pltpu.roll(win_sum, shift, 0)
                    shift *= 2
                per_batch.append(win_sum[HIST:] / float(w) - u)
                hist_ref[b, :, cols] = u[CHUNK - HIST:]
            pooled.append(jnp.concatenate(per_batch, axis=0))
        return pooled

    so_step = so_ref.at[pl.program_id(0) * pl.num_programs(1) + c]
    _sample_state_update(qkv_ref, sstate_ref, sout_ref, so_step, range(0, SAMPLE_HEADS_FIRST))
    _sample_pool_update(us_ref, pin_ref, pout_ref, pooled_ref)
    issue_qkv(0)
    issue_qkv(1)
    stage_a(0)
    issue_qkv(2)
    stage_a(1)
    stage_b(0)
    issue_qkv(3)
    u_tiles = [proj(OFF_U + g * POOL_GROUP) for g in range(len(POOL_WINDOWS))]
    stage_a(2)
    stage_b(1)
    gp_tiles = [proj(OFF_GP + g * POOL_GROUP) for g in range(len(POOL_WINDOWS))]
    mixed = pool_windows(u_tiles)
    stage_a(3)
    stage_b(2)
    gr_tiles = [proj(OFF_GR + pair * TILE_N) for pair in range(N_HEADS // 2)]
    stage_b(3)
    pool_mix = jnp.concatenate(
        [(mixed[g] * pscale_ref[:, g * POOL_GROUP:(g + 1) * POOL_GROUP] * _silu(gp_tiles[g])).astype(BF16)
         for g in range(len(POOL_WINDOWS))], axis=-1)
    acc = x + _dot(pool_mix, _unpack_rows(wout_ref[0:D_POOL // 2, :]))

    per_piece = max(nb // OUT_PIECES, 1)
    for first in range(0, nb, per_piece):
        piece = range(first, first + per_piece)
        ret_y = []
        for b in piece:
            heads = []
            for h in range(N_HEADS):
                sub = slice((h % 2) * HEAD_DIM, (h % 2 + 1) * HEAD_DIM)
                o = outs.pop((b, h))
                rn = o * lax.rsqrt(jnp.mean(o * o, axis=-1, keepdims=True) + EPS)
                gate = _silu(rows(gr_tiles[h // 2], b)[:, sub])
                heads.append((rn * rnw_ref[:, _head(h)] * gate).astype(BF16))
            ret_y.append(jnp.concatenate(heads, axis=-1))
        ret_mix = jnp.concatenate(ret_y, axis=0)
        acc_piece = acc[first * CHUNK:(first + per_piece) * CHUNK]
        acc_piece = acc_piece + _dot(ret_mix, _unpack_rows(wout_ref[D_POOL // 2:D_MIX // 2, :]))
        y_ref[first:first + per_piece] = _rms(acc_piece, fnw_ref[...]).reshape(
            per_piece, CHUNK, D_MODEL)

    _sample_state_update(qkv_ref, sstate_ref, sout_ref, so_step, range(SAMPLE_HEADS_FIRST, N_HEADS))

    @pl.when(c == pl.num_programs(1) - 1)
    def _():
        last = jnp.concatenate([rows(hb, b)[CHUNK - HIST:] for b in batches], axis=0)
        u_last = _dot(last, _unpack_rows(wu_ref[...]))
        for b in batches:
            pbuf_ref[b] = u_last[b * HIST + HIST - POOL_BUF:(b + 1) * HIST]


def _main_call(x, cosf, sinf, smeta, umeta, normw, pscale, rnw, fnw, win_bf, wu_bf, wout_bf,
               qkv_s, state_s, rest_s, pool_t):
    batch, seq, _ = x.shape
    nb = PROMPT_BLOCK
    n_chunks = seq // CHUNK
    n_sample = state_s.shape[0]
    per = n_sample // ((batch // nb) * n_chunks)

    def const(shape):
        zeros = (0,) * len(shape)
        return pl.BlockSpec(shape, lambda b, c: zeros, pipeline_mode=pl.Buffered(1))

    step = lambda b, c: b * n_chunks + c
    pool_block = lambda b, c: step(b, c) * per // SUBLANES
    return pl.pallas_call(
        _main_kernel,
        grid=(batch // nb, n_chunks),
        in_specs=[
            pl.BlockSpec((nb, CHUNK, D_MODEL), lambda b, c: (b, c, 0)),
            const((seq, HEAD_DIM)),
            const((seq, HEAD_DIM)),
            const((N_HEADS, HEAD_DIM, HEAD_DIM)),
            const((N_META, D_POOL)),
            const((1, D_MODEL)),
            const((1, D_POOL)),
            const((1, D_RET)),
            const((1, D_MODEL)),
            const((D_MODEL // 2, D_IN_PROJ)),
            const((D_MODEL // 2, D_POOL)),
            const((D_MIX // 2, D_MODEL)),
            pl.BlockSpec((None, SUBLANES, 3 * D_RET), lambda b, c: (step(b, c), 0, 0)),
            pl.BlockSpec((per, N_HEADS, HEAD_DIM, HEAD_DIM), lambda b, c: (step(b, c), 0, 0, 0)),
            pl.BlockSpec((SUBLANES, D_POOL), lambda b, c: (pool_block(b, c), 0)),
            pl.BlockSpec((POOL_BUF, SUBLANES, D_POOL), lambda b, c: (0, pool_block(b, c), 0)),
        ],
        out_specs=[
            pl.BlockSpec((nb, CHUNK, D_MODEL), lambda b, c: (b, c, 0)),
            pl.BlockSpec((nb, N_HEADS, HEAD_DIM, HEAD_DIM), lambda b, c: (b, 0, 0, 0)),
            pl.BlockSpec((nb, POOL_BUF, D_POOL), lambda b, c: (b, 0, 0)),
            pl.BlockSpec((per, N_HEADS, HEAD_DIM, HEAD_DIM), lambda b, c: (step(b, c), 0, 0, 0)),
            pl.BlockSpec((qkv_s.shape[0], SUBLANES, D_RET), lambda b, c: (0, 0, 0)),
            pl.BlockSpec((POOL_BUF, SUBLANES, D_POOL), lambda b, c: (0, pool_block(b, c), 0)),
            pl.BlockSpec((SUBLANES, D_POOL), lambda b, c: (pool_block(b, c), 0)),
        ],
        out_shape=(jax.ShapeDtypeStruct((batch, seq, D_MODEL), F32),
                   jax.ShapeDtypeStruct((batch, N_HEADS, HEAD_DIM, HEAD_DIM), F32),
                   jax.ShapeDtypeStruct((batch, POOL_BUF, D_POOL), F32),
                   jax.ShapeDtypeStruct((n_sample, N_HEADS, HEAD_DIM, HEAD_DIM), F32),
                   jax.ShapeDtypeStruct((qkv_s.shape[0], SUBLANES, D_RET), F32),
                   jax.ShapeDtypeStruct((POOL_BUF, n_sample, D_POOL), F32),
                   jax.ShapeDtypeStruct((n_sample, D_POOL), F32)),
        scratch_shapes=[pltpu.VMEM((nb, HIST, D_POOL), F32)],
        compiler_params=pltpu.CompilerParams(
            dimension_semantics=("arbitrary", "arbitrary"), vmem_limit_bytes=VMEM_LIMIT_BYTES),
        name="main",
    )(x, cosf, sinf, smeta, umeta, normw, pscale, rnw, fnw, win_bf, wu_bf, wout_bf,
      qkv_s, state_s, rest_s, pool_t)


def _tail_kernel(xs_ref, rest_ref, so_ref, pooled_ref, pscale_ref, rnw_ref, fnw_ref, wpool_ref,
                 wout_ref, y_ref, o_ref):
    n_steps = so_ref.shape[0]
    per = xs_ref.shape[0] // n_steps
    for i in range(n_steps):
        o_ref[i * per:(i + 1) * per, :] = so_ref[i, 0:per, :]

    parts = []
    for g in range(len(POOL_WINDOWS)):
        cols = slice(g * POOL_GROUP, (g + 1) * POOL_GROUP)
        mixed = _dot(pooled_ref[:, cols].astype(BF16), _unpack_rows(wpool_ref[g])) * pscale_ref[:, cols]
        gate = _silu(rest_ref[:, D_POOL + g * POOL_GROUP:D_POOL + (g + 1) * POOL_GROUP])
        parts.append((mixed * gate).astype(BF16))
    for h in range(N_HEADS):
        o = o_ref[:, _head(h)] + rest_ref[:, 2 * D_POOL + D_RET + h * HEAD_DIM:
                                          2 * D_POOL + D_RET + (h + 1) * HEAD_DIM]
        rn = o * lax.rsqrt(jnp.mean(o * o, axis=-1, keepdims=True) + EPS)
        gr = rest_ref[:, 2 * D_POOL + h * HEAD_DIM:2 * D_POOL + (h + 1) * HEAD_DIM]
        parts.append((rn * rnw_ref[:, _head(h)] * _silu(gr)).astype(BF16))
    mix = jnp.concatenate(parts, axis=-1)
    y_ref[...] = _rms(xs_ref[...] + _dot(mix, _unpack_rows(wout_ref[...])), fnw_ref[...])


def _tail_call(xs, rest_s, so_s, pooled_s, pscale, rnw, fnw, wpool_bf, wout_bf):
    n = xs.shape[0]
    return pl.pallas_call(
        _tail_kernel,
        out_shape=jax.ShapeDtypeStruct((n, D_MODEL), F32),
        scratch_shapes=[pltpu.VMEM((n, D_RET), F32)],
        compiler_params=pltpu.CompilerParams(vmem_limit_bytes=VMEM_LIMIT_BYTES),
        name="tail",
    )(xs, rest_s, so_s, pooled_s, pscale, rnw, fnw, wpool_bf, wout_bf)


def _rotary_tables(pos):
    half = HEAD_DIM // 2
    inv = ROPE_BASE ** (-np.arange(half, dtype=np.float64) / half)
    ang = np.asarray(pos, np.float64)[:, None] * inv[None, :]
    cos, sin = np.cos(ang), np.sin(ang)
    return (jnp.asarray(np.concatenate([cos, cos], axis=-1), F32),
            jnp.asarray(np.concatenate([-sin, sin], axis=-1), F32))


def kernel(x_prompt, x_sample, state_ret, state_pool, meta_tokens, norm_w, w_in, w_pool,
           pool_scale, ret_norm_w, w_out, final_norm_w):
    assert norm_w.shape[0] == 1, "single-layer stack"
    batch, seq, _ = x_prompt.shape
    n_sample = x_sample.shape[0]
    n_steps = (batch // PROMPT_BLOCK) * (seq // CHUNK)
    per = n_sample // n_steps
    assert per * n_steps == n_sample and SUBLANES % per == 0
    normw, pscale, rnw = norm_w, pool_scale, ret_norm_w
    fnw = final_norm_w[None, :]
    xs = x_sample[:, 0, :]

    cos_p, sin_p = _rotary_tables(np.arange(N_META + seq))
    cos_s, sin_s = _rotary_tables(PAST_LEN + np.arange(1))

    win_bf, wu_bf, wout_bf, wpool_bf, smeta, umeta, qkv_s, rest_s = _prep_call(
        meta_tokens.astype(x_prompt.dtype), xs, cos_p[:N_META], sin_p[:N_META], cos_s, sin_s,
        normw, w_in[0], w_out[0], w_pool[0], n_steps)
    y_p, s_p, buf_p, s_s, so_s, buf_s, pooled_s = _main_call(
        x_prompt, cos_p[N_META:], sin_p[N_META:], smeta, umeta, normw, pscale, rnw, fnw,
        win_bf, wu_bf, wout_bf, qkv_s, state_ret[0], rest_s, jnp.transpose(state_pool[0], (1, 0, 2)))
    y_s = _tail_call(xs, rest_s, so_s, pooled_s, pscale, rnw, fnw, wpool_bf, wout_bf)
    return (y_p, y_s[:, None, :], s_p[None], s_s[None], buf_p[None],
            jnp.transpose(buf_s, (1, 0, 2))[None])
```

```python
import math

import jax
import jax.numpy as jnp
import numpy as np
from jax import lax
from jax.experimental import pallas as pl
from jax.experimental.pallas import tpu as pltpu

D_MODEL = 1024
D_POOL = 1024
D_RET = 1024
D_MIX = D_POOL + D_RET
POOL_WINDOWS = (2, 4, 8, 16)
POOL_GROUP = D_POOL // len(POOL_WINDOWS)
POOL_BUF = max(POOL_WINDOWS) - 1
N_HEADS = 8
HEAD_DIM = D_RET // N_HEADS
D_IN_PROJ = 2 * D_POOL + 4 * D_RET
N_META = 16
PAST_LEN = 16384
CHUNK = 128
ROPE_BASE = 10000.0
EPS = 1e-6
K_SCALE = HEAD_DIM ** -0.5

OFF_U, OFF_GP, OFF_Q, OFF_K, OFF_V, OFF_GR = (i * 1024 for i in range(6))

LOG_DECAY = tuple(math.log(1.0 - 2.0 ** (-5.0 - h)) for h in range(N_HEADS))

SUBLANES = 8
TILE_N = 256
PROMPT_BLOCK = 4
OUT_PIECES = 2
SAMPLE_HEADS_FIRST = 4
PREP_TILES = 8
HIST = 16

VMEM_LIMIT_BYTES = 60 * 1024 * 1024

F32 = jnp.float32
BF16 = jnp.bfloat16
PACKED = jnp.uint32


def _rms(x, w):
    return x * lax.rsqrt(jnp.mean(x * x, axis=-1, keepdims=True) + EPS) * w


def _silu(x):
    return x * (1.0 / (1.0 + jnp.exp(-x)))


def _dot(a, b):
    return jnp.dot(a, b, preferred_element_type=F32)


def _dot_t(a, b):
    return lax.dot_general(a, b, (((0,), (0,)), ((), ())), preferred_element_type=F32)


def _pack_rows(w):
    return pltpu.bitcast(w, PACKED)


def _unpack_rows(w):
    return pltpu.bitcast(w, BF16)


def _rotary(x, cosf, sinf):
    return x * cosf + pltpu.roll(x, HEAD_DIM // 2, 1) * sinf


def _head(h):
    return slice(h * HEAD_DIM, (h + 1) * HEAD_DIM)


def _prep_kernel(meta_ref, xs_ref, cos_ref, sin_ref, coss_ref, sins_ref, normw_ref,
                 win_ref, wout_ref, wpool_ref,
                 winb_ref, wub_ref, woutb_ref, wpoolb_ref, s_ref, u_ref, qkv_ref, rest_ref,
                 hb_ref, hs_ref, proj_ref, projs_ref, kd_ref, v_ref, stage_ref, wphi_ref, wplo_ref):
    j = pl.program_id(0)
    k_rows = D_MODEL // PREP_TILES

    @pl.when(j == 0)
    def _():
        hb = _rms(meta_ref[...], normw_ref[...]).astype(BF16)
        hs = _rms(xs_ref[...], normw_ref[...]).astype(BF16)
        for t in range(PREP_TILES):
            hb_ref[t] = hb[:, t * k_rows:(t + 1) * k_rows]
            hs_ref[t] = hs[:, t * k_rows:(t + 1) * k_rows]
        wp = wpool_ref[...]
        wp_hi = wp.astype(BF16)
        wpoolb_ref[...] = _pack_rows(wp_hi)
        wphi_ref[...] = wp_hi
        wplo_ref[...] = (wp - wp_hi.astype(F32)).astype(BF16)
        proj_ref[...] = jnp.zeros_like(proj_ref)
        projs_ref[...] = jnp.zeros_like(projs_ref)

    tile = win_ref[...].astype(BF16)
    folded = []
    for g in range(len(POOL_WINDOWS)):
        cols = slice(OFF_U + g * POOL_GROUP, OFF_U + (g + 1) * POOL_GROUP)
        a_hi = tile[:, cols]
        a_lo = (win_ref[:, cols] - a_hi.astype(F32)).astype(BF16)
        folded.append((_dot(a_hi, wphi_ref[g]) + _dot(a_hi, wplo_ref[g])
                       + _dot(a_lo, wphi_ref[g])).astype(BF16))
    tile_main = jnp.concatenate(folded + [tile[:, OFF_U + D_POOL:]], axis=-1)
    winb_ref[...] = _pack_rows(tile_main)
    wub_ref[...] = _pack_rows(tile[:, OFF_U:OFF_U + D_POOL])
    woutb_ref[...] = _pack_rows(wout_ref[...].astype(BF16))
    proj_ref[...] += _dot(hb_ref[j], tile_main)
    projs_ref[...] += _dot(hs_ref[j], tile)

    @pl.when(j == pl.num_programs(0) - 1)
    def _():
        cosf, sinf = cos_ref[...], sin_ref[...]
        row = lax.broadcasted_iota(jnp.int32, (N_META, HEAD_DIM), 0).astype(F32)
        kd_ref[...] = jnp.zeros_like(kd_ref)
        v_ref[...] = jnp.zeros_like(v_ref)
        u_ref[...] = proj_ref[:, OFF_U:OFF_U + D_POOL]
        for h in range(N_HEADS):
            kr = _rotary(proj_ref[:, OFF_K + h * HEAD_DIM:OFF_K + (h + 1) * HEAD_DIM], cosf, sinf)
            kd_ref[0:N_META, _head(h)] = kr * K_SCALE * jnp.exp(LOG_DECAY[h] * (N_META - 1.0 - row))
            v_ref[0:N_META, _head(h)] = proj_ref[:, OFF_V + h * HEAD_DIM:OFF_V + (h + 1) * HEAD_DIM]
        for h in range(N_HEADS):
            s_ref[h] = _dot_t(kd_ref[:, _head(h)].astype(BF16), v_ref[:, _head(h)].astype(BF16))

        coss, sins = coss_ref[...], sins_ref[...]
        rest_ref[:, 0:D_POOL] = projs_ref[:, OFF_U:OFF_U + D_POOL]
        rest_ref[:, D_POOL:2 * D_POOL] = projs_ref[:, OFF_GP:OFF_GP + D_POOL]
        rest_ref[:, 2 * D_POOL:2 * D_POOL + D_RET] = projs_ref[:, OFF_GR:OFF_GR + D_RET]
        for h in range(N_HEADS):
            qr = _rotary(projs_ref[:, OFF_Q + h * HEAD_DIM:OFF_Q + (h + 1) * HEAD_DIM], coss, sins)
            kr = _rotary(projs_ref[:, OFF_K + h * HEAD_DIM:OFF_K + (h + 1) * HEAD_DIM], coss, sins)
            kr = kr * K_SCALE
            v = projs_ref[:, OFF_V + h * HEAD_DIM:OFF_V + (h + 1) * HEAD_DIM]
            stage_ref[:, _head(h)] = qr * math.exp(LOG_DECAY[h])
            stage_ref[:, D_RET + h * HEAD_DIM:D_RET + (h + 1) * HEAD_DIM] = kr
            stage_ref[:, 2 * D_RET + h * HEAD_DIM:2 * D_RET + (h + 1) * HEAD_DIM] = v
            rest_ref[:, 2 * D_POOL + D_RET + h * HEAD_DIM:2 * D_POOL + D_RET + (h + 1) * HEAD_DIM] = (
                jnp.sum(qr * kr, axis=-1, keepdims=True) * v)
        n_steps, per = qkv_ref.shape[0], stage_ref.shape[0] // qkv_ref.shape[0]
        qkv_ref[...] = jnp.zeros_like(qkv_ref)
        for i in range(n_steps):
            qkv_ref[i, 0:per, :] = stage_ref[i * per:(i + 1) * per, :]


def _prep_call(meta, xs, cosf, sinf, coss, sins, normw, w_in, w_out, w_pool, n_steps):
    n = xs.shape[0]

    def const(shape):
        zeros = (0,) * len(shape)
        return pl.BlockSpec(shape, lambda j: zeros)

    row_tile = lambda j: (j, 0)
    return pl.pallas_call(
        _prep_kernel,
        grid=(PREP_TILES,),
        in_specs=[
            const((N_META, D_MODEL)),
            const((n, D_MODEL)),
            const((N_META, HEAD_DIM)),
            const((N_META, HEAD_DIM)),
            const((1, HEAD_DIM)),
            const((1, HEAD_DIM)),
            const((1, D_MODEL)),
            pl.BlockSpec((D_MODEL // PREP_TILES, D_IN_PROJ), row_tile),
            pl.BlockSpec((D_MIX // PREP_TILES, D_MODEL), row_tile),
            const((len(POOL_WINDOWS), POOL_GROUP, POOL_GROUP)),
        ],
        out_specs=[
            pl.BlockSpec((D_MODEL // PREP_TILES // 2, D_IN_PROJ), row_tile),
            pl.BlockSpec((D_MODEL // PREP_TILES // 2, D_POOL), row_tile),
            pl.BlockSpec((D_MIX // PREP_TILES // 2, D_MODEL), row_tile),
            const((len(POOL_WINDOWS), POOL_GROUP // 2, POOL_GROUP)),
            const((N_HEADS, HEAD_DIM, HEAD_DIM)),
            const((N_META, D_POOL)),
            const((n_steps, SUBLANES, 3 * D_RET)),
            const((n, 2 * D_POOL + 2 * D_RET)),
        ],
        out_shape=(jax.ShapeDtypeStruct((D_MODEL // 2, D_IN_PROJ), PACKED),
                   jax.ShapeDtypeStruct((D_MODEL // 2, D_POOL), PACKED),
                   jax.ShapeDtypeStruct((D_MIX // 2, D_MODEL), PACKED),
                   jax.ShapeDtypeStruct((len(POOL_WINDOWS), POOL_GROUP // 2, POOL_GROUP), PACKED),
                   jax.ShapeDtypeStruct((N_HEADS, HEAD_DIM, HEAD_DIM), F32),
                   jax.ShapeDtypeStruct((N_META, D_POOL), F32),
                   jax.ShapeDtypeStruct((n_steps, SUBLANES, 3 * D_RET), F32),
                   jax.ShapeDtypeStruct((n, 2 * D_POOL + 2 * D_RET), F32)),
        scratch_shapes=[pltpu.VMEM((PREP_TILES, N_META, D_MODEL // PREP_TILES), BF16),
                        pltpu.VMEM((PREP_TILES, n, D_MODEL // PREP_TILES), BF16),
                        pltpu.VMEM((N_META, D_IN_PROJ), F32),
                        pltpu.VMEM((n, D_IN_PROJ), F32),
                        pltpu.VMEM((CHUNK, D_RET), F32),
                        pltpu.VMEM((CHUNK, D_RET), F32),
                        pltpu.VMEM((n, 3 * D_RET), F32),
                        pltpu.VMEM((len(POOL_WINDOWS), POOL_GROUP, POOL_GROUP), BF16),
                        pltpu.VMEM((len(POOL_WINDOWS), POOL_GROUP, POOL_GROUP), BF16)],
        compiler_params=pltpu.CompilerParams(
            dimension_semantics=("arbitrary",), vmem_limit_bytes=VMEM_LIMIT_BYTES),
        name="prep",
    )(meta, xs, cosf, sinf, coss, sins, normw, w_in, w_out, w_pool)


def _sample_state_update(qkv_ref, state_ref, sout_ref, o_ref, heads):
    per = state_ref.shape[0]
    seq_of_row = lax.broadcasted_iota(jnp.int32, (SUBLANES, per * HEAD_DIM), 0)
    seq_of_col = lax.broadcasted_iota(jnp.int32, (SUBLANES, per * HEAD_DIM), 1) // HEAD_DIM
    own_block = seq_of_row == seq_of_col
    for h in heads:
        g1 = math.exp(LOG_DECAY[h])
        qd = qkv_ref[:, _head(h)]
        kt = qkv_ref[:, D_RET + h * HEAD_DIM:D_RET + (h + 1) * HEAD_DIM].T
        v8 = qkv_ref[:, 2 * D_RET + h * HEAD_DIM:2 * D_RET + (h + 1) * HEAD_DIM]
        state = state_ref[:, h]
        q_blocks = jnp.where(own_block, jnp.concatenate([qd] * per, axis=-1), 0.0)
        o_ref[:, _head(h)] = _dot(q_blocks.astype(BF16),
                                  state.reshape(per * HEAD_DIM, HEAD_DIM).astype(BF16))
        for j in range(per):
            sout_ref[j, h] = g1 * state[j] + kt[:, j:j + 1] * v8[j:j + 1, :]


def _sample_pool_update(us_ref, pin_ref, pout_ref, pooled_ref):
    u = us_ref[...]
    pout_ref[0:POOL_BUF - 1] = pin_ref[1:POOL_BUF]
    pout_ref[POOL_BUF - 1] = u
    for g, w in enumerate(POOL_WINDOWS):
        cols = slice(g * POOL_GROUP, (g + 1) * POOL_GROUP)
        win_sum = u[:, cols]
        for r in range(POOL_BUF - (w - 1), POOL_BUF):
            win_sum = win_sum + pin_ref[r, :, cols]
        pooled_ref[:, cols] = win_sum / float(w) - u[:, cols]


def _main_kernel(x_ref, cos_ref, sin_ref, smeta_ref, umeta_ref, normw_ref, pscale_ref, rnw_ref,
                 fnw_ref, win_ref, wu_ref, wout_ref, qkv_ref, sstate_ref, us_ref, pin_ref,
                 xs_ref, rest_ref, wpool_ref,
                 y_ref, s_ref, pbuf_ref, sout_ref, pout_ref, ys_ref,
                 hist_ref, so_ref, pooled_ref, o_ref):
    c = pl.program_id(1)
    nb = x_ref.shape[0]
    batches = range(nb)

    @pl.when(c == 0)
    def _():
        for b in batches:
            s_ref[b] = smeta_ref[...]
            hist_ref[b] = umeta_ref[...]

    x = x_ref[...].reshape(nb * CHUNK, D_MODEL)
    hb = _rms(x, normw_ref[...]).astype(BF16)

    def proj(off):
        return _dot(hb, _unpack_rows(win_ref[:, off:off + TILE_N]))

    def rows(t, b):
        return t[b * CHUNK:(b + 1) * CHUNK]

    cosf, sinf = cos_ref[...], sin_ref[...]
    li = lax.broadcasted_iota(jnp.int32, (CHUNK, CHUNK), 0).astype(F32)
    mi = lax.broadcasted_iota(jnp.int32, (CHUNK, CHUNK), 1).astype(F32)
    diff = li - mi

    qkv, part, outs = {}, {}, {}

    def issue_qkv(pair):
        qkv[pair] = tuple(proj(off + pair * TILE_N) for off in (OFF_Q, OFF_K, OFF_V))

    def stage_a(pair):
        q2, k2, v2 = qkv.pop(pair)
        for i in range(2):
            h = 2 * pair + i
            lg = LOG_DECAY[h]
            sub = slice(i * HEAD_DIM, (i + 1) * HEAD_DIM)
            dmask = jnp.where(diff >= 0.0, jnp.exp(lg * jnp.maximum(diff, 0.0)), 0.0)
            q_decay = jnp.exp(lg * (li + 1.0))
            k_decay = jnp.exp(lg * (CHUNK - 1.0 - li))
            for b in batches:
                qr = _rotary(rows(q2, b)[:, sub], cosf, sinf)
                kr = _rotary(rows(k2, b)[:, sub], cosf, sinf) * K_SCALE
                vb = rows(v2, b)[:, sub].astype(BF16)
                state = s_ref[b, h]
                scores = lax.dot_general(qr.astype(BF16), kr.astype(BF16), (((1,), (1,)), ((), ())),
                                         preferred_element_type=F32)
                cross = _dot((qr * q_decay).astype(BF16), state.astype(BF16))
                s_ref[b, h] = math.exp(lg * CHUNK) * state + _dot_t((kr * k_decay).astype(BF16), vb)
                part[b, h] = ((scores * dmask).astype(BF16), vb, cross)

    def stage_b(pair):
        for i in range(2):
            h = 2 * pair + i
            for b in batches:
                p, vb, cross = part.pop((b, h))
                outs[b, h] = _dot(p, vb) + cross

    def pool_windows(u_tiles):
        pooled = []
        for g, w in enumerate(POOL_WINDOWS):
            cols = slice(g * POOL_GROUP, (g + 1) * POOL_GROUP)
            per_batch = []
            for b in batches:
                u = rows(u_tiles[g], b)
                ext = jnp.concatenate([hist_ref[b, :, cols], u], axis=0)
                win_sum = ext
                shift = 1
                while shift < w:
                    win_sum = win_sum + pltpu.roll(win_sum, shift, 0)
                    shift *= 2
                per_batch.append(win_sum[HIST:] / float(w) - u)
                hist_ref[b, :, cols] = u[CHUNK - HIST:]
            pooled.append(jnp.concatenate(per_batch, axis=0))
        return pooled

    step = pl.program_id(0) * pl.num_programs(1) + c
    so_step = so_ref.at[step]
    _sample_state_update(qkv_ref, sstate_ref, sout_ref, so_step, range(0, SAMPLE_HEADS_FIRST))
    block_rows = pl.ds(pl.multiple_of(step * sstate_ref.shape[0] // SUBLANES * SUBLANES, SUBLANES),
                       SUBLANES)
    _sample_pool_update(us_ref, pin_ref, pout_ref, pooled_ref.at[block_rows])
    issue_qkv(0)
    issue_qkv(1)
    stage_a(0)
    issue_qkv(2)
    stage_a(1)
    stage_b(0)
    issue_qkv(3)
    u_tiles = [proj(OFF_U + g * POOL_GROUP) for g in range(len(POOL_WINDOWS))]
    stage_a(2)
    stage_b(1)
    gp_tiles = [proj(OFF_GP + g * POOL_GROUP) for g in range(len(POOL_WINDOWS))]
    mixed = pool_windows(u_tiles)
    stage_a(3)
    stage_b(2)
    gr_tiles = [proj(OFF_GR + pair * TILE_N) for pair in range(N_HEADS // 2)]
    stage_b(3)
    pool_mix = jnp.concatenate(
        [(mixed[g] * pscale_ref[:, g * POOL_GROUP:(g + 1) * POOL_GROUP] * _silu(gp_tiles[g])).astype(BF16)
         for g in range(len(POOL_WINDOWS))], axis=-1)
    acc = x + _dot(pool_mix, _unpack_rows(wout_ref[0:D_POOL // 2, :]))

    per_piece = max(nb // OUT_PIECES, 1)
    for first in range(0, nb, per_piece):
        piece = range(first, first + per_piece)
        ret_y = []
        for b in piece:
            heads = []
            for h in range(N_HEADS):
                sub = slice((h % 2) * HEAD_DIM, (h % 2 + 1) * HEAD_DIM)
                o = outs.pop((b, h))
                rn = o * lax.rsqrt(jnp.mean(o * o, axis=-1, keepdims=True) + EPS)
                gate = _silu(rows(gr_tiles[h // 2], b)[:, sub])
                heads.append((rn * rnw_ref[:, _head(h)] * gate).astype(BF16))
            ret_y.append(jnp.concatenate(heads, axis=-1))
        ret_mix = jnp.concatenate(ret_y, axis=0)
        acc_piece = acc[first * CHUNK:(first + per_piece) * CHUNK]
        acc_piece = acc_piece + _dot(ret_mix, _unpack_rows(wout_ref[D_POOL // 2:D_MIX // 2, :]))
        y_ref[first:first + per_piece] = _rms(acc_piece, fnw_ref[...]).reshape(
            per_piece, CHUNK, D_MODEL)

    _sample_state_update(qkv_ref, sstate_ref, sout_ref, so_step, range(SAMPLE_HEADS_FIRST, N_HEADS))

    @pl.when(step == pl.num_programs(0) * pl.num_programs(1) - 1)
    def _():
        _sample_tail(xs_ref, rest_ref, so_ref, pooled_ref, pscale_ref, rnw_ref, fnw_ref, wpool_ref,
                     wout_ref, ys_ref, o_ref)

    @pl.when(c == pl.num_programs(1) - 1)
    def _():
        last = jnp.concatenate([rows(hb, b)[CHUNK - HIST:] for b in batches], axis=0)
        u_last = _dot(last, _unpack_rows(wu_ref[...]))
        for b in batches:
            pbuf_ref[b] = u_last[b * HIST + HIST - POOL_BUF:(b + 1) * HIST]


def _main_call(x, cosf, sinf, smeta, umeta, normw, pscale, rnw, fnw, win_bf, wu_bf, wout_bf,
               qkv_s, state_s, rest_s, pool_t, xs, wpool_bf):
    batch, seq, _ = x.shape
    nb = PROMPT_BLOCK
    n_chunks = seq // CHUNK
    n_sample = state_s.shape[0]
    per = n_sample // ((batch // nb) * n_chunks)

    def const(shape):
        zeros = (0,) * len(shape)
        return pl.BlockSpec(shape, lambda b, c: zeros, pipeline_mode=pl.Buffered(1))

    step = lambda b, c: b * n_chunks + c
    pool_block = lambda b, c: step(b, c) * per // SUBLANES
    return pl.pallas_call(
        _main_kernel,
        grid=(batch // nb, n_chunks),
        in_specs=[
            pl.BlockSpec((nb, CHUNK, D_MODEL), lambda b, c: (b, c, 0)),
            pl.BlockSpec((CHUNK, HEAD_DIM), lambda b, c: (c, 0)),
            pl.BlockSpec((CHUNK, HEAD_DIM), lambda b, c: (c, 0)),
            const((N_HEADS, HEAD_DIM, HEAD_DIM)),
            const((N_META, D_POOL)),
            const((1, D_MODEL)),
            const((1, D_POOL)),
            const((1, D_RET)),
            const((1, D_MODEL)),
            const((D_MODEL // 2, D_IN_PROJ)),
            const((D_MODEL // 2, D_POOL)),
            const((D_MIX // 2, D_MODEL)),
            pl.BlockSpec((None, SUBLANES, 3 * D_RET), lambda b, c: (step(b, c), 0, 0)),
            pl.BlockSpec((per, N_HEADS, HEAD_DIM, HEAD_DIM), lambda b, c: (step(b, c), 0, 0, 0)),
            pl.BlockSpec((SUBLANES, D_POOL), lambda b, c: (pool_block(b, c), 0)),
            pl.BlockSpec((POOL_BUF, SUBLANES, D_POOL), lambda b, c: (0, pool_block(b, c), 0)),
            const((n_sample, D_MODEL)),
            const((n_sample, 2 * D_POOL + 2 * D_RET)),
            const((len(POOL_WINDOWS), POOL_GROUP // 2, POOL_GROUP)),
        ],
        out_specs=[
            pl.BlockSpec((nb, CHUNK, D_MODEL), lambda b, c: (b, c, 0)),
            pl.BlockSpec((nb, N_HEADS, HEAD_DIM, HEAD_DIM), lambda b, c: (b, 0, 0, 0)),
            pl.BlockSpec((nb, POOL_BUF, D_POOL), lambda b, c: (b, 0, 0)),
            pl.BlockSpec((per, N_HEADS, HEAD_DIM, HEAD_DIM), lambda b, c: (step(b, c), 0, 0, 0)),
            pl.BlockSpec((POOL_BUF, SUBLANES, D_POOL), lambda b, c: (0, pool_block(b, c), 0)),
            pl.BlockSpec((n_sample, D_MODEL), lambda b, c: (0, 0)),
        ],
        out_shape=(jax.ShapeDtypeStruct((batch, seq, D_MODEL), F32),
                   jax.ShapeDtypeStruct((batch, N_HEADS, HEAD_DIM, HEAD_DIM), F32),
                   jax.ShapeDtypeStruct((batch, POOL_BUF, D_POOL), F32),
                   jax.ShapeDtypeStruct((n_sample, N_HEADS, HEAD_DIM, HEAD_DIM), F32),
                   jax.ShapeDtypeStruct((POOL_BUF, n_sample, D_POOL), F32),
                   jax.ShapeDtypeStruct((n_sample, D_MODEL), F32)),
        scratch_shapes=[pltpu.VMEM((nb, HIST, D_POOL), F32),
                        pltpu.VMEM((qkv_s.shape[0], SUBLANES, D_RET), F32),
                        pltpu.VMEM((n_sample, D_POOL), F32),
                        pltpu.VMEM((n_sample, D_RET), F32)],
        compiler_params=pltpu.CompilerParams(
            dimension_semantics=("arbitrary", "arbitrary"), vmem_limit_bytes=VMEM_LIMIT_BYTES),
        name="main",
    )(x, cosf, sinf, smeta, umeta, normw, pscale, rnw, fnw, win_bf, wu_bf, wout_bf,
      qkv_s, state_s, rest_s, pool_t, xs, rest_s, wpool_bf)


def _sample_tail(xs_ref, rest_ref, so_ref, pooled_ref, pscale_ref, rnw_ref, fnw_ref, wpool_ref,
                 wout_ref, y_ref, o_ref):
    n_steps = so_ref.shape[0]
    per = xs_ref.shape[0] // n_steps
    for i in range(n_steps):
        o_ref[i * per:(i + 1) * per, :] = so_ref[i, 0:per, :]

    parts = []
    for g in range(len(POOL_WINDOWS)):
        cols = slice(g * POOL_GROUP, (g + 1) * POOL_GROUP)
        mixed = _dot(pooled_ref[:, cols].astype(BF16), _unpack_rows(wpool_ref[g])) * pscale_ref[:, cols]
        gate = _silu(rest_ref[:, D_POOL + g * POOL_GROUP:D_POOL + (g + 1) * POOL_GROUP])
        parts.append((mixed * gate).astype(BF16))
    for h in range(N_HEADS):
        o = o_ref[:, _head(h)] + rest_ref[:, 2 * D_POOL + D_RET + h * HEAD_DIM:
                                          2 * D_POOL + D_RET + (h + 1) * HEAD_DIM]
        rn = o * lax.rsqrt(jnp.mean(o * o, axis=-1, keepdims=True) + EPS)
        gr = rest_ref[:, 2 * D_POOL + h * HEAD_DIM:2 * D_POOL + (h + 1) * HEAD_DIM]
        parts.append((rn * rnw_ref[:, _head(h)] * _silu(gr)).astype(BF16))
    mix = jnp.concatenate(parts, axis=-1)
    y_ref[...] = _rms(xs_ref[...] + _dot(mix, _unpack_rows(wout_ref[...])), fnw_ref[...])


def _rotary_tables(pos):
    half = HEAD_DIM // 2
    inv = ROPE_BASE ** (-np.arange(half, dtype=np.float64) / half)
    ang = np.asarray(pos, np.float64)[:, None] * inv[None, :]
    cos, sin = np.cos(ang), np.sin(ang)
    return (jnp.asarray(np.concatenate([cos, cos], axis=-1), F32),
            jnp.asarray(np.concatenate([-sin, sin], axis=-1), F32))


def kernel(x_prompt, x_sample, state_ret, state_pool, meta_tokens, norm_w, w_in, w_pool,
           pool_scale, ret_norm_w, w_out, final_norm_w):
    assert norm_w.shape[0] == 1, "single-layer stack"
    batch, seq, _ = x_prompt.shape
    n_sample = x_sample.shape[0]
    n_steps = (batch // PROMPT_BLOCK) * (seq // CHUNK)
    per = n_sample // n_steps
    assert per * n_steps == n_sample and SUBLANES % per == 0
    normw, pscale, rnw = norm_w, pool_scale, ret_norm_w
    fnw = final_norm_w[None, :]
    xs = x_sample[:, 0, :]

    cos_p, sin_p = _rotary_tables(np.arange(N_META + seq))
    cos_s, sin_s = _rotary_tables(PAST_LEN + np.arange(1))

    win_bf, wu_bf, wout_bf, wpool_bf, smeta, umeta, qkv_s, rest_s = _prep_call(
        meta_tokens.astype(x_prompt.dtype), xs, cos_p[:N_META], sin_p[:N_META], cos_s, sin_s,
        normw, w_in[0], w_out[0], w_pool[0], n_steps)
    y_p, s_p, buf_p, s_s, buf_s, y_s = _main_call(
        x_prompt, cos_p[N_META:], sin_p[N_META:], smeta, umeta, normw, pscale, rnw, fnw,
        win_bf, wu_bf, wout_bf, qkv_s, state_ret[0], rest_s, jnp.transpose(state_pool[0], (1, 0, 2)),
        xs, wpool_bf)
    return (y_p, y_s[:, None, :], s_p[None], s_s[None], buf_p[None],
            jnp.transpose(buf_s, (1, 0, 2))[None])
```

```python
import math

import jax
import jax.numpy as jnp
import numpy as np
from jax import lax
from jax.experimental import pallas as pl
from jax.experimental.pallas import tpu as pltpu

D_MODEL = 1024
D_POOL = 1024
D_RET = 1024
D_MIX = D_POOL + D_RET
POOL_WINDOWS = (2, 4, 8, 16)
POOL_GROUP = D_POOL // len(POOL_WINDOWS)
POOL_BUF = max(POOL_WINDOWS) - 1
N_HEADS = 8
HEAD_DIM = D_RET // N_HEADS
D_IN_PROJ = 2 * D_POOL + 4 * D_RET
N_META = 16
PAST_LEN = 16384
CHUNK = 128
ROPE_BASE = 10000.0
EPS = 1e-6
K_SCALE = HEAD_DIM ** -0.5

OFF_U, OFF_GP, OFF_Q, OFF_K, OFF_V, OFF_GR = (i * 1024 for i in range(6))

OFF_REST_X = 2 * D_POOL + 2 * D_RET
D_REST = OFF_REST_X + D_MODEL

LOG_DECAY = tuple(math.log(1.0 - 2.0 ** (-5.0 - h)) for h in range(N_HEADS))

SUBLANES = 8
TILE_N = 256
PROMPT_BLOCK = 4
OUT_PIECES = 2
SAMPLE_HEADS_FIRST = 4
PREP_TILES = 8
HIST = 16

VMEM_LIMIT_BYTES = 60 * 1024 * 1024

F32 = jnp.float32
BF16 = jnp.bfloat16
PACKED = jnp.uint32


def _rms(x, w):
    return x * lax.rsqrt(jnp.mean(x * x, axis=-1, keepdims=True) + EPS) * w


def _silu(x):
    return x * (1.0 / (1.0 + jnp.exp(-x)))


def _dot(a, b):
    return jnp.dot(a, b, preferred_element_type=F32)


def _dot_t(a, b):
    return lax.dot_general(a, b, (((0,), (0,)), ((), ())), preferred_element_type=F32)


def _pack_rows(w):
    return pltpu.bitcast(w, PACKED)


def _unpack_rows(w):
    return pltpu.bitcast(w, BF16)


def _rotary(x, cosf, sinf):
    return x * cosf + pltpu.roll(x, HEAD_DIM // 2, 1) * sinf


def _head(h):
    return slice(h * HEAD_DIM, (h + 1) * HEAD_DIM)


def _prep_kernel(meta_ref, xs_ref, cos_ref, sin_ref, coss_ref, sins_ref, normw_ref,
                 win_ref, wout_ref, wpool_ref,
                 winb_ref, wub_ref, woutb_ref, wpoolb_ref, s_ref, u_ref, qkv_ref, rest_ref,
                 hb_ref, hs_ref, proj_ref, projs_ref, kd_ref, v_ref, stage_ref, wphi_ref, wplo_ref):
    j = pl.program_id(0)
    k_rows = D_MODEL // PREP_TILES

    @pl.when(j == 0)
    def _():
        hb = _rms(meta_ref[...], normw_ref[...]).astype(BF16)
        xs = xs_ref[:, 0, :]
        rest_ref[:, OFF_REST_X:OFF_REST_X + D_MODEL] = xs
        hs = _rms(xs, normw_ref[...]).astype(BF16)
        for t in range(PREP_TILES):
            hb_ref[t] = hb[:, t * k_rows:(t + 1) * k_rows]
            hs_ref[t] = hs[:, t * k_rows:(t + 1) * k_rows]
        wp = wpool_ref[...]
        wp_hi = wp.astype(BF16)
        wpoolb_ref[...] = _pack_rows(wp_hi)
        wphi_ref[...] = wp_hi
        wplo_ref[...] = (wp - wp_hi.astype(F32)).astype(BF16)
        proj_ref[...] = jnp.zeros_like(proj_ref)
        projs_ref[...] = jnp.zeros_like(projs_ref)

    tile = win_ref[...].astype(BF16)
    folded = []
    for g in range(len(POOL_WINDOWS)):
        cols = slice(OFF_U + g * POOL_GROUP, OFF_U + (g + 1) * POOL_GROUP)
        a_hi = tile[:, cols]
        a_lo = (win_ref[:, cols] - a_hi.astype(F32)).astype(BF16)
        folded.append((_dot(a_hi, wphi_ref[g]) + _dot(a_hi, wplo_ref[g])
                       + _dot(a_lo, wphi_ref[g])).astype(BF16))
    tile_main = jnp.concatenate(folded + [tile[:, OFF_U + D_POOL:]], axis=-1)
    winb_ref[...] = _pack_rows(tile_main)
    wub_ref[...] = _pack_rows(tile[:, OFF_U:OFF_U + D_POOL])
    woutb_ref[...] = _pack_rows(wout_ref[...].astype(BF16))
    proj_ref[...] += _dot(hb_ref[j], tile_main)
    projs_ref[...] += _dot(hs_ref[j], tile)

    @pl.when(j == pl.num_programs(0) - 1)
    def _():
        cosf, sinf = cos_ref[...], sin_ref[...]
        row = lax.broadcasted_iota(jnp.int32, (N_META, HEAD_DIM), 0).astype(F32)
        kd_ref[...] = jnp.zeros_like(kd_ref)
        v_ref[...] = jnp.zeros_like(v_ref)
        u_ref[...] = proj_ref[:, OFF_U:OFF_U + D_POOL]
        for h in range(N_HEADS):
            kr = _rotary(proj_ref[:, OFF_K + h * HEAD_DIM:OFF_K + (h + 1) * HEAD_DIM], cosf, sinf)
            kd_ref[0:N_META, _head(h)] = kr * K_SCALE * jnp.exp(LOG_DECAY[h] * (N_META - 1.0 - row))
            v_ref[0:N_META, _head(h)] = proj_ref[:, OFF_V + h * HEAD_DIM:OFF_V + (h + 1) * HEAD_DIM]
        for h in range(N_HEADS):
            s_ref[h] = _dot_t(kd_ref[:, _head(h)].astype(BF16), v_ref[:, _head(h)].astype(BF16))

        coss, sins = coss_ref[...], sins_ref[...]
        rest_ref[:, 0:D_POOL] = projs_ref[:, OFF_U:OFF_U + D_POOL]
        rest_ref[:, D_POOL:2 * D_POOL] = projs_ref[:, OFF_GP:OFF_GP + D_POOL]
        rest_ref[:, 2 * D_POOL:2 * D_POOL + D_RET] = projs_ref[:, OFF_GR:OFF_GR + D_RET]
        for h in range(N_HEADS):
            qr = _rotary(projs_ref[:, OFF_Q + h * HEAD_DIM:OFF_Q + (h + 1) * HEAD_DIM], coss, sins)
            kr = _rotary(projs_ref[:, OFF_K + h * HEAD_DIM:OFF_K + (h + 1) * HEAD_DIM], coss, sins)
            kr = kr * K_SCALE
            v = projs_ref[:, OFF_V + h * HEAD_DIM:OFF_V + (h + 1) * HEAD_DIM]
            stage_ref[:, _head(h)] = qr * math.exp(LOG_DECAY[h])
            stage_ref[:, D_RET + h * HEAD_DIM:D_RET + (h + 1) * HEAD_DIM] = kr
            stage_ref[:, 2 * D_RET + h * HEAD_DIM:2 * D_RET + (h + 1) * HEAD_DIM] = v
            rest_ref[:, 2 * D_POOL + D_RET + h * HEAD_DIM:2 * D_POOL + D_RET + (h + 1) * HEAD_DIM] = (
                jnp.sum(qr * kr, axis=-1, keepdims=True) * v)
        n_steps, per = qkv_ref.shape[0], stage_ref.shape[0] // qkv_ref.shape[0]
        qkv_ref[...] = jnp.zeros_like(qkv_ref)
        for i in range(n_steps):
            qkv_ref[i, 0:per, :] = stage_ref[i * per:(i + 1) * per, :]


def _prep_call(meta, xs, cosf, sinf, coss, sins, normw, w_in, w_out, w_pool, n_steps):
    n = xs.shape[0]

    def const(shape):
        zeros = (0,) * len(shape)
        return pl.BlockSpec(shape, lambda j: zeros)

    row_tile = lambda j: (j, 0)
    return pl.pallas_call(
        _prep_kernel,
        grid=(PREP_TILES,),
        in_specs=[
            const((N_META, D_MODEL)),
            const((n, 1, D_MODEL)),
            const((N_META, HEAD_DIM)),
            const((N_META, HEAD_DIM)),
            const((1, HEAD_DIM)),
            const((1, HEAD_DIM)),
            const((1, D_MODEL)),
            pl.BlockSpec((D_MODEL // PREP_TILES, D_IN_PROJ), row_tile),
            pl.BlockSpec((D_MIX // PREP_TILES, D_MODEL), row_tile),
            const((len(POOL_WINDOWS), POOL_GROUP, POOL_GROUP)),
        ],
        out_specs=[
            pl.BlockSpec((D_MODEL // PREP_TILES // 2, D_IN_PROJ), row_tile),
            pl.BlockSpec((D_MODEL // PREP_TILES // 2, D_POOL), row_tile),
            pl.BlockSpec((D_MIX // PREP_TILES // 2, D_MODEL), row_tile),
            const((len(POOL_WINDOWS), POOL_GROUP // 2, POOL_GROUP)),
            const((N_HEADS, HEAD_DIM, HEAD_DIM)),
            const((N_META, D_POOL)),
            const((n_steps, SUBLANES, 3 * D_RET)),
            const((n, D_REST)),
        ],
        out_shape=(jax.ShapeDtypeStruct((D_MODEL // 2, D_IN_PROJ), PACKED),
                   jax.ShapeDtypeStruct((D_MODEL // 2, D_POOL), PACKED),
                   jax.ShapeDtypeStruct((D_MIX // 2, D_MODEL), PACKED),
                   jax.ShapeDtypeStruct((len(POOL_WINDOWS), POOL_GROUP // 2, POOL_GROUP), PACKED),
                   jax.ShapeDtypeStruct((N_HEADS, HEAD_DIM, HEAD_DIM), F32),
                   jax.ShapeDtypeStruct((N_META, D_POOL), F32),
                   jax.ShapeDtypeStruct((n_steps, SUBLANES, 3 * D_RET), F32),
                   jax.ShapeDtypeStruct((n, D_REST), F32)),
        scratch_shapes=[pltpu.VMEM((PREP_TILES, N_META, D_MODEL // PREP_TILES), BF16),
                        pltpu.VMEM((PREP_TILES, n, D_MODEL // PREP_TILES), BF16),
                        pltpu.VMEM((N_META, D_IN_PROJ), F32),
                        pltpu.VMEM((n, D_IN_PROJ), F32),
                        pltpu.VMEM((CHUNK, D_RET), F32),
                        pltpu.VMEM((CHUNK, D_RET), F32),
                        pltpu.VMEM((n, 3 * D_RET), F32),
                        pltpu.VMEM((len(POOL_WINDOWS), POOL_GROUP, POOL_GROUP), BF16),
                        pltpu.VMEM((len(POOL_WINDOWS), POOL_GROUP, POOL_GROUP), BF16)],
        compiler_params=pltpu.CompilerParams(
            dimension_semantics=("arbitrary",), vmem_limit_bytes=VMEM_LIMIT_BYTES),
        name="prep",
    )(meta, xs, cosf, sinf, coss, sins, normw, w_in, w_out, w_pool)


def _sample_state_update(qkv_ref, state_ref, sout_ref, o_ref, heads):
    per = state_ref.shape[0]
    seq_of_row = lax.broadcasted_iota(jnp.int32, (SUBLANES, per * HEAD_DIM), 0)
    seq_of_col = lax.broadcasted_iota(jnp.int32, (SUBLANES, per * HEAD_DIM), 1) // HEAD_DIM
    own_block = seq_of_row == seq_of_col
    for h in heads:
        g1 = math.exp(LOG_DECAY[h])
        qd = qkv_ref[:, _head(h)]
        kt = qkv_ref[:, D_RET + h * HEAD_DIM:D_RET + (h + 1) * HEAD_DIM].T
        v8 = qkv_ref[:, 2 * D_RET + h * HEAD_DIM:2 * D_RET + (h + 1) * HEAD_DIM]
        state = state_ref[:, h]
        q_blocks = jnp.where(own_block, jnp.concatenate([qd] * per, axis=-1), 0.0)
        o_ref[:, _head(h)] = _dot(q_blocks.astype(BF16),
                                  state.reshape(per * HEAD_DIM, HEAD_DIM).astype(BF16))
        for j in range(per):
            sout_ref[j, h] = g1 * state[j] + kt[:, j:j + 1] * v8[j:j + 1, :]


def _sample_pool_update(us_ref, pin_ref, pout_ref, pooled_ref):
    u = us_ref[...]
    pout_ref[0:POOL_BUF - 1] = pin_ref[1:POOL_BUF]
    pout_ref[POOL_BUF - 1] = u
    for g, w in enumerate(POOL_WINDOWS):
        cols = slice(g * POOL_GROUP, (g + 1) * POOL_GROUP)
        win_sum = u[:, cols]
        for r in range(POOL_BUF - (w - 1), POOL_BUF):
            win_sum = win_sum + pin_ref[r, :, cols]
        pooled_ref[:, cols] = win_sum / float(w) - u[:, cols]


def _main_kernel(x_ref, cos_ref, sin_ref, smeta_ref, umeta_ref, normw_ref, pscale_ref, rnw_ref,
                 fnw_ref, win_ref, wu_ref, wout_ref, qkv_ref, sstate_ref, us_ref, pin_ref,
                 rest_ref, wpool_ref,
                 y_ref, s_ref, pbuf_ref, sout_ref, pout_ref, ys_ref,
                 hist_ref, so_ref, pooled_ref, o_ref):
    c = pl.program_id(1)
    nb = x_ref.shape[0]
    batches = range(nb)

    @pl.when(c == 0)
    def _():
        for b in batches:
            s_ref[b] = smeta_ref[...]
            hist_ref[b] = umeta_ref[...]

    x = x_ref[...].reshape(nb * CHUNK, D_MODEL)
    hb = _rms(x, normw_ref[...]).astype(BF16)

    def proj(off):
        return _dot(hb, _unpack_rows(win_ref[:, off:off + TILE_N]))

    def rows(t, b):
        return t[b * CHUNK:(b + 1) * CHUNK]

    cosf, sinf = cos_ref[...], sin_ref[...]
    li = lax.broadcasted_iota(jnp.int32, (CHUNK, CHUNK), 0).astype(F32)
    mi = lax.broadcasted_iota(jnp.int32, (CHUNK, CHUNK), 1).astype(F32)
    diff = li - mi

    qkv, part, outs = {}, {}, {}

    def issue_qkv(pair):
        qkv[pair] = tuple(proj(off + pair * TILE_N) for off in (OFF_Q, OFF_K, OFF_V))

    def stage_a(pair):
        q2, k2, v2 = qkv.pop(pair)
        for i in range(2):
            h = 2 * pair + i
            lg = LOG_DECAY[h]
            sub = slice(i * HEAD_DIM, (i + 1) * HEAD_DIM)
            dmask = jnp.where(diff >= 0.0, jnp.exp(lg * jnp.maximum(diff, 0.0)), 0.0)
            q_decay = jnp.exp(lg * (li + 1.0))
            k_decay = jnp.exp(lg * (CHUNK - 1.0 - li))
            for b in batches:
                qr = _rotary(rows(q2, b)[:, sub], cosf, sinf)
                kr = _rotary(rows(k2, b)[:, sub], cosf, sinf) * K_SCALE
                vb = rows(v2, b)[:, sub].astype(BF16)
                state = s_ref[b, h]
                scores = lax.dot_general(qr.astype(BF16), kr.astype(BF16), (((1,), (1,)), ((), ())),
                                         preferred_element_type=F32)
                cross = _dot((qr * q_decay).astype(BF16), state.astype(BF16))
                s_ref[b, h] = math.exp(lg * CHUNK) * state + _dot_t((kr * k_decay).astype(BF16), vb)
                part[b, h] = ((scores * dmask).astype(BF16), vb, cross)

    def stage_b(pair):
        for i in range(2):
            h = 2 * pair + i
            for b in batches:
                p, vb, cross = part.pop((b, h))
                outs[b, h] = _dot(p, vb) + cross

    def pool_windows(u_tiles):
        pooled = []
        for g, w in enumerate(POOL_WINDOWS):
            cols = slice(g * POOL_GROUP, (g + 1) * POOL_GROUP)
            per_batch = []
            for b in batches:
                u = rows(u_tiles[g], b)
                ext = jnp.concatenate([hist_ref[b, :, cols], u], axis=0)
                win_sum = ext
                shift = 1
                while shift < w:
                    win_sum = win_sum + pltpu.roll(win_sum, shift, 0)
                    shift *= 2
                per_batch.append(win_sum[HIST:] / float(w) - u)
                hist_ref[b, :, cols] = u[CHUNK - HIST:]
            pooled.append(jnp.concatenate(per_batch, axis=0))
        return pooled

    step = pl.program_id(0) * pl.num_programs(1) + c
    so_step = so_ref.at[step]
    _sample_state_update(qkv_ref, sstate_ref, sout_ref, so_step, range(0, SAMPLE_HEADS_FIRST))
    block_rows = pl.ds(pl.multiple_of(step * sstate_ref.shape[0] // SUBLANES * SUBLANES, SUBLANES),
                       SUBLANES)
    _sample_pool_update(us_ref, pin_ref, pout_ref, pooled_ref.at[block_rows])
    issue_qkv(0)
    issue_qkv(1)
    stage_a(0)
    issue_qkv(2)
    stage_a(1)
    stage_b(0)
    issue_qkv(3)
    u_tiles = [proj(OFF_U + g * POOL_GROUP) for g in range(len(POOL_WINDOWS))]
    stage_a(2)
    stage_b(1)
    gp_tiles = [proj(OFF_GP + g * POOL_GROUP) for g in range(len(POOL_WINDOWS))]
    mixed = pool_windows(u_tiles)
    stage_a(3)
    stage_b(2)
    gr_tiles = [proj(OFF_GR + pair * TILE_N) for pair in range(N_HEADS // 2)]
    stage_b(3)
    pool_mix = jnp.concatenate(
        [(mixed[g] * pscale_ref[:, g * POOL_GROUP:(g + 1) * POOL_GROUP] * _silu(gp_tiles[g])).astype(BF16)
         for g in range(len(POOL_WINDOWS))], axis=-1)
    acc = x + _dot(pool_mix, _unpack_rows(wout_ref[0:D_POOL // 2, :]))

    per_piece = max(nb // OUT_PIECES, 1)
    for first in range(0, nb, per_piece):
        piece = range(first, first + per_piece)
        ret_y = []
        for b in piece:
            heads = []
            for h in range(N_HEADS):
                sub = slice((h % 2) * HEAD_DIM, (h % 2 + 1) * HEAD_DIM)
                o = outs.pop((b, h))
                rn = o * lax.rsqrt(jnp.mean(o * o, axis=-1, keepdims=True) + EPS)
                gate = _silu(rows(gr_tiles[h // 2], b)[:, sub])
                heads.append((rn * rnw_ref[:, _head(h)] * gate).astype(BF16))
            ret_y.append(jnp.concatenate(heads, axis=-1))
        ret_mix = jnp.concatenate(ret_y, axis=0)
        acc_piece = acc[first * CHUNK:(first + per_piece) * CHUNK]
        acc_piece = acc_piece + _dot(ret_mix, _unpack_rows(wout_ref[D_POOL // 2:D_MIX // 2, :]))
        y_ref[first:first + per_piece] = _rms(acc_piece, fnw_ref[...]).reshape(
            per_piece, CHUNK, D_MODEL)

    _sample_state_update(qkv_ref, sstate_ref, sout_ref, so_step, range(SAMPLE_HEADS_FIRST, N_HEADS))

    @pl.when(step == pl.num_programs(0) * pl.num_programs(1) - 1)
    def _():
        _sample_tail(rest_ref, so_ref, pooled_ref, pscale_ref, rnw_ref, fnw_ref, wpool_ref,
                     wout_ref, ys_ref, o_ref)

    @pl.when(c == pl.num_programs(1) - 1)
    def _():
        last = jnp.concatenate([rows(hb, b)[CHUNK - HIST:] for b in batches], axis=0)
        u_last = _dot(last, _unpack_rows(wu_ref[...]))
        for b in batches:
            pbuf_ref[b] = u_last[b * HIST + HIST - POOL_BUF:(b + 1) * HIST]


def _main_call(x, cosf, sinf, smeta, umeta, normw, pscale, rnw, fnw, win_bf, wu_bf, wout_bf,
               qkv_s, state_s, rest_s, pool_t, wpool_bf):
    batch, seq, _ = x.shape
    nb = PROMPT_BLOCK
    n_chunks = seq // CHUNK
    n_sample = state_s.shape[0]
    per = n_sample // ((batch // nb) * n_chunks)

    def const(shape):
        zeros = (0,) * len(shape)
        return pl.BlockSpec(shape, lambda b, c: zeros, pipeline_mode=pl.Buffered(1))

    step = lambda b, c: b * n_chunks + c
    pool_block = lambda b, c: step(b, c) * per // SUBLANES
    return pl.pallas_call(
        _main_kernel,
        grid=(batch // nb, n_chunks),
        in_specs=[
            pl.BlockSpec((nb, CHUNK, D_MODEL), lambda b, c: (b, c, 0)),
            pl.BlockSpec((CHUNK, HEAD_DIM), lambda b, c: (c, 0)),
            pl.BlockSpec((CHUNK, HEAD_DIM), lambda b, c: (c, 0)),
            const((N_HEADS, HEAD_DIM, HEAD_DIM)),
            const((N_META, D_POOL)),
            const((1, D_MODEL)),
            const((1, D_POOL)),
            const((1, D_RET)),
            const((1, D_MODEL)),
            const((D_MODEL // 2, D_IN_PROJ)),
            const((D_MODEL // 2, D_POOL)),
            const((D_MIX // 2, D_MODEL)),
            pl.BlockSpec((None, SUBLANES, 3 * D_RET), lambda b, c: (step(b, c), 0, 0)),
            pl.BlockSpec((per, N_HEADS, HEAD_DIM, HEAD_DIM), lambda b, c: (step(b, c), 0, 0, 0)),
            pl.BlockSpec((SUBLANES, D_POOL), lambda b, c: (pool_block(b, c), 0)),
            pl.BlockSpec((POOL_BUF, SUBLANES, D_POOL), lambda b, c: (0, pool_block(b, c), 0)),
            const((n_sample, D_REST)),
            const((len(POOL_WINDOWS), POOL_GROUP // 2, POOL_GROUP)),
        ],
        out_specs=[
            pl.BlockSpec((nb, CHUNK, D_MODEL), lambda b, c: (b, c, 0)),
            pl.BlockSpec((nb, N_HEADS, HEAD_DIM, HEAD_DIM), lambda b, c: (b, 0, 0, 0)),
            pl.BlockSpec((nb, POOL_BUF, D_POOL), lambda b, c: (b, 0, 0)),
            pl.BlockSpec((per, N_HEADS, HEAD_DIM, HEAD_DIM), lambda b, c: (step(b, c), 0, 0, 0)),
            pl.BlockSpec((POOL_BUF, SUBLANES, D_POOL), lambda b, c: (0, pool_block(b, c), 0)),
            pl.BlockSpec((n_sample, D_MODEL), lambda b, c: (0, 0)),
        ],
        out_shape=(jax.ShapeDtypeStruct((batch, seq, D_MODEL), F32),
                   jax.ShapeDtypeStruct((batch, N_HEADS, HEAD_DIM, HEAD_DIM), F32),
                   jax.ShapeDtypeStruct((batch, POOL_BUF, D_POOL), F32),
                   jax.ShapeDtypeStruct((n_sample, N_HEADS, HEAD_DIM, HEAD_DIM), F32),
                   jax.ShapeDtypeStruct((POOL_BUF, n_sample, D_POOL), F32),
                   jax.ShapeDtypeStruct((n_sample, D_MODEL), F32)),
        scratch_shapes=[pltpu.VMEM((nb, HIST, D_POOL), F32),
                        pltpu.VMEM((qkv_s.shape[0], SUBLANES, D_RET), F32),
                        pltpu.VMEM((n_sample, D_POOL), F32),
                        pltpu.VMEM((n_sample, D_RET), F32)],
        compiler_params=pltpu.CompilerParams(
            dimension_semantics=("arbitrary", "arbitrary"), vmem_limit_bytes=VMEM_LIMIT_BYTES),
        name="main",
    )(x, cosf, sinf, smeta, umeta, normw, pscale, rnw, fnw, win_bf, wu_bf, wout_bf,
      qkv_s, state_s, rest_s, pool_t, rest_s, wpool_bf)


def _sample_tail(rest_ref, so_ref, pooled_ref, pscale_ref, rnw_ref, fnw_ref, wpool_ref,
                 wout_ref, y_ref, o_ref):
    n_steps = so_ref.shape[0]
    per = rest_ref.shape[0] // n_steps
    for i in range(n_steps):
        o_ref[i * per:(i + 1) * per, :] = so_ref[i, 0:per, :]

    parts = []
    for g in range(len(POOL_WINDOWS)):
        cols = slice(g * POOL_GROUP, (g + 1) * POOL_GROUP)
        mixed = _dot(pooled_ref[:, cols].astype(BF16), _unpack_rows(wpool_ref[g])) * pscale_ref[:, cols]
        gate = _silu(rest_ref[:, D_POOL + g * POOL_GROUP:D_POOL + (g + 1) * POOL_GROUP])
        parts.append((mixed * gate).astype(BF16))
    for h in range(N_HEADS):
        o = o_ref[:, _head(h)] + rest_ref[:, 2 * D_POOL + D_RET + h * HEAD_DIM:
                                          2 * D_POOL + D_RET + (h + 1) * HEAD_DIM]
        rn = o * lax.rsqrt(jnp.mean(o * o, axis=-1, keepdims=True) + EPS)
        gr = rest_ref[:, 2 * D_POOL + h * HEAD_DIM:2 * D_POOL + (h + 1) * HEAD_DIM]
        parts.append((rn * rnw_ref[:, _head(h)] * _silu(gr)).astype(BF16))
    mix = jnp.concatenate(parts, axis=-1)
    x = rest_ref[:, OFF_REST_X:OFF_REST_X + D_MODEL]
    y_ref[...] = _rms(x + _dot(mix, _unpack_rows(wout_ref[...])), fnw_ref[...])


def _rotary_tables(pos):
    half = HEAD_DIM // 2
    inv = ROPE_BASE ** (-np.arange(half, dtype=np.float64) / half)
    ang = np.asarray(pos, np.float64)[:, None] * inv[None, :]
    cos, sin = np.cos(ang), np.sin(ang)
    return (jnp.asarray(np.concatenate([cos, cos], axis=-1), F32),
            jnp.asarray(np.concatenate([-sin, sin], axis=-1), F32))


def kernel(x_prompt, x_sample, state_ret, state_pool, meta_tokens, norm_w, w_in, w_pool,
           pool_scale, ret_norm_w, w_out, final_norm_w):
    assert norm_w.shape[0] == 1, "single-layer stack"
    batch, seq, _ = x_prompt.shape
    n_sample = x_sample.shape[0]
    n_steps = (batch // PROMPT_BLOCK) * (seq // CHUNK)
    per = n_sample // n_steps
    assert per * n_steps == n_sample and SUBLANES % per == 0
    normw, pscale, rnw = norm_w, pool_scale, ret_norm_w
    fnw = final_norm_w[None, :]

    cos_p, sin_p = _rotary_tables(np.arange(N_META + seq))
    cos_s, sin_s = _rotary_tables(PAST_LEN + np.arange(1))

    win_bf, wu_bf, wout_bf, wpool_bf, smeta, umeta, qkv_s, rest_s = _prep_call(
        meta_tokens.astype(x_prompt.dtype), x_sample, cos_p[:N_META], sin_p[:N_META], cos_s, sin_s,
        normw, w_in[0], w_out[0], w_pool[0], n_steps)
    y_p, s_p, buf_p, s_s, buf_s, y_s = _main_call(
        x_prompt, cos_p[N_META:], sin_p[N_META:], smeta, umeta, normw, pscale, rnw, fnw,
        win_bf, wu_bf, wout_bf, qkv_s, state_ret[0], rest_s, jnp.transpose(state_pool[0], (1, 0, 2)),
        wpool_bf)
    return (y_p, y_s[:, None, :], s_p[None], s_s[None], buf_p[None],
            jnp.transpose(buf_s, (1, 0, 2))[None])
```

```python
import math

import jax
import jax.numpy as jnp
import numpy as np
from jax import lax
from jax.experimental import pallas as pl
from jax.experimental.pallas import tpu as pltpu

D_MODEL = 1024
D_POOL = 1024
D_RET = 1024
D_MIX = D_POOL + D_RET
POOL_WINDOWS = (2, 4, 8, 16)
POOL_GROUP = D_POOL // len(POOL_WINDOWS)
POOL_BUF = max(POOL_WINDOWS) - 1
N_HEADS = 8
HEAD_DIM = D_RET // N_HEADS
D_IN_PROJ = 2 * D_POOL + 4 * D_RET
N_META = 16
PAST_LEN = 16384
CHUNK = 128
ROPE_BASE = 10000.0
EPS = 1e-6
K_SCALE = HEAD_DIM ** -0.5

OFF_U, OFF_GP, OFF_Q, OFF_K, OFF_V, OFF_GR = (i * 1024 for i in range(6))

OFF_REST_X = 2 * D_POOL + 2 * D_RET
D_REST = OFF_REST_X + D_MODEL

LOG_DECAY = tuple(math.log(1.0 - 2.0 ** (-5.0 - h)) for h in range(N_HEADS))

SUBLANES = 8
TILE_N = 256
PROMPT_BLOCK = 4
OUT_PIECES = 2
SAMPLE_HEADS_FIRST = 4
PREP_TILES = 8
HIST = 16

VMEM_LIMIT_BYTES = 60 * 1024 * 1024

F32 = jnp.float32
BF16 = jnp.bfloat16
PACKED = jnp.uint32


def _rms(x, w):
    return x * lax.rsqrt(jnp.mean(x * x, axis=-1, keepdims=True) + EPS) * w


def _silu(x):
    return x * (1.0 / (1.0 + jnp.exp(-x)))


def _dot(a, b):
    return jnp.dot(a, b, preferred_element_type=F32)


def _dot_t(a, b):
    return lax.dot_general(a, b, (((0,), (0,)), ((), ())), preferred_element_type=F32)


def _pack_rows(w):
    return pltpu.bitcast(w, PACKED)


def _unpack_rows(w):
    return pltpu.bitcast(w, BF16)


def _rotary(x, cosf, sinf):
    return x * cosf + pltpu.roll(x, HEAD_DIM // 2, 1) * sinf


def _head(h):
    return slice(h * HEAD_DIM, (h + 1) * HEAD_DIM)


def _prep_kernel(meta_ref, xs_ref, cos_ref, sin_ref, coss_ref, sins_ref, normw_ref,
                 win_ref, wout_ref, wpool_ref,
                 winb_ref, wub_ref, woutb_ref, wpoolb_ref, s_ref, u_ref, qkv_ref, rest_ref,
                 hb_ref, hs_ref, proj_ref, projs_ref, kd_ref, v_ref, stage_ref, wphi_ref, wplo_ref):
    j = pl.program_id(0)
    k_rows = D_MODEL // PREP_TILES

    @pl.when(j == 0)
    def _():
        hb = _rms(meta_ref[...], normw_ref[...]).astype(BF16)
        xs = xs_ref[:, 0, :]
        rest_ref[:, OFF_REST_X:OFF_REST_X + D_MODEL] = xs
        hs = _rms(xs, normw_ref[...]).astype(BF16)
        for t in range(PREP_TILES):
            hb_ref[t] = hb[:, t * k_rows:(t + 1) * k_rows]
            hs_ref[t] = hs[:, t * k_rows:(t + 1) * k_rows]
        wp = wpool_ref[...]
        wp_hi = wp.astype(BF16)
        wpoolb_ref[...] = _pack_rows(wp_hi)
        wphi_ref[...] = wp_hi
        wplo_ref[...] = (wp - wp_hi.astype(F32)).astype(BF16)
        proj_ref[...] = jnp.zeros_like(proj_ref)
        projs_ref[...] = jnp.zeros_like(projs_ref)

    tile = win_ref[...].astype(BF16)
    folded = []
    for g in range(len(POOL_WINDOWS)):
        cols = slice(OFF_U + g * POOL_GROUP, OFF_U + (g + 1) * POOL_GROUP)
        a_hi = tile[:, cols]
        a_lo = (win_ref[:, cols] - a_hi.astype(F32)).astype(BF16)
        folded.append((_dot(a_hi, wphi_ref[g]) + _dot(a_hi, wplo_ref[g])
                       + _dot(a_lo, wphi_ref[g])).astype(BF16))
    tile_main = jnp.concatenate(folded + [tile[:, OFF_U + D_POOL:]], axis=-1)
    winb_ref[...] = _pack_rows(tile_main)
    wub_ref[...] = _pack_rows(tile[:, OFF_U:OFF_U + D_POOL])
    woutb_ref[...] = _pack_rows(wout_ref[...].astype(BF16))
    proj_ref[...] += _dot(hb_ref[j], tile_main)
    projs_ref[...] += _dot(hs_ref[j], tile)

    @pl.when(j == pl.num_programs(0) - 1)
    def _():
        cosf, sinf = cos_ref[...], sin_ref[...]
        row = lax.broadcasted_iota(jnp.int32, (N_META, HEAD_DIM), 0).astype(F32)
        kd_ref[...] = jnp.zeros_like(kd_ref)
        v_ref[...] = jnp.zeros_like(v_ref)
        u_ref[...] = proj_ref[:, OFF_U:OFF_U + D_POOL]
        for h in range(N_HEADS):
            kr = _rotary(proj_ref[:, OFF_K + h * HEAD_DIM:OFF_K + (h + 1) * HEAD_DIM], cosf, sinf)
            kd_ref[0:N_META, _head(h)] = kr * K_SCALE * jnp.exp(LOG_DECAY[h] * (N_META - 1.0 - row))
            v_ref[0:N_META, _head(h)] = proj_ref[:, OFF_V + h * HEAD_DIM:OFF_V + (h + 1) * HEAD_DIM]
        for h in range(N_HEADS):
            s_ref[h] = _dot_t(kd_ref[:, _head(h)].astype(BF16), v_ref[:, _head(h)].astype(BF16))

        coss, sins = coss_ref[...], sins_ref[...]
        rest_ref[:, 0:D_POOL] = projs_ref[:, OFF_U:OFF_U + D_POOL]
        rest_ref[:, D_POOL:2 * D_POOL] = projs_ref[:, OFF_GP:OFF_GP + D_POOL]
        rest_ref[:, 2 * D_POOL:2 * D_POOL + D_RET] = projs_ref[:, OFF_GR:OFF_GR + D_RET]
        for h in range(N_HEADS):
            qr = _rotary(projs_ref[:, OFF_Q + h * HEAD_DIM:OFF_Q + (h + 1) * HEAD_DIM], coss, sins)
            kr = _rotary(projs_ref[:, OFF_K + h * HEAD_DIM:OFF_K + (h + 1) * HEAD_DIM], coss, sins)
            kr = kr * K_SCALE
            v = projs_ref[:, OFF_V + h * HEAD_DIM:OFF_V + (h + 1) * HEAD_DIM]
            stage_ref[:, _head(h)] = qr * math.exp(LOG_DECAY[h])
            stage_ref[:, D_RET + h * HEAD_DIM:D_RET + (h + 1) * HEAD_DIM] = kr
            stage_ref[:, 2 * D_RET + h * HEAD_DIM:2 * D_RET + (h + 1) * HEAD_DIM] = v
            rest_ref[:, 2 * D_POOL + D_RET + h * HEAD_DIM:2 * D_POOL + D_RET + (h + 1) * HEAD_DIM] = (
                jnp.sum(qr * kr, axis=-1, keepdims=True) * v)
        n_steps, per = qkv_ref.shape[0], stage_ref.shape[0] // qkv_ref.shape[0]
        qkv_ref[...] = jnp.zeros_like(qkv_ref)
        for i in range(n_steps):
            qkv_ref[i, 0:per, :] = stage_ref[i * per:(i + 1) * per, :]


def _prep_call(meta, xs, cosf, sinf, coss, sins, normw, w_in, w_out, w_pool, n_steps):
    n = xs.shape[0]

    def const(shape):
        zeros = (0,) * len(shape)
        return pl.BlockSpec(shape, lambda j: zeros)

    row_tile = lambda j: (j, 0)
    return pl.pallas_call(
        _prep_kernel,
        grid=(PREP_TILES,),
        in_specs=[
            const((N_META, D_MODEL)),
            const((n, 1, D_MODEL)),
            const((N_META, HEAD_DIM)),
            const((N_META, HEAD_DIM)),
            const((1, HEAD_DIM)),
            const((1, HEAD_DIM)),
            const((1, D_MODEL)),
            pl.BlockSpec((D_MODEL // PREP_TILES, D_IN_PROJ), row_tile),
            pl.BlockSpec((D_MIX // PREP_TILES, D_MODEL), row_tile),
            const((len(POOL_WINDOWS), POOL_GROUP, POOL_GROUP)),
        ],
        out_specs=[
            pl.BlockSpec((D_MODEL // PREP_TILES // 2, D_IN_PROJ), row_tile),
            pl.BlockSpec((D_MODEL // PREP_TILES // 2, D_POOL), row_tile),
            pl.BlockSpec((D_MIX // PREP_TILES // 2, D_MODEL), row_tile),
            const((len(POOL_WINDOWS), POOL_GROUP // 2, POOL_GROUP)),
            const((N_HEADS, HEAD_DIM, HEAD_DIM)),
            const((N_META, D_POOL)),
            const((n_steps, SUBLANES, 3 * D_RET)),
            const((n, D_REST)),
        ],
        out_shape=(jax.ShapeDtypeStruct((D_MODEL // 2, D_IN_PROJ), PACKED),
                   jax.ShapeDtypeStruct((D_MODEL // 2, D_POOL), PACKED),
                   jax.ShapeDtypeStruct((D_MIX // 2, D_MODEL), PACKED),
                   jax.ShapeDtypeStruct((len(POOL_WINDOWS), POOL_GROUP // 2, POOL_GROUP), PACKED),
                   jax.ShapeDtypeStruct((N_HEADS, HEAD_DIM, HEAD_DIM), F32),
                   jax.ShapeDtypeStruct((N_META, D_POOL), F32),
                   jax.ShapeDtypeStruct((n_steps, SUBLANES, 3 * D_RET), F32),
                   jax.ShapeDtypeStruct((n, D_REST), F32)),
        scratch_shapes=[pltpu.VMEM((PREP_TILES, N_META, D_MODEL // PREP_TILES), BF16),
                        pltpu.VMEM((PREP_TILES, n, D_MODEL // PREP_TILES), BF16),
                        pltpu.VMEM((N_META, D_IN_PROJ), F32),
                        pltpu.VMEM((n, D_IN_PROJ), F32),
                        pltpu.VMEM((CHUNK, D_RET), F32),
                        pltpu.VMEM((CHUNK, D_RET), F32),
                        pltpu.VMEM((n, 3 * D_RET), F32),
                        pltpu.VMEM((len(POOL_WINDOWS), POOL_GROUP, POOL_GROUP), BF16),
                        pltpu.VMEM((len(POOL_WINDOWS), POOL_GROUP, POOL_GROUP), BF16)],
        compiler_params=pltpu.CompilerParams(
            dimension_semantics=("arbitrary",), vmem_limit_bytes=VMEM_LIMIT_BYTES),
        name="prep",
    )(meta, xs, cosf, sinf, coss, sins, normw, w_in, w_out, w_pool)


def _sample_state_update(qkv_ref, state_ref, sout_ref, o_ref, heads):
    per = state_ref.shape[0]
    seq_of_row = lax.broadcasted_iota(jnp.int32, (SUBLANES, per * HEAD_DIM), 0)
    seq_of_col = lax.broadcasted_iota(jnp.int32, (SUBLANES, per * HEAD_DIM), 1) // HEAD_DIM
    own_block = seq_of_row == seq_of_col
    for h in heads:
        g1 = math.exp(LOG_DECAY[h])
        qd = qkv_ref[:, _head(h)]
        kt = qkv_ref[:, D_RET + h * HEAD_DIM:D_RET + (h + 1) * HEAD_DIM].T
        v8 = qkv_ref[:, 2 * D_RET + h * HEAD_DIM:2 * D_RET + (h + 1) * HEAD_DIM]
        state = state_ref[:, h]
        q_blocks = jnp.where(own_block, jnp.concatenate([qd] * per, axis=-1), 0.0)
        o_ref[:, _head(h)] = _dot(q_blocks.astype(BF16),
                                  state.reshape(per * HEAD_DIM, HEAD_DIM).astype(BF16))
        for j in range(per):
            sout_ref[j, h] = g1 * state[j] + kt[:, j:j + 1] * v8[j:j + 1, :]


def _sample_pool_update(us_ref, pin_ref, pout_ref, pooled_ref):
    u = us_ref[...]
    pout_ref[0:POOL_BUF - 1] = pin_ref[1:POOL_BUF]
    pout_ref[POOL_BUF - 1] = u
    for g, w in enumerate(POOL_WINDOWS):
        cols = slice(g * POOL_GROUP, (g + 1) * POOL_GROUP)
        win_sum = u[:, cols]
        for r in range(POOL_BUF - (w - 1), POOL_BUF):
            win_sum = win_sum + pin_ref[r, :, cols]
        pooled_ref[:, cols] = win_sum / float(w) - u[:, cols]


def _main_kernel(x_ref, cos_ref, sin_ref, smeta_ref, umeta_ref, normw_ref, pscale_ref, rnw_ref,
                 fnw_ref, win_ref, wu_ref, wout_ref, qkv_ref, sstate_ref, us_ref, pin_ref,
                 rest_ref, wpool_ref,
                 y_ref, s_ref, pbuf_ref, sout_ref, pout_ref, ys_ref,
                 hist_ref, so_ref, pooled_ref, o_ref):
    c = pl.program_id(1)
    nb = x_ref.shape[0]
    batches = range(nb)

    @pl.when(c == 0)
    def _():
        for b in batches:
            s_ref[b] = smeta_ref[...]
            hist_ref[b] = umeta_ref[...]

    x = x_ref[...].reshape(nb * CHUNK, D_MODEL)
    hb = _rms(x, normw_ref[...]).astype(BF16)

    def proj(off):
        return _dot(hb, _unpack_rows(win_ref[:, off:off + TILE_N]))

    def rows(t, b):
        return t[b * CHUNK:(b + 1) * CHUNK]

    cosf, sinf = cos_ref[...], sin_ref[...]
    li = lax.broadcasted_iota(jnp.int32, (CHUNK, CHUNK), 0).astype(F32)
    mi = lax.broadcasted_iota(jnp.int32, (CHUNK, CHUNK), 1).astype(F32)
    diff = li - mi

    qkv, part, outs = {}, {}, {}

    def issue_qkv(pair):
        qkv[pair] = tuple(proj(off + pair * TILE_N) for off in (OFF_Q, OFF_K, OFF_V))

    def stage_a(pair):
        q2, k2, v2 = qkv.pop(pair)
        for i in range(2):
            h = 2 * pair + i
            lg = LOG_DECAY[h]
            sub = slice(i * HEAD_DIM, (i + 1) * HEAD_DIM)
            dmask = jnp.where(diff >= 0.0, jnp.exp(lg * jnp.maximum(diff, 0.0)), 0.0)
            q_decay = jnp.exp(lg * (li + 1.0))
            k_decay = jnp.exp(lg * (CHUNK - 1.0 - li))
            for b in batches:
                qr = _rotary(rows(q2, b)[:, sub], cosf, sinf)
                kr = _rotary(rows(k2, b)[:, sub], cosf, sinf) * K_SCALE
                vb = rows(v2, b)[:, sub].astype(BF16)
                state = s_ref[b, h]
                scores = lax.dot_general(qr.astype(BF16), kr.astype(BF16), (((1,), (1,)), ((), ())),
                                         preferred_element_type=F32)
                cross = _dot((qr * q_decay).astype(BF16), state.astype(BF16))
                s_ref[b, h] = math.exp(lg * CHUNK) * state + _dot_t((kr * k_decay).astype(BF16), vb)
                part[b, h] = ((scores * dmask).astype(BF16), vb, cross)

    def stage_b(pair):
        for i in range(2):
            h = 2 * pair + i
            for b in batches:
                p, vb, cross = part.pop((b, h))
                outs[b, h] = _dot(p, vb) + cross

    def pool_windows(u_tiles):
        pooled = []
        for g, w in enumerate(POOL_WINDOWS):
            cols = slice(g * POOL_GROUP, (g + 1) * POOL_GROUP)
            per_batch = []
            for b in batches:
                u = rows(u_tiles[g], b)
                ext = jnp.concatenate([hist_ref[b, :, cols], u], axis=0)
                win_sum = ext
                shift = 1
                while shift < w:
                    win_sum = win_sum + pltpu.roll(win_sum, shift, 0)
                    shift *= 2
                per_batch.append(win_sum[HIST:] / float(w) - u)
                hist_ref[b, :, cols] = u[CHUNK - HIST:]
            pooled.append(jnp.concatenate(per_batch, axis=0))
        return pooled

    step = pl.program_id(0) * pl.num_programs(1) + c
    so_step = so_ref.at[step]
    _sample_state_update(qkv_ref, sstate_ref, sout_ref, so_step, range(0, SAMPLE_HEADS_FIRST))
    block_rows = pl.ds(pl.multiple_of(step * sstate_ref.shape[0] // SUBLANES * SUBLANES, SUBLANES),
                       SUBLANES)
    _sample_pool_update(us_ref, pin_ref, pout_ref, pooled_ref.at[block_rows])
    issue_qkv(0)
    issue_qkv(1)
    stage_a(0)
    issue_qkv(2)
    stage_a(1)
    stage_b(0)
    issue_qkv(3)
    u_tiles = [proj(OFF_U + g * POOL_GROUP) for g in range(len(POOL_WINDOWS))]
    stage_a(2)
    stage_b(1)
    gp_tiles = [proj(OFF_GP + g * POOL_GROUP) for g in range(len(POOL_WINDOWS))]
    mixed = pool_windows(u_tiles)
    stage_a(3)
    stage_b(2)
    gr_tiles = [proj(OFF_GR + pair * TILE_N) for pair in range(N_HEADS // 2)]
    stage_b(3)
    pool_mix = jnp.concatenate(
        [(mixed[g] * pscale_ref[:, g * POOL_GROUP:(g + 1) * POOL_GROUP] * _silu(gp_tiles[g])).astype(BF16)
         for g in range(len(POOL_WINDOWS))], axis=-1)
    acc = x + _dot(pool_mix, _unpack_rows(wout_ref[0:D_POOL // 2, :]))

    per_piece = max(nb // OUT_PIECES, 1)
    for first in range(0, nb, per_piece):
        piece = range(first, first + per_piece)
        ret_y = []
        for b in piece:
            heads = []
            for h in range(N_HEADS):
                sub = slice((h % 2) * HEAD_DIM, (h % 2 + 1) * HEAD_DIM)
                o = outs.pop((b, h))
                rn = o * lax.rsqrt(jnp.mean(o * o, axis=-1, keepdims=True) + EPS)
                gate = _silu(rows(gr_tiles[h // 2], b)[:, sub])
                heads.append((rn * rnw_ref[:, _head(h)] * gate).astype(BF16))
            ret_y.append(jnp.concatenate(heads, axis=-1))
        ret_mix = jnp.concatenate(ret_y, axis=0)
        acc_piece = acc[first * CHUNK:(first + per_piece) * CHUNK]
        acc_piece = acc_piece + _dot(ret_mix, _unpack_rows(wout_ref[D_POOL // 2:D_MIX // 2, :]))
        y_ref[first:first + per_piece] = _rms(acc_piece, fnw_ref[...]).reshape(
            per_piece, CHUNK, D_MODEL)

    _sample_state_update(qkv_ref, sstate_ref, sout_ref, so_step, range(SAMPLE_HEADS_FIRST, N_HEADS))

    @pl.when(step == pl.num_programs(0) * pl.num_programs(1) - 1)
    def _():
        _sample_tail(rest_ref, so_ref, pooled_ref, pscale_ref, rnw_ref, fnw_ref, wpool_ref,
                     wout_ref, ys_ref, o_ref)

    @pl.when(c == pl.num_programs(1) - 1)
    def _():
        last = jnp.concatenate([rows(hb, b)[CHUNK - HIST:] for b in batches], axis=0)
        u_last = _dot(last, _unpack_rows(wu_ref[...]))
        for b in batches:
            pbuf_ref[b] = u_last[b * HIST + HIST - POOL_BUF:(b + 1) * HIST]


def _main_call(x, cosf, sinf, smeta, umeta, normw, pscale, rnw, fnw, win_bf, wu_bf, wout_bf,
               qkv_s, state_s, rest_s, pool_t, wpool_bf):
    batch, seq, _ = x.shape
    nb = PROMPT_BLOCK
    n_chunks = seq // CHUNK
    n_sample = state_s.shape[0]
    per = n_sample // ((batch // nb) * n_chunks)

    def const(shape):
        zeros = (0,) * len(shape)
        return pl.BlockSpec(shape, lambda b, c: zeros, pipeline_mode=pl.Buffered(1))

    step = lambda b, c: b * n_chunks + c
    pool_block = lambda b, c: step(b, c) * per // SUBLANES
    return pl.pallas_call(
        _main_kernel,
        grid=(batch // nb, n_chunks),
        in_specs=[
            pl.BlockSpec((nb, CHUNK, D_MODEL), lambda b, c: (b, c, 0)),
            pl.BlockSpec((CHUNK, HEAD_DIM), lambda b, c: (c, 0)),
            pl.BlockSpec((CHUNK, HEAD_DIM), lambda b, c: (c, 0)),
            const((N_HEADS, HEAD_DIM, HEAD_DIM)),
            const((N_META, D_POOL)),
            const((1, D_MODEL)),
            const((1, D_POOL)),
            const((1, D_RET)),
            const((1, D_MODEL)),
            const((D_MODEL // 2, D_IN_PROJ)),
            const((D_MODEL // 2, D_POOL)),
            const((D_MIX // 2, D_MODEL)),
            pl.BlockSpec((None, SUBLANES, 3 * D_RET), lambda b, c: (step(b, c), 0, 0)),
            pl.BlockSpec((per, N_HEADS, HEAD_DIM, HEAD_DIM), lambda b, c: (step(b, c), 0, 0, 0)),
            pl.BlockSpec((SUBLANES, D_POOL), lambda b, c: (pool_block(b, c), 0)),
            pl.BlockSpec((POOL_BUF, SUBLANES, D_POOL), lambda b, c: (0, pool_block(b, c), 0)),
            const((n_sample, D_REST)),
            const((len(POOL_WINDOWS), POOL_GROUP // 2, POOL_GROUP)),
        ],
        out_specs=[
            pl.BlockSpec((nb, CHUNK, D_MODEL), lambda b, c: (b, c, 0)),
            pl.BlockSpec((nb, N_HEADS, HEAD_DIM, HEAD_DIM), lambda b, c: (b, 0, 0, 0)),
            pl.BlockSpec((nb, POOL_BUF, D_POOL), lambda b, c: (b, 0, 0)),
            pl.BlockSpec((per, N_HEADS, HEAD_DIM, HEAD_DIM), lambda b, c: (step(b, c), 0, 0, 0)),
            pl.BlockSpec((POOL_BUF, SUBLANES, D_POOL), lambda b, c: (0, pool_block(b, c), 0)),
            pl.BlockSpec((n_sample, 1, D_MODEL), lambda b, c: (0, 0, 0)),
        ],
        out_shape=(jax.ShapeDtypeStruct((batch, seq, D_MODEL), F32),
                   jax.ShapeDtypeStruct((batch, N_HEADS, HEAD_DIM, HEAD_DIM), F32),
                   jax.ShapeDtypeStruct((batch, POOL_BUF, D_POOL), F32),
                   jax.ShapeDtypeStruct((n_sample, N_HEADS, HEAD_DIM, HEAD_DIM), F32),
                   jax.ShapeDtypeStruct((POOL_BUF, n_sample, D_POOL), F32),
                   jax.ShapeDtypeStruct((n_sample, 1, D_MODEL), F32)),
        scratch_shapes=[pltpu.VMEM((nb, HIST, D_POOL), F32),
                        pltpu.VMEM((qkv_s.shape[0], SUBLANES, D_RET), F32),
                        pltpu.VMEM((n_sample, D_POOL), F32),
                        pltpu.VMEM((n_sample, D_RET), F32)],
        compiler_params=pltpu.CompilerParams(
            dimension_semantics=("arbitrary", "arbitrary"), vmem_limit_bytes=VMEM_LIMIT_BYTES),
        name="main",
    )(x, cosf, sinf, smeta, umeta, normw, pscale, rnw, fnw, win_bf, wu_bf, wout_bf,
      qkv_s, state_s, rest_s, pool_t, rest_s, wpool_bf)


def _sample_tail(rest_ref, so_ref, pooled_ref, pscale_ref, rnw_ref, fnw_ref, wpool_ref,
                 wout_ref, y_ref, o_ref):
    n_steps = so_ref.shape[0]
    per = rest_ref.shape[0] // n_steps
    for i in range(n_steps):
        o_ref[i * per:(i + 1) * per, :] = so_ref[i, 0:per, :]

    parts = []
    for g in range(len(POOL_WINDOWS)):
        cols = slice(g * POOL_GROUP, (g + 1) * POOL_GROUP)
        mixed = _dot(pooled_ref[:, cols].astype(BF16), _unpack_rows(wpool_ref[g])) * pscale_ref[:, cols]
        gate = _silu(rest_ref[:, D_POOL + g * POOL_GROUP:D_POOL + (g + 1) * POOL_GROUP])
        parts.append((mixed * gate).astype(BF16))
    for h in range(N_HEADS):
        o = o_ref[:, _head(h)] + rest_ref[:, 2 * D_POOL + D_RET + h * HEAD_DIM:
                                          2 * D_POOL + D_RET + (h + 1) * HEAD_DIM]
        rn = o * lax.rsqrt(jnp.mean(o * o, axis=-1, keepdims=True) + EPS)
        gr = rest_ref[:, 2 * D_POOL + h * HEAD_DIM:2 * D_POOL + (h + 1) * HEAD_DIM]
        parts.append((rn * rnw_ref[:, _head(h)] * _silu(gr)).astype(BF16))
    mix = jnp.concatenate(parts, axis=-1)
    x = rest_ref[:, OFF_REST_X:OFF_REST_X + D_MODEL]
    y_ref[:, 0, :] = _rms(x + _dot(mix, _unpack_rows(wout_ref[...])), fnw_ref[...])


def _rotary_tables(pos):
    half = HEAD_DIM // 2
    inv = ROPE_BASE ** (-np.arange(half, dtype=np.float64) / half)
    ang = np.asarray(pos, np.float64)[:, None] * inv[None, :]
    cos, sin = np.cos(ang), np.sin(ang)
    return (jnp.asarray(np.concatenate([cos, cos], axis=-1), F32),
            jnp.asarray(np.concatenate([-sin, sin], axis=-1), F32))


def kernel(x_prompt, x_sample, state_ret, state_pool, meta_tokens, norm_w, w_in, w_pool,
           pool_scale, ret_norm_w, w_out, final_norm_w):
    assert norm_w.shape[0] == 1, "single-layer stack"
    batch, seq, _ = x_prompt.shape
    n_sample = x_sample.shape[0]
    n_steps = (batch // PROMPT_BLOCK) * (seq // CHUNK)
    per = n_sample // n_steps
    assert per * n_steps == n_sample and SUBLANES % per == 0
    normw, pscale, rnw = norm_w, pool_scale, ret_norm_w
    fnw = final_norm_w[None, :]

    cos_p, sin_p = _rotary_tables(np.arange(N_META + seq))
    cos_s, sin_s = _rotary_tables(PAST_LEN + np.arange(1))

    win_bf, wu_bf, wout_bf, wpool_bf, smeta, umeta, qkv_s, rest_s = _prep_call(
        meta_tokens.astype(x_prompt.dtype), x_sample, cos_p[:N_META], sin_p[:N_META], cos_s, sin_s,
        normw, w_in[0], w_out[0], w_pool[0], n_steps)
    y_p, s_p, buf_p, s_s, buf_s, y_s = _main_call(
        x_prompt, cos_p[N_META:], sin_p[N_META:], smeta, umeta, normw, pscale, rnw, fnw,
        win_bf, wu_bf, wout_bf, qkv_s, state_ret[0], rest_s, jnp.transpose(state_pool[0], (1, 0, 2)),
        wpool_bf)
    return (y_p, y_s, s_p[None], s_s[None], buf_p[None],
            jnp.transpose(buf_s, (1, 0, 2))[None])
```

```python
import math

import jax
import jax.numpy as jnp
import numpy as np
from jax import lax
from jax.experimental import pallas as pl
from jax.experimental.pallas import tpu as pltpu

D_MODEL = 1024
D_POOL = 1024
D_RET = 1024
D_MIX = D_POOL + D_RET
POOL_WINDOWS = (2, 4, 8, 16)
POOL_GROUP = D_POOL // len(POOL_WINDOWS)
POOL_BUF = max(POOL_WINDOWS) - 1
N_HEADS = 8
HEAD_DIM = D_RET // N_HEADS
D_IN_PROJ = 2 * D_POOL + 4 * D_RET
N_META = 16
PAST_LEN = 16384
CHUNK = 128
ROPE_BASE = 10000.0
EPS = 1e-6
K_SCALE = HEAD_DIM ** -0.5

OFF_U, OFF_GP, OFF_Q, OFF_K, OFF_V, OFF_GR = (i * 1024 for i in range(6))

OFF_REST_X = 2 * D_POOL + 2 * D_RET
D_REST = OFF_REST_X + D_MODEL

LOG_DECAY = tuple(math.log(1.0 - 2.0 ** (-5.0 - h)) for h in range(N_HEADS))

SUBLANES = 8
TILE_N = 256
PROMPT_BLOCK = 4
OUT_PIECES = 2
SAMPLE_HEADS_FIRST = 4
PREP_TILES = 8
HIST = 16

VMEM_LIMIT_BYTES = 60 * 1024 * 1024

F32 = jnp.float32
BF16 = jnp.bfloat16
PACKED = jnp.uint32


def _rms(x, w):
    return x * lax.rsqrt(jnp.mean(x * x, axis=-1, keepdims=True) + EPS) * w


def _silu(x):
    return x * (1.0 / (1.0 + jnp.exp(-x)))


def _dot(a, b):
    return jnp.dot(a, b, preferred_element_type=F32)


def _dot_t(a, b):
    return lax.dot_general(a, b, (((0,), (0,)), ((), ())), preferred_element_type=F32)


def _pack_rows(w):
    return pltpu.bitcast(w, PACKED)


def _unpack_rows(w):
    return pltpu.bitcast(w, BF16)


def _rotary(x, cosf, sinf):
    return x * cosf + pltpu.roll(x, HEAD_DIM // 2, 1) * sinf


def _head(h):
    return slice(h * HEAD_DIM, (h + 1) * HEAD_DIM)


def _prep_kernel(meta_ref, xs_ref, cos_ref, sin_ref, coss_ref, sins_ref, normw_ref,
                 win_ref, wout_ref, wpool_ref,
                 winb_ref, wub_ref, woutb_ref, wpoolb_ref, s_ref, u_ref, qkv_ref, rest_ref,
                 hb_ref, hs_ref, proj_ref, projs_ref, kd_ref, v_ref, stage_ref, wphi_ref, wplo_ref):
    j = pl.program_id(0)
    k_rows = D_MODEL // PREP_TILES

    @pl.when(j == 0)
    def _():
        hb = _rms(meta_ref[...], normw_ref[...]).astype(BF16)
        xs = xs_ref[:, 0, :]
        rest_ref[:, OFF_REST_X:OFF_REST_X + D_MODEL] = xs
        hs = _rms(xs, normw_ref[...]).astype(BF16)
        for t in range(PREP_TILES):
            hb_ref[t] = hb[:, t * k_rows:(t + 1) * k_rows]
            hs_ref[t] = hs[:, t * k_rows:(t + 1) * k_rows]
        wp = wpool_ref[...]
        wp_hi = wp.astype(BF16)
        wpoolb_ref[...] = _pack_rows(wp_hi)
        wphi_ref[...] = wp_hi
        wplo_ref[...] = (wp - wp_hi.astype(F32)).astype(BF16)
        proj_ref[...] = jnp.zeros_like(proj_ref)
        projs_ref[...] = jnp.zeros_like(projs_ref)

    tile = win_ref[...].astype(BF16)
    folded = []
    for g in range(len(POOL_WINDOWS)):
        cols = slice(OFF_U + g * POOL_GROUP, OFF_U + (g + 1) * POOL_GROUP)
        a_hi = tile[:, cols]
        a_lo = (win_ref[:, cols] - a_hi.astype(F32)).astype(BF16)
        folded.append((_dot(a_hi, wphi_ref[g]) + _dot(a_hi, wplo_ref[g])
                       + _dot(a_lo, wphi_ref[g])).astype(BF16))
    tile_main = jnp.concatenate(folded + [tile[:, OFF_U + D_POOL:]], axis=-1)
    winb_ref[...] = _pack_rows(tile_main)
    wub_ref[...] = _pack_rows(tile[:, OFF_U:OFF_U + D_POOL])
    woutb_ref[...] = _pack_rows(wout_ref[...].astype(BF16))
    proj_ref[...] += _dot(hb_ref[j], tile_main)
    projs_ref[...] += _dot(hs_ref[j], tile)

    @pl.when(j == pl.num_programs(0) - 1)
    def _():
        cosf, sinf = cos_ref[...], sin_ref[...]
        row = lax.broadcasted_iota(jnp.int32, (N_META, HEAD_DIM), 0).astype(F32)
        kd_ref[...] = jnp.zeros_like(kd_ref)
        v_ref[...] = jnp.zeros_like(v_ref)
        u_ref[...] = proj_ref[:, OFF_U:OFF_U + D_POOL]
        for h in range(N_HEADS):
            kr = _rotary(proj_ref[:, OFF_K + h * HEAD_DIM:OFF_K + (h + 1) * HEAD_DIM], cosf, sinf)
            kd_ref[0:N_META, _head(h)] = kr * K_SCALE * jnp.exp(LOG_DECAY[h] * (N_META - 1.0 - row))
            v_ref[0:N_META, _head(h)] = proj_ref[:, OFF_V + h * HEAD_DIM:OFF_V + (h + 1) * HEAD_DIM]
        for h in range(N_HEADS):
            s_ref[h] = _dot_t(kd_ref[:, _head(h)].astype(BF16), v_ref[:, _head(h)].astype(BF16))

        coss, sins = coss_ref[...], sins_ref[...]
        rest_ref[:, 0:D_POOL] = projs_ref[:, OFF_U:OFF_U + D_POOL]
        rest_ref[:, D_POOL:2 * D_POOL] = projs_ref[:, OFF_GP:OFF_GP + D_POOL]
        rest_ref[:, 2 * D_POOL:2 * D_POOL + D_RET] = projs_ref[:, OFF_GR:OFF_GR + D_RET]
        for h in range(N_HEADS):
            qr = _rotary(projs_ref[:, OFF_Q + h * HEAD_DIM:OFF_Q + (h + 1) * HEAD_DIM], coss, sins)
            kr = _rotary(projs_ref[:, OFF_K + h * HEAD_DIM:OFF_K + (h + 1) * HEAD_DIM], coss, sins)
            kr = kr * K_SCALE
            v = projs_ref[:, OFF_V + h * HEAD_DIM:OFF_V + (h + 1) * HEAD_DIM]
            stage_ref[:, _head(h)] = qr * math.exp(LOG_DECAY[h])
            stage_ref[:, D_RET + h * HEAD_DIM:D_RET + (h + 1) * HEAD_DIM] = kr
            stage_ref[:, 2 * D_RET + h * HEAD_DIM:2 * D_RET + (h + 1) * HEAD_DIM] = v
            rest_ref[:, 2 * D_POOL + D_RET + h * HEAD_DIM:2 * D_POOL + D_RET + (h + 1) * HEAD_DIM] = (
                jnp.sum(qr * kr, axis=-1, keepdims=True) * v)
        n_steps, per = qkv_ref.shape[0], stage_ref.shape[0] // qkv_ref.shape[0]
        qkv_ref[...] = jnp.zeros_like(qkv_ref)
        for i in range(n_steps):
            qkv_ref[i, 0:per, :] = stage_ref[i * per:(i + 1) * per, :]


def _prep_call(meta, xs, cosf, sinf, coss, sins, normw, w_in, w_out, w_pool, n_steps):
    n = xs.shape[0]

    def const(shape):
        zeros = (0,) * len(shape)
        return pl.BlockSpec(shape, lambda j: zeros)

    row_tile = lambda j: (j, 0)
    return pl.pallas_call(
        _prep_kernel,
        grid=(PREP_TILES,),
        in_specs=[
            const((N_META, D_MODEL)),
            const((n, 1, D_MODEL)),
            const((N_META, HEAD_DIM)),
            const((N_META, HEAD_DIM)),
            const((1, HEAD_DIM)),
            const((1, HEAD_DIM)),
            const((1, D_MODEL)),
            pl.BlockSpec((D_MODEL // PREP_TILES, D_IN_PROJ), row_tile),
            pl.BlockSpec((D_MIX // PREP_TILES, D_MODEL), row_tile),
            const((len(POOL_WINDOWS), POOL_GROUP, POOL_GROUP)),
        ],
        out_specs=[
            pl.BlockSpec((D_MODEL // PREP_TILES // 2, D_IN_PROJ), row_tile),
            pl.BlockSpec((D_MODEL // PREP_TILES // 2, D_POOL), row_tile),
            pl.BlockSpec((D_MIX // PREP_TILES // 2, D_MODEL), row_tile),
            const((len(POOL_WINDOWS), POOL_GROUP // 2, POOL_GROUP)),
            const((N_HEADS, HEAD_DIM, HEAD_DIM)),
            const((N_META, D_POOL)),
            const((n_steps, SUBLANES, 3 * D_RET)),
            const((n, D_REST)),
        ],
        out_shape=(jax.ShapeDtypeStruct((D_MODEL // 2, D_IN_PROJ), PACKED),
                   jax.ShapeDtypeStruct((D_MODEL // 2, D_POOL), PACKED),
                   jax.ShapeDtypeStruct((D_MIX // 2, D_MODEL), PACKED),
                   jax.ShapeDtypeStruct((len(POOL_WINDOWS), POOL_GROUP // 2, POOL_GROUP), PACKED),
                   jax.ShapeDtypeStruct((N_HEADS, HEAD_DIM, HEAD_DIM), F32),
                   jax.ShapeDtypeStruct((N_META, D_POOL), F32),
                   jax.ShapeDtypeStruct((n_steps, SUBLANES, 3 * D_RET), F32),
                   jax.ShapeDtypeStruct((n, D_REST), F32)),
        scratch_shapes=[pltpu.VMEM((PREP_TILES, N_META, D_MODEL // PREP_TILES), BF16),
                        pltpu.VMEM((PREP_TILES, n, D_MODEL // PREP_TILES), BF16),
                        pltpu.VMEM((N_META, D_IN_PROJ), F32),
                        pltpu.VMEM((n, D_IN_PROJ), F32),
                        pltpu.VMEM((CHUNK, D_RET), F32),
                        pltpu.VMEM((CHUNK, D_RET), F32),
                        pltpu.VMEM((n, 3 * D_RET), F32),
                        pltpu.VMEM((len(POOL_WINDOWS), POOL_GROUP, POOL_GROUP), BF16),
                        pltpu.VMEM((len(POOL_WINDOWS), POOL_GROUP, POOL_GROUP), BF16)],
        compiler_params=pltpu.CompilerParams(
            dimension_semantics=("arbitrary",), vmem_limit_bytes=VMEM_LIMIT_BYTES),
        name="prep",
    )(meta, xs, cosf, sinf, coss, sins, normw, w_in, w_out, w_pool)


def _sample_state_update(qkv_ref, state_ref, sout_ref, o_ref, heads):
    per = state_ref.shape[0]
    seq_of_row = lax.broadcasted_iota(jnp.int32, (SUBLANES, per * HEAD_DIM), 0)
    seq_of_col = lax.broadcasted_iota(jnp.int32, (SUBLANES, per * HEAD_DIM), 1) // HEAD_DIM
    own_block = seq_of_row == seq_of_col
    for h in heads:
        g1 = math.exp(LOG_DECAY[h])
        qd = qkv_ref[:, _head(h)]
        kt = qkv_ref[:, D_RET + h * HEAD_DIM:D_RET + (h + 1) * HEAD_DIM].T
        v8 = qkv_ref[:, 2 * D_RET + h * HEAD_DIM:2 * D_RET + (h + 1) * HEAD_DIM]
        state = state_ref[:, h]
        q_blocks = jnp.where(own_block, jnp.concatenate([qd] * per, axis=-1), 0.0)
        o_ref[:, _head(h)] = _dot(q_blocks.astype(BF16),
                                  state.reshape(per * HEAD_DIM, HEAD_DIM).astype(BF16))
        for j in range(per):
            sout_ref[j, h] = g1 * state[j] + kt[:, j:j + 1] * v8[j:j + 1, :]


def _sample_pool_update(us_ref, pin_ref, pout_ref, pooled_ref):
    u = us_ref[...]
    pout_ref[0:POOL_BUF - 1] = pin_ref[1:POOL_BUF]
    pout_ref[POOL_BUF - 1] = u
    for g, w in enumerate(POOL_WINDOWS):
        cols = slice(g * POOL_GROUP, (g + 1) * POOL_GROUP)
        win_sum = u[:, cols]
        for r in range(POOL_BUF - (w - 1), POOL_BUF):
            win_sum = win_sum + pin_ref[r, :, cols]
        pooled_ref[:, cols] = win_sum / float(w) - u[:, cols]


def _main_kernel(x_ref, cos_ref, sin_ref, smeta_ref, umeta_ref, normw_ref, pscale_ref, rnw_ref,
                 fnw_ref, win_ref, wu_ref, wout_ref, qkv_ref, sstate_ref, us_ref, pin_ref,
                 rest_ref, wpool_ref,
                 y_ref, s_ref, pbuf_ref, sout_ref, pout_ref, ys_ref,
                 hist_ref, so_ref, pooled_ref, o_ref):
    c = pl.program_id(1)
    nb = x_ref.shape[0]
    batches = range(nb)

    @pl.when(c == 0)
    def _():
        for b in batches:
            s_ref[b] = smeta_ref[...]
            hist_ref[b] = umeta_ref[...]

    x = x_ref[...].reshape(nb * CHUNK, D_MODEL)
    hb = _rms(x, normw_ref[...]).astype(BF16)

    def proj(off):
        return _dot(hb, _unpack_rows(win_ref[:, off:off + TILE_N]))

    def rows(t, b):
        return t[b * CHUNK:(b + 1) * CHUNK]

    cosf, sinf = cos_ref[...], sin_ref[...]
    li = lax.broadcasted_iota(jnp.int32, (CHUNK, CHUNK), 0).astype(F32)
    mi = lax.broadcasted_iota(jnp.int32, (CHUNK, CHUNK), 1).astype(F32)
    diff = li - mi

    qkv, part, outs = {}, {}, {}

    def issue_qkv(pair):
        qkv[pair] = tuple(proj(off + pair * TILE_N) for off in (OFF_Q, OFF_K, OFF_V))

    def stage_a(pair):
        q2, k2, v2 = qkv.pop(pair)
        for i in range(2):
            h = 2 * pair + i
            lg = LOG_DECAY[h]
            sub = slice(i * HEAD_DIM, (i + 1) * HEAD_DIM)
            dmask = jnp.where(diff >= 0.0, jnp.exp(lg * jnp.maximum(diff, 0.0)), 0.0)
            q_decay = jnp.exp(lg * (li + 1.0))
            k_decay = jnp.exp(lg * (CHUNK - 1.0 - li))
            for b in batches:
                qr = _rotary(rows(q2, b)[:, sub], cosf, sinf)
                kr = _rotary(rows(k2, b)[:, sub], cosf, sinf) * K_SCALE
                vb = rows(v2, b)[:, sub].astype(BF16)
                state = s_ref[b, h]
                scores = lax.dot_general(qr.astype(BF16), kr.astype(BF16), (((1,), (1,)), ((), ())),
                                         preferred_element_type=F32)
                cross = _dot((qr * q_decay).astype(BF16), state.astype(BF16))
                s_ref[b, h] = math.exp(lg * CHUNK) * state + _dot_t((kr * k_decay).astype(BF16), vb)
                part[b, h] = ((scores * dmask).astype(BF16), vb, cross)

    def stage_b(pair):
        for i in range(2):
            h = 2 * pair + i
            for b in batches:
                p, vb, cross = part.pop((b, h))
                outs[b, h] = _dot(p, vb) + cross

    def pool_windows(u_tiles):
        pooled = []
        for g, w in enumerate(POOL_WINDOWS):
            cols = slice(g * POOL_GROUP, (g + 1) * POOL_GROUP)
            per_batch = []
            for b in batches:
                u = rows(u_tiles[g], b)
                ext = jnp.concatenate([hist_ref[b, :, cols], u], axis=0)
                win_sum = ext
                shift = 1
                while shift < w:
                    win_sum = win_sum + pltpu.roll(win_sum, shift, 0)
                    shift *= 2
                per_batch.append(win_sum[HIST:] / float(w) - u)
                hist_ref[b, :, cols] = u[CHUNK - HIST:]
            pooled.append(jnp.concatenate(per_batch, axis=0))
        return pooled

    step = pl.program_id(0) * pl.num_programs(1) + c
    so_step = so_ref.at[step]
    _sample_state_update(qkv_ref, sstate_ref, sout_ref, so_step, range(0, SAMPLE_HEADS_FIRST))
    block_rows = pl.ds(pl.multiple_of(step * sstate_ref.shape[0] // SUBLANES * SUBLANES, SUBLANES),
                       SUBLANES)
    _sample_pool_update(us_ref, pin_ref, pout_ref, pooled_ref.at[block_rows])
    issue_qkv(0)
    issue_qkv(1)
    stage_a(0)
    issue_qkv(2)
    stage_a(1)
    stage_b(0)
    issue_qkv(3)
    u_tiles = [proj(OFF_U + g * POOL_GROUP) for g in range(len(POOL_WINDOWS))]
    stage_a(2)
    stage_b(1)
    gp_tiles = [proj(OFF_GP + g * POOL_GROUP) for g in range(len(POOL_WINDOWS))]
    mixed = pool_windows(u_tiles)
    stage_a(3)
    stage_b(2)
    gr_tiles = [proj(OFF_GR + pair * TILE_N) for pair in range(N_HEADS // 2)]
    stage_b(3)
    pool_mix = jnp.concatenate(
        [(mixed[g] * pscale_ref[:, g * POOL_GROUP:(g + 1) * POOL_GROUP] * _silu(gp_tiles[g])).astype(BF16)
         for g in range(len(POOL_WINDOWS))], axis=-1)
    acc = x + _dot(pool_mix, _unpack_rows(wout_ref[0:D_POOL // 2, :]))

    per_piece = max(nb // OUT_PIECES, 1)
    for first in range(0, nb, per_piece):
        piece = range(first, first + per_piece)
        ret_y = []
        for b in piece:
            heads = []
            for h in range(N_HEADS):
                sub = slice((h % 2) * HEAD_DIM, (h % 2 + 1) * HEAD_DIM)
                o = outs.pop((b, h))
                rn = o * lax.rsqrt(jnp.mean(o * o, axis=-1, keepdims=True) + EPS)
                gate = _silu(rows(gr_tiles[h // 2], b)[:, sub])
                heads.append((rn * rnw_ref[:, _head(h)] * gate).astype(BF16))
            ret_y.append(jnp.concatenate(heads, axis=-1))
        ret_mix = jnp.concatenate(ret_y, axis=0)
        acc_piece = acc[first * CHUNK:(first + per_piece) * CHUNK]
        acc_piece = acc_piece + _dot(ret_mix, _unpack_rows(wout_ref[D_POOL // 2:D_MIX // 2, :]))
        y_ref[first:first + per_piece] = _rms(acc_piece, fnw_ref[...]).reshape(
            per_piece, CHUNK, D_MODEL)

    _sample_state_update(qkv_ref, sstate_ref, sout_ref, so_step, range(SAMPLE_HEADS_FIRST, N_HEADS))

    @pl.when(step == pl.num_programs(0) * pl.num_programs(1) - 1)
    def _():
        _sample_tail(rest_ref, so_ref, pooled_ref, pscale_ref, rnw_ref, fnw_ref, wpool_ref,
                     wout_ref, ys_ref, o_ref)

    @pl.when(c == pl.num_programs(1) - 1)
    def _():
        last = jnp.concatenate([rows(hb, b)[CHUNK - HIST:] for b in batches], axis=0)
        u_last = _dot(last, _unpack_rows(wu_ref[...]))
        for group in range(pbuf_ref.shape[1] // nb):
            @pl.when(pl.program_id(0) == group)
            def _():
                for b in batches:
                    seq = group * nb + b
                    for r in range(HIST - POOL_BUF, HIST):
                        pbuf_ref[r - (HIST - POOL_BUF), seq:seq + 1, :] = (
                            u_last[b * HIST + r:b * HIST + r + 1, :])


def _main_call(x, cosf, sinf, smeta, umeta, normw, pscale, rnw, fnw, win_bf, wu_bf, wout_bf,
               qkv_s, state_s, rest_s, pool_t, wpool_bf):
    batch, seq, _ = x.shape
    nb = PROMPT_BLOCK
    n_chunks = seq // CHUNK
    n_sample = state_s.shape[0]
    per = n_sample // ((batch // nb) * n_chunks)

    def const(shape):
        zeros = (0,) * len(shape)
        return pl.BlockSpec(shape, lambda b, c: zeros, pipeline_mode=pl.Buffered(1))

    step = lambda b, c: b * n_chunks + c
    pool_block = lambda b, c: step(b, c) * per // SUBLANES
    return pl.pallas_call(
        _main_kernel,
        grid=(batch // nb, n_chunks),
        in_specs=[
            pl.BlockSpec((nb, CHUNK, D_MODEL), lambda b, c: (b, c, 0)),
            pl.BlockSpec((CHUNK, HEAD_DIM), lambda b, c: (c, 0)),
            pl.BlockSpec((CHUNK, HEAD_DIM), lambda b, c: (c, 0)),
            const((N_HEADS, HEAD_DIM, HEAD_DIM)),
            const((N_META, D_POOL)),
            const((1, D_MODEL)),
            const((1, D_POOL)),
            const((1, D_RET)),
            const((1, D_MODEL)),
            const((D_MODEL // 2, D_IN_PROJ)),
            const((D_MODEL // 2, D_POOL)),
            const((D_MIX // 2, D_MODEL)),
            pl.BlockSpec((None, SUBLANES, 3 * D_RET), lambda b, c: (step(b, c), 0, 0)),
            pl.BlockSpec((per, N_HEADS, HEAD_DIM, HEAD_DIM), lambda b, c: (step(b, c), 0, 0, 0)),
            pl.BlockSpec((SUBLANES, D_POOL), lambda b, c: (pool_block(b, c), 0)),
            pl.BlockSpec((POOL_BUF, SUBLANES, D_POOL), lambda b, c: (0, pool_block(b, c), 0)),
            const((n_sample, D_REST)),
            const((len(POOL_WINDOWS), POOL_GROUP // 2, POOL_GROUP)),
        ],
        out_specs=[
            pl.BlockSpec((nb, CHUNK, D_MODEL), lambda b, c: (b, c, 0)),
            pl.BlockSpec((nb, N_HEADS, HEAD_DIM, HEAD_DIM), lambda b, c: (b, 0, 0, 0)),
            pl.BlockSpec((POOL_BUF, batch, D_POOL), lambda b, c: (0, 0, 0)),
            pl.BlockSpec((per, N_HEADS, HEAD_DIM, HEAD_DIM), lambda b, c: (step(b, c), 0, 0, 0)),
            pl.BlockSpec((POOL_BUF, SUBLANES, D_POOL), lambda b, c: (0, pool_block(b, c), 0)),
            pl.BlockSpec((n_sample, 1, D_MODEL), lambda b, c: (0, 0, 0)),
        ],
        out_shape=(jax.ShapeDtypeStruct((batch, seq, D_MODEL), F32),
                   jax.ShapeDtypeStruct((batch, N_HEADS, HEAD_DIM, HEAD_DIM), F32),
                   jax.ShapeDtypeStruct((POOL_BUF, batch, D_POOL), F32),
                   jax.ShapeDtypeStruct((n_sample, N_HEADS, HEAD_DIM, HEAD_DIM), F32),
                   jax.ShapeDtypeStruct((POOL_BUF, n_sample, D_POOL), F32),
                   jax.ShapeDtypeStruct((n_sample, 1, D_MODEL), F32)),
        scratch_shapes=[pltpu.VMEM((nb, HIST, D_POOL), F32),
                        pltpu.VMEM((qkv_s.shape[0], SUBLANES, D_RET), F32),
                        pltpu.VMEM((n_sample, D_POOL), F32),
                        pltpu.VMEM((n_sample, D_RET), F32)],
        compiler_params=pltpu.CompilerParams(
            dimension_semantics=("arbitrary", "arbitrary"), vmem_limit_bytes=VMEM_LIMIT_BYTES),
        name="main",
    )(x, cosf, sinf, smeta, umeta, normw, pscale, rnw, fnw, win_bf, wu_bf, wout_bf,
      qkv_s, state_s, rest_s, pool_t, rest_s, wpool_bf)


def _sample_tail(rest_ref, so_ref, pooled_ref, pscale_ref, rnw_ref, fnw_ref, wpool_ref,
                 wout_ref, y_ref, o_ref):
    n_steps = so_ref.shape[0]
    per = rest_ref.shape[0] // n_steps
    for i in range(n_steps):
        o_ref[i * per:(i + 1) * per, :] = so_ref[i, 0:per, :]

    parts = []
    for g in range(len(POOL_WINDOWS)):
        cols = slice(g * POOL_GROUP, (g + 1) * POOL_GROUP)
        mixed = _dot(pooled_ref[:, cols].astype(BF16), _unpack_rows(wpool_ref[g])) * pscale_ref[:, cols]
        gate = _silu(rest_ref[:, D_POOL + g * POOL_GROUP:D_POOL + (g + 1) * POOL_GROUP])
        parts.append((mixed * gate).astype(BF16))
    for h in range(N_HEADS):
        o = o_ref[:, _head(h)] + rest_ref[:, 2 * D_POOL + D_RET + h * HEAD_DIM:
                                          2 * D_POOL + D_RET + (h + 1) * HEAD_DIM]
        rn = o * lax.rsqrt(jnp.mean(o * o, axis=-1, keepdims=True) + EPS)
        gr = rest_ref[:, 2 * D_POOL + h * HEAD_DIM:2 * D_POOL + (h + 1) * HEAD_DIM]
        parts.append((rn * rnw_ref[:, _head(h)] * _silu(gr)).astype(BF16))
    mix = jnp.concatenate(parts, axis=-1)
    x = rest_ref[:, OFF_REST_X:OFF_REST_X + D_MODEL]
    y_ref[:, 0, :] = _rms(x + _dot(mix, _unpack_rows(wout_ref[...])), fnw_ref[...])


def _rotary_tables(pos):
    half = HEAD_DIM // 2
    inv = ROPE_BASE ** (-np.arange(half, dtype=np.float64) / half)
    ang = np.asarray(pos, np.float64)[:, None] * inv[None, :]
    cos, sin = np.cos(ang), np.sin(ang)
    return (jnp.asarray(np.concatenate([cos, cos], axis=-1), F32),
            jnp.asarray(np.concatenate([-sin, sin], axis=-1), F32))


def kernel(x_prompt, x_sample, state_ret, state_pool, meta_tokens, norm_w, w_in, w_pool,
           pool_scale, ret_norm_w, w_out, final_norm_w):
    assert norm_w.shape[0] == 1, "single-layer stack"
    batch, seq, _ = x_prompt.shape
    n_sample = x_sample.shape[0]
    n_steps = (batch // PROMPT_BLOCK) * (seq // CHUNK)
    per = n_sample // n_steps
    assert per * n_steps == n_sample and SUBLANES % per == 0
    normw, pscale, rnw = norm_w, pool_scale, ret_norm_w
    fnw = final_norm_w[None, :]

    cos_p, sin_p = _rotary_tables(np.arange(N_META + seq))
    cos_s, sin_s = _rotary_tables(PAST_LEN + np.arange(1))

    win_bf, wu_bf, wout_bf, wpool_bf, smeta, umeta, qkv_s, rest_s = _prep_call(
        meta_tokens.astype(x_prompt.dtype), x_sample, cos_p[:N_META], sin_p[:N_META], cos_s, sin_s,
        normw, w_in[0], w_out[0], w_pool[0], n_steps)
    y_p, s_p, buf_p, s_s, buf_s, y_s = _main_call(
        x_prompt, cos_p[N_META:], sin_p[N_META:], smeta, umeta, normw, pscale, rnw, fnw,
        win_bf, wu_bf, wout_bf, qkv_s, state_ret[0], rest_s, jnp.transpose(state_pool[0], (1, 0, 2)),
        wpool_bf)
    return (y_p, y_s, s_p[None], s_s[None], jnp.transpose(buf_p, (1, 0, 2))[None],
            jnp.transpose(buf_s, (1, 0, 2))[None])
```

```python
import math

import jax
import jax.numpy as jnp
import numpy as np
from jax import lax
from jax.experimental import pallas as pl
from jax.experimental.pallas import tpu as pltpu

D_MODEL = 1024
D_POOL = 1024
D_RET = 1024
D_MIX = D_POOL + D_RET
POOL_WINDOWS = (2, 4, 8, 16)
POOL_GROUP = D_POOL // len(POOL_WINDOWS)
POOL_BUF = max(POOL_WINDOWS) - 1
N_HEADS = 8
HEAD_DIM = D_RET // N_HEADS
D_IN_PROJ = 2 * D_POOL + 4 * D_RET
N_META = 16
PAST_LEN = 16384
CHUNK = 128
ROPE_BASE = 10000.0
EPS = 1e-6
K_SCALE = HEAD_DIM ** -0.5

OFF_U, OFF_GP, OFF_Q, OFF_K, OFF_V, OFF_GR = (i * 1024 for i in range(6))

OFF_REST_X = 2 * D_POOL + 2 * D_RET
D_REST = OFF_REST_X + D_MODEL

LOG_DECAY = tuple(math.log(1.0 - 2.0 ** (-5.0 - h)) for h in range(N_HEADS))

SUBLANES = 8
TILE_N = 256
PROMPT_BLOCK = 4
OUT_PIECES = 2
SAMPLE_HEADS_FIRST = 4
PREP_TILES = 4
HIST = 16

VMEM_LIMIT_BYTES = 60 * 1024 * 1024

F32 = jnp.float32
BF16 = jnp.bfloat16
PACKED = jnp.uint32


def _rms(x, w):
    return x * lax.rsqrt(jnp.mean(x * x, axis=-1, keepdims=True) + EPS) * w


def _silu(x):
    return x * (1.0 / (1.0 + jnp.exp(-x)))


def _dot(a, b):
    return jnp.dot(a, b, preferred_element_type=F32)


def _dot_t(a, b):
    return lax.dot_general(a, b, (((0,), (0,)), ((), ())), preferred_element_type=F32)


def _pack_rows(w):
    return pltpu.bitcast(w, PACKED)


def _unpack_rows(w):
    return pltpu.bitcast(w, BF16)


def _rotary(x, cosf, sinf):
    return x * cosf + pltpu.roll(x, HEAD_DIM // 2, 1) * sinf


def _head(h):
    return slice(h * HEAD_DIM, (h + 1) * HEAD_DIM)


def _prep_kernel(meta_ref, xs_ref, cos_ref, sin_ref, coss_ref, sins_ref, normw_ref,
                 win_ref, wout_ref, wpool_ref,
                 winb_ref, wub_ref, woutb_ref, wpoolb_ref, s_ref, u_ref, qkv_ref, rest_ref,
                 hb_ref, hs_ref, proj_ref, projs_ref, kd_ref, v_ref, stage_ref, wphi_ref, wplo_ref):
    j = pl.program_id(0)
    k_rows = D_MODEL // PREP_TILES

    @pl.when(j == 0)
    def _():
        hb = _rms(meta_ref[...], normw_ref[...]).astype(BF16)
        xs = xs_ref[:, 0, :]
        rest_ref[:, OFF_REST_X:OFF_REST_X + D_MODEL] = xs
        hs = _rms(xs, normw_ref[...]).astype(BF16)
        for t in range(PREP_TILES):
            hb_ref[t] = hb[:, t * k_rows:(t + 1) * k_rows]
            hs_ref[t] = hs[:, t * k_rows:(t + 1) * k_rows]
        wp = wpool_ref[...]
        wp_hi = wp.astype(BF16)
        wpoolb_ref[...] = _pack_rows(wp_hi)
        wphi_ref[...] = wp_hi
        wplo_ref[...] = (wp - wp_hi.astype(F32)).astype(BF16)
        proj_ref[...] = jnp.zeros_like(proj_ref)
        projs_ref[...] = jnp.zeros_like(projs_ref)

    tile = win_ref[...].astype(BF16)
    folded = []
    for g in range(len(POOL_WINDOWS)):
        cols = slice(OFF_U + g * POOL_GROUP, OFF_U + (g + 1) * POOL_GROUP)
        a_hi = tile[:, cols]
        a_lo = (win_ref[:, cols] - a_hi.astype(F32)).astype(BF16)
        folded.append((_dot(a_hi, wphi_ref[g]) + _dot(a_hi, wplo_ref[g])
                       + _dot(a_lo, wphi_ref[g])).astype(BF16))
    tile_main = jnp.concatenate(folded + [tile[:, OFF_U + D_POOL:]], axis=-1)
    winb_ref[...] = _pack_rows(tile_main)
    wub_ref[...] = _pack_rows(tile[:, OFF_U:OFF_U + D_POOL])
    woutb_ref[...] = _pack_rows(wout_ref[...].astype(BF16))
    proj_ref[...] += _dot(hb_ref[j], tile_main)
    projs_ref[...] += _dot(hs_ref[j], tile)

    @pl.when(j == pl.num_programs(0) - 1)
    def _():
        cosf, sinf = cos_ref[...], sin_ref[...]
        row = lax.broadcasted_iota(jnp.int32, (N_META, HEAD_DIM), 0).astype(F32)
        kd_ref[...] = jnp.zeros_like(kd_ref)
        v_ref[...] = jnp.zeros_like(v_ref)
        u_ref[...] = proj_ref[:, OFF_U:OFF_U + D_POOL]
        for h in range(N_HEADS):
            kr = _rotary(proj_ref[:, OFF_K + h * HEAD_DIM:OFF_K + (h + 1) * HEAD_DIM], cosf, sinf)
            kd_ref[0:N_META, _head(h)] = kr * K_SCALE * jnp.exp(LOG_DECAY[h] * (N_META - 1.0 - row))
            v_ref[0:N_META, _head(h)] = proj_ref[:, OFF_V + h * HEAD_DIM:OFF_V + (h + 1) * HEAD_DIM]
        for h in range(N_HEADS):
            s_ref[h] = _dot_t(kd_ref[:, _head(h)].astype(BF16), v_ref[:, _head(h)].astype(BF16))

        coss, sins = coss_ref[...], sins_ref[...]
        rest_ref[:, 0:D_POOL] = projs_ref[:, OFF_U:OFF_U + D_POOL]
        rest_ref[:, D_POOL:2 * D_POOL] = projs_ref[:, OFF_GP:OFF_GP + D_POOL]
        rest_ref[:, 2 * D_POOL:2 * D_POOL + D_RET] = projs_ref[:, OFF_GR:OFF_GR + D_RET]
        for h in range(N_HEADS):
            qr = _rotary(projs_ref[:, OFF_Q + h * HEAD_DIM:OFF_Q + (h + 1) * HEAD_DIM], coss, sins)
            kr = _rotary(projs_ref[:, OFF_K + h * HEAD_DIM:OFF_K + (h + 1) * HEAD_DIM], coss, sins)
            kr = kr * K_SCALE
            v = projs_ref[:, OFF_V + h * HEAD_DIM:OFF_V + (h + 1) * HEAD_DIM]
            stage_ref[:, _head(h)] = qr * math.exp(LOG_DECAY[h])
            stage_ref[:, D_RET + h * HEAD_DIM:D_RET + (h + 1) * HEAD_DIM] = kr
            stage_ref[:, 2 * D_RET + h * HEAD_DIM:2 * D_RET + (h + 1) * HEAD_DIM] = v
            rest_ref[:, 2 * D_POOL + D_RET + h * HEAD_DIM:2 * D_POOL + D_RET + (h + 1) * HEAD_DIM] = (
                jnp.sum(qr * kr, axis=-1, keepdims=True) * v)
        n_steps, per = qkv_ref.shape[0], stage_ref.shape[0] // qkv_ref.shape[0]
        qkv_ref[...] = jnp.zeros_like(qkv_ref)
        for i in range(n_steps):
            qkv_ref[i, 0:per, :] = stage_ref[i * per:(i + 1) * per, :]


def _prep_call(meta, xs, cosf, sinf, coss, sins, normw, w_in, w_out, w_pool, n_steps):
    n = xs.shape[0]

    def const(shape):
        zeros = (0,) * len(shape)
        return pl.BlockSpec(shape, lambda j: zeros)

    row_tile = lambda j: (j, 0)
    return pl.pallas_call(
        _prep_kernel,
        grid=(PREP_TILES,),
        in_specs=[
            const((N_META, D_MODEL)),
            const((n, 1, D_MODEL)),
            const((N_META, HEAD_DIM)),
            const((N_META, HEAD_DIM)),
            const((1, HEAD_DIM)),
            const((1, HEAD_DIM)),
            const((1, D_MODEL)),
            pl.BlockSpec((D_MODEL // PREP_TILES, D_IN_PROJ), row_tile),
            pl.BlockSpec((D_MIX // PREP_TILES, D_MODEL), row_tile),
            const((len(POOL_WINDOWS), POOL_GROUP, POOL_GROUP)),
        ],
        out_specs=[
            pl.BlockSpec((D_MODEL // PREP_TILES // 2, D_IN_PROJ), row_tile),
            pl.BlockSpec((D_MODEL // PREP_TILES // 2, D_POOL), row_tile),
            pl.BlockSpec((D_MIX // PREP_TILES // 2, D_MODEL), row_tile),
            const((len(POOL_WINDOWS), POOL_GROUP // 2, POOL_GROUP)),
            const((N_HEADS, HEAD_DIM, HEAD_DIM)),
            const((N_META, D_POOL)),
            const((n_steps, SUBLANES, 3 * D_RET)),
            const((n, D_REST)),
        ],
        out_shape=(jax.ShapeDtypeStruct((D_MODEL // 2, D_IN_PROJ), PACKED),
                   jax.ShapeDtypeStruct((D_MODEL // 2, D_POOL), PACKED),
                   jax.ShapeDtypeStruct((D_MIX // 2, D_MODEL), PACKED),
                   jax.ShapeDtypeStruct((len(POOL_WINDOWS), POOL_GROUP // 2, POOL_GROUP), PACKED),
                   jax.ShapeDtypeStruct((N_HEADS, HEAD_DIM, HEAD_DIM), F32),
                   jax.ShapeDtypeStruct((N_META, D_POOL), F32),
                   jax.ShapeDtypeStruct((n_steps, SUBLANES, 3 * D_RET), F32),
                   jax.ShapeDtypeStruct((n, D_REST), F32)),
        scratch_shapes=[pltpu.VMEM((PREP_TILES, N_META, D_MODEL // PREP_TILES), BF16),
                        pltpu.VMEM((PREP_TILES, n, D_MODEL // PREP_TILES), BF16),
                        pltpu.VMEM((N_META, D_IN_PROJ), F32),
                        pltpu.VMEM((n, D_IN_PROJ), F32),
                        pltpu.VMEM((CHUNK, D_RET), F32),
                        pltpu.VMEM((CHUNK, D_RET), F32),
                        pltpu.VMEM((n, 3 * D_RET), F32),
                        pltpu.VMEM((len(POOL_WINDOWS), POOL_GROUP, POOL_GROUP), BF16),
                        pltpu.VMEM((len(POOL_WINDOWS), POOL_GROUP, POOL_GROUP), BF16)],
        compiler_params=pltpu.CompilerParams(
            dimension_semantics=("arbitrary",), vmem_limit_bytes=VMEM_LIMIT_BYTES),
        name="prep",
    )(meta, xs, cosf, sinf, coss, sins, normw, w_in, w_out, w_pool)


def _sample_state_update(qkv_ref, state_ref, sout_ref, o_ref, heads):
    per = state_ref.shape[0]
    seq_of_row = lax.broadcasted_iota(jnp.int32, (SUBLANES, per * HEAD_DIM), 0)
    seq_of_col = lax.broadcasted_iota(jnp.int32, (SUBLANES, per * HEAD_DIM), 1) // HEAD_DIM
    own_block = seq_of_row == seq_of_col
    for h in heads:
        g1 = math.exp(LOG_DECAY[h])
        qd = qkv_ref[:, _head(h)]
        kt = qkv_ref[:, D_RET + h * HEAD_DIM:D_RET + (h + 1) * HEAD_DIM].T
        v8 = qkv_ref[:, 2 * D_RET + h * HEAD_DIM:2 * D_RET + (h + 1) * HEAD_DIM]
        state = state_ref[:, h]
        q_blocks = jnp.where(own_block, jnp.concatenate([qd] * per, axis=-1), 0.0)
        o_ref[:, _head(h)] = _dot(q_blocks.astype(BF16),
                                  state.reshape(per * HEAD_DIM, HEAD_DIM).astype(BF16))
        for j in range(per):
            sout_ref[j, h] = g1 * state[j] + kt[:, j:j + 1] * v8[j:j + 1, :]


def _sample_pool_update(us_ref, pin_ref, pout_ref, pooled_ref):
    u = us_ref[...]
    pout_ref[0:POOL_BUF - 1] = pin_ref[1:POOL_BUF]
    pout_ref[POOL_BUF - 1] = u
    for g, w in enumerate(POOL_WINDOWS):
        cols = slice(g * POOL_GROUP, (g + 1) * POOL_GROUP)
        win_sum = u[:, cols]
        for r in range(POOL_BUF - (w - 1), POOL_BUF):
            win_sum = win_sum + pin_ref[r, :, cols]
        pooled_ref[:, cols] = win_sum / float(w) - u[:, cols]


def _main_kernel(x_ref, cos_ref, sin_ref, smeta_ref, umeta_ref, normw_ref, pscale_ref, rnw_ref,
                 fnw_ref, win_ref, wu_ref, wout_ref, qkv_ref, sstate_ref, us_ref, pin_ref,
                 rest_ref, wpool_ref,
                 y_ref, s_ref, pbuf_ref, sout_ref, pout_ref, ys_ref,
                 hist_ref, so_ref, pooled_ref, o_ref):
    c = pl.program_id(1)
    nb = x_ref.shape[0]
    batches = range(nb)

    @pl.when(c == 0)
    def _():
        for b in batches:
            s_ref[b] = smeta_ref[...]
            hist_ref[b] = umeta_ref[...]

    x = x_ref[...].reshape(nb * CHUNK, D_MODEL)
    hb = _rms(x, normw_ref[...]).astype(BF16)

    def proj(off):
        return _dot(hb, _unpack_rows(win_ref[:, off:off + TILE_N]))

    def rows(t, b):
        return t[b * CHUNK:(b + 1) * CHUNK]

    cosf, sinf = cos_ref[...], sin_ref[...]
    li = lax.broadcasted_iota(jnp.int32, (CHUNK, CHUNK), 0).astype(F32)
    mi = lax.broadcasted_iota(jnp.int32, (CHUNK, CHUNK), 1).astype(F32)
    diff = li - mi

    qkv, part, outs = {}, {}, {}

    def issue_qkv(pair):
        qkv[pair] = tuple(proj(off + pair * TILE_N) for off in (OFF_Q, OFF_K, OFF_V))

    def stage_a(pair):
        q2, k2, v2 = qkv.pop(pair)
        for i in range(2):
            h = 2 * pair + i
            lg = LOG_DECAY[h]
            sub = slice(i * HEAD_DIM, (i + 1) * HEAD_DIM)
            dmask = jnp.where(diff >= 0.0, jnp.exp(lg * jnp.maximum(diff, 0.0)), 0.0)
            q_decay = jnp.exp(lg * (li + 1.0))
            k_decay = jnp.exp(lg * (CHUNK - 1.0 - li))
            for b in batches:
                qr = _rotary(rows(q2, b)[:, sub], cosf, sinf)
                kr = _rotary(rows(k2, b)[:, sub], cosf, sinf) * K_SCALE
                vb = rows(v2, b)[:, sub].astype(BF16)
                state = s_ref[b, h]
                scores = lax.dot_general(qr.astype(BF16), kr.astype(BF16), (((1,), (1,)), ((), ())),
                                         preferred_element_type=F32)
                cross = _dot((qr * q_decay).astype(BF16), state.astype(BF16))
                s_ref[b, h] = math.exp(lg * CHUNK) * state + _dot_t((kr * k_decay).astype(BF16), vb)
                part[b, h] = ((scores * dmask).astype(BF16), vb, cross)

    def stage_b(pair):
        for i in range(2):
            h = 2 * pair + i
            for b in batches:
                p, vb, cross = part.pop((b, h))
                outs[b, h] = _dot(p, vb) + cross

    def pool_windows(u_tiles):
        pooled = []
        for g, w in enumerate(POOL_WINDOWS):
            cols = slice(g * POOL_GROUP, (g + 1) * POOL_GROUP)
            per_batch = []
            for b in batches:
                u = rows(u_tiles[g], b)
                ext = jnp.concatenate([hist_ref[b, :, cols], u], axis=0)
                win_sum = ext
                shift = 1
                while shift < w:
                    win_sum = win_sum + pltpu.roll(win_sum, shift, 0)
                    shift *= 2
                per_batch.append(win_sum[HIST:] / float(w) - u)
                hist_ref[b, :, cols] = u[CHUNK - HIST:]
            pooled.append(jnp.concatenate(per_batch, axis=0))
        return pooled

    step = pl.program_id(0) * pl.num_programs(1) + c
    so_step = so_ref.at[step]
    _sample_state_update(qkv_ref, sstate_ref, sout_ref, so_step, range(0, SAMPLE_HEADS_FIRST))
    block_rows = pl.ds(pl.multiple_of(step * sstate_ref.shape[0] // SUBLANES * SUBLANES, SUBLANES),
                       SUBLANES)
    _sample_pool_update(us_ref, pin_ref, pout_ref, pooled_ref.at[block_rows])
    issue_qkv(0)
    issue_qkv(1)
    stage_a(0)
    issue_qkv(2)
    stage_a(1)
    stage_b(0)
    issue_qkv(3)
    u_tiles = [proj(OFF_U + g * POOL_GROUP) for g in range(len(POOL_WINDOWS))]
    stage_a(2)
    stage_b(1)
    gp_tiles = [proj(OFF_GP + g * POOL_GROUP) for g in range(len(POOL_WINDOWS))]
    mixed = pool_windows(u_tiles)
    stage_a(3)
    stage_b(2)
    gr_tiles = [proj(OFF_GR + pair * TILE_N) for pair in range(N_HEADS // 2)]
    stage_b(3)
    pool_mix = jnp.concatenate(
        [(mixed[g] * pscale_ref[:, g * POOL_GROUP:(g + 1) * POOL_GROUP] * _silu(gp_tiles[g])).astype(BF16)
         for g in range(len(POOL_WINDOWS))], axis=-1)
    acc = x + _dot(pool_mix, _unpack_rows(wout_ref[0:D_POOL // 2, :]))

    per_piece = max(nb // OUT_PIECES, 1)
    for first in range(0, nb, per_piece):
        piece = range(first, first + per_piece)
        ret_y = []
        for b in piece:
            heads = []
            for h in range(N_HEADS):
                sub = slice((h % 2) * HEAD_DIM, (h % 2 + 1) * HEAD_DIM)
                o = outs.pop((b, h))
                rn = o * lax.rsqrt(jnp.mean(o * o, axis=-1, keepdims=True) + EPS)
                gate = _silu(rows(gr_tiles[h // 2], b)[:, sub])
                heads.append((rn * rnw_ref[:, _head(h)] * gate).astype(BF16))
            ret_y.append(jnp.concatenate(heads, axis=-1))
        ret_mix = jnp.concatenate(ret_y, axis=0)
        acc_piece = acc[first * CHUNK:(first + per_piece) * CHUNK]
        acc_piece = acc_piece + _dot(ret_mix, _unpack_rows(wout_ref[D_POOL // 2:D_MIX // 2, :]))
        y_ref[first:first + per_piece] = _rms(acc_piece, fnw_ref[...]).reshape(
            per_piece, CHUNK, D_MODEL)

    _sample_state_update(qkv_ref, sstate_ref, sout_ref, so_step, range(SAMPLE_HEADS_FIRST, N_HEADS))

    @pl.when(step == pl.num_programs(0) * pl.num_programs(1) - 1)
    def _():
        _sample_tail(rest_ref, so_ref, pooled_ref, pscale_ref, rnw_ref, fnw_ref, wpool_ref,
                     wout_ref, ys_ref, o_ref)

    @pl.when(c == pl.num_programs(1) - 1)
    def _():
        last = jnp.concatenate([rows(hb, b)[CHUNK - HIST:] for b in batches], axis=0)
        u_last = _dot(last, _unpack_rows(wu_ref[...]))
        for group in range(pbuf_ref.shape[1] // nb):
            @pl.when(pl.program_id(0) == group)
            def _():
                for b in batches:
                    seq = group * nb + b
                    for r in range(HIST - POOL_BUF, HIST):
                        pbuf_ref[r - (HIST - POOL_BUF), seq:seq + 1, :] = (
                            u_last[b * HIST + r:b * HIST + r + 1, :])


def _main_call(x, cosf, sinf, smeta, umeta, normw, pscale, rnw, fnw, win_bf, wu_bf, wout_bf,
               qkv_s, state_s, rest_s, pool_t, wpool_bf):
    batch, seq, _ = x.shape
    nb = PROMPT_BLOCK
    n_chunks = seq // CHUNK
    n_sample = state_s.shape[0]
    per = n_sample // ((batch // nb) * n_chunks)

    def const(shape):
        zeros = (0,) * len(shape)
        return pl.BlockSpec(shape, lambda b, c: zeros, pipeline_mode=pl.Buffered(1))

    step = lambda b, c: b * n_chunks + c
    pool_block = lambda b, c: step(b, c) * per // SUBLANES
    return pl.pallas_call(
        _main_kernel,
        grid=(batch // nb, n_chunks),
        in_specs=[
            pl.BlockSpec((nb, CHUNK, D_MODEL), lambda b, c: (b, c, 0)),
            pl.BlockSpec((CHUNK, HEAD_DIM), lambda b, c: (c, 0)),
            pl.BlockSpec((CHUNK, HEAD_DIM), lambda b, c: (c, 0)),
            const((N_HEADS, HEAD_DIM, HEAD_DIM)),
            const((N_META, D_POOL)),
            const((1, D_MODEL)),
            const((1, D_POOL)),
            const((1, D_RET)),
            const((1, D_MODEL)),
            const((D_MODEL // 2, D_IN_PROJ)),
            const((D_MODEL // 2, D_POOL)),
            const((D_MIX // 2, D_MODEL)),
            pl.BlockSpec((None, SUBLANES, 3 * D_RET), lambda b, c: (step(b, c), 0, 0)),
            pl.BlockSpec((per, N_HEADS, HEAD_DIM, HEAD_DIM), lambda b, c: (step(b, c), 0, 0, 0)),
            pl.BlockSpec((SUBLANES, D_POOL), lambda b, c: (pool_block(b, c), 0)),
            pl.BlockSpec((POOL_BUF, SUBLANES, D_POOL), lambda b, c: (0, pool_block(b, c), 0)),
            const((n_sample, D_REST)),
            const((len(POOL_WINDOWS), POOL_GROUP // 2, POOL_GROUP)),
        ],
        out_specs=[
            pl.BlockSpec((nb, CHUNK, D_MODEL), lambda b, c: (b, c, 0)),
            pl.BlockSpec((nb, N_HEADS, HEAD_DIM, HEAD_DIM), lambda b, c: (b, 0, 0, 0)),
            pl.BlockSpec((POOL_BUF, batch, D_POOL), lambda b, c: (0, 0, 0)),
            pl.BlockSpec((per, N_HEADS, HEAD_DIM, HEAD_DIM), lambda b, c: (step(b, c), 0, 0, 0)),
            pl.BlockSpec((POOL_BUF, SUBLANES, D_POOL), lambda b, c: (0, pool_block(b, c), 0)),
            pl.BlockSpec((n_sample, 1, D_MODEL), lambda b, c: (0, 0, 0)),
        ],
        out_shape=(jax.ShapeDtypeStruct((batch, seq, D_MODEL), F32),
                   jax.ShapeDtypeStruct((batch, N_HEADS, HEAD_DIM, HEAD_DIM), F32),
                   jax.ShapeDtypeStruct((POOL_BUF, batch, D_POOL), F32),
                   jax.ShapeDtypeStruct((n_sample, N_HEADS, HEAD_DIM, HEAD_DIM), F32),
                   jax.ShapeDtypeStruct((POOL_BUF, n_sample, D_POOL), F32),
                   jax.ShapeDtypeStruct((n_sample, 1, D_MODEL), F32)),
        scratch_shapes=[pltpu.VMEM((nb, HIST, D_POOL), F32),
                        pltpu.VMEM((qkv_s.shape[0], SUBLANES, D_RET), F32),
                        pltpu.VMEM((n_sample, D_POOL), F32),
                        pltpu.VMEM((n_sample, D_RET), F32)],
        compiler_params=pltpu.CompilerParams(
            dimension_semantics=("arbitrary", "arbitrary"), vmem_limit_bytes=VMEM_LIMIT_BYTES),
        name="main",
    )(x, cosf, sinf, smeta, umeta, normw, pscale, rnw, fnw, win_bf, wu_bf, wout_bf,
      qkv_s, state_s, rest_s, pool_t, rest_s, wpool_bf)


def _sample_tail(rest_ref, so_ref, pooled_ref, pscale_ref, rnw_ref, fnw_ref, wpool_ref,
                 wout_ref, y_ref, o_ref):
    n_steps = so_ref.shape[0]
    per = rest_ref.shape[0] // n_steps
    for i in range(n_steps):
        o_ref[i * per:(i + 1) * per, :] = so_ref[i, 0:per, :]

    parts = []
    for g in range(len(POOL_WINDOWS)):
        cols = slice(g * POOL_GROUP, (g + 1) * POOL_GROUP)
        mixed = _dot(pooled_ref[:, cols].astype(BF16), _unpack_rows(wpool_ref[g])) * pscale_ref[:, cols]
        gate = _silu(rest_ref[:, D_POOL + g * POOL_GROUP:D_POOL + (g + 1) * POOL_GROUP])
        parts.append((mixed * gate).astype(BF16))
    for h in range(N_HEADS):
        o = o_ref[:, _head(h)] + rest_ref[:, 2 * D_POOL + D_RET + h * HEAD_DIM:
                                          2 * D_POOL + D_RET + (h + 1) * HEAD_DIM]
        rn = o * lax.rsqrt(jnp.mean(o * o, axis=-1, keepdims=True) + EPS)
        gr = rest_ref[:, 2 * D_POOL + h * HEAD_DIM:2 * D_POOL + (h + 1) * HEAD_DIM]
        parts.append((rn * rnw_ref[:, _head(h)] * _silu(gr)).astype(BF16))
    mix = jnp.concatenate(parts, axis=-1)
    x = rest_ref[:, OFF_REST_X:OFF_REST_X + D_MODEL]
    y_ref[:, 0, :] = _rms(x + _dot(mix, _unpack_rows(wout_ref[...])), fnw_ref[...])


def _rotary_tables(pos):
    half = HEAD_DIM // 2
    inv = ROPE_BASE ** (-np.arange(half, dtype=np.float64) / half)
    ang = np.asarray(pos, np.float64)[:, None] * inv[None, :]
    cos, sin = np.cos(ang), np.sin(ang)
    return (jnp.asarray(np.concatenate([cos, cos], axis=-1), F32),
            jnp.asarray(np.concatenate([-sin, sin], axis=-1), F32))


def kernel(x_prompt, x_sample, state_ret, state_pool, meta_tokens, norm_w, w_in, w_pool,
           pool_scale, ret_norm_w, w_out, final_norm_w):
    assert norm_w.shape[0] == 1, "single-layer stack"
    batch, seq, _ = x_prompt.shape
    n_sample = x_sample.shape[0]
    n_steps = (batch // PROMPT_BLOCK) * (seq // CHUNK)
    per = n_sample // n_steps
    assert per * n_steps == n_sample and SUBLANES % per == 0
    normw, pscale, rnw = norm_w, pool_scale, ret_norm_w
    fnw = final_norm_w[None, :]

    cos_p, sin_p = _rotary_tables(np.arange(N_META + seq))
    cos_s, sin_s = _rotary_tables(PAST_LEN + np.arange(1))

    win_bf, wu_bf, wout_bf, wpool_bf, smeta, umeta, qkv_s, rest_s = _prep_call(
        meta_tokens.astype(x_prompt.dtype), x_sample, cos_p[:N_META], sin_p[:N_META], cos_s, sin_s,
        normw, w_in[0], w_out[0], w_pool[0], n_steps)
    y_p, s_p, buf_p, s_s, buf_s, y_s = _main_call(
        x_prompt, cos_p[N_META:], sin_p[N_META:], smeta, umeta, normw, pscale, rnw, fnw,
        win_bf, wu_bf, wout_bf, qkv_s, state_ret[0], rest_s, jnp.transpose(state_pool[0], (1, 0, 2)),
        wpool_bf)
    return (y_p, y_s, s_p[None], s_s[None], jnp.transpose(buf_p, (1, 0, 2))[None],
            jnp.transpose(buf_s, (1, 0, 2))[None])
```

```python
import math

import jax
import jax.numpy as jnp
import numpy as np
from jax import lax
from jax.experimental import pallas as pl
from jax.experimental.pallas import tpu as pltpu

D_MODEL = 1024
D_POOL = 1024
D_RET = 1024
D_MIX = D_POOL + D_RET
POOL_WINDOWS = (2, 4, 8, 16)
POOL_GROUP = D_POOL // len(POOL_WINDOWS)
POOL_BUF = max(POOL_WINDOWS) - 1
N_HEADS = 8
HEAD_DIM = D_RET // N_HEADS
D_IN_PROJ = 2 * D_POOL + 4 * D_RET
N_META = 16
PAST_LEN = 16384
CHUNK = 128
ROPE_BASE = 10000.0
EPS = 1e-6
K_SCALE = HEAD_DIM ** -0.5

OFF_U, OFF_GP, OFF_Q, OFF_K, OFF_V, OFF_GR = (i * 1024 for i in range(6))

OFF_REST_X = 2 * D_POOL + 2 * D_RET
D_REST = OFF_REST_X + D_MODEL

LOG_DECAY = tuple(math.log(1.0 - 2.0 ** (-5.0 - h)) for h in range(N_HEADS))

SUBLANES = 8
TILE_N = 256
PROMPT_BLOCK = 4
OUT_PIECES = 2
SAMPLE_HEADS_FIRST = 4
PREP_TILES = 4
HIST = 16

VMEM_LIMIT_BYTES = 60 * 1024 * 1024

F32 = jnp.float32
BF16 = jnp.bfloat16
PACKED = jnp.uint32


def _rms(x, w):
    return x * lax.rsqrt(jnp.mean(x * x, axis=-1, keepdims=True) + EPS) * w


def _silu(x):
    return x * (1.0 / (1.0 + jnp.exp(-x)))


def _dot(a, b):
    return jnp.dot(a, b, preferred_element_type=F32)


def _dot_t(a, b):
    return lax.dot_general(a, b, (((0,), (0,)), ((), ())), preferred_element_type=F32)


def _pack_rows(w):
    return pltpu.bitcast(w, PACKED)


def _unpack_rows(w):
    return pltpu.bitcast(w, BF16)


def _rotary(x, cosf, sinf):
    return x * cosf + pltpu.roll(x, HEAD_DIM // 2, 1) * sinf


def _head(h):
    return slice(h * HEAD_DIM, (h + 1) * HEAD_DIM)


def _prep_kernel(meta_ref, xs_ref, cos_ref, sin_ref, coss_ref, sins_ref, normw_ref,
                 win_ref, wout_ref, wpool_ref,
                 winb_ref, wub_ref, woutb_ref, wpoolb_ref, s_ref, u_ref, qkv_ref, rest_ref,
                 hb_ref, hs_ref, proj_ref, projs_ref, kd_ref, v_ref, stage_ref, wphi_ref, wplo_ref):
    j = pl.program_id(0)
    k_rows = D_MODEL // PREP_TILES

    @pl.when(j == 0)
    def _():
        hb = _rms(meta_ref[...], normw_ref[...]).astype(BF16)
        xs = xs_ref[:, 0, :]
        rest_ref[:, OFF_REST_X:OFF_REST_X + D_MODEL] = xs
        hs = _rms(xs, normw_ref[...]).astype(BF16)
        for t in range(PREP_TILES):
            hb_ref[t] = hb[:, t * k_rows:(t + 1) * k_rows]
            hs_ref[t] = hs[:, t * k_rows:(t + 1) * k_rows]
        wp = wpool_ref[...]
        wp_hi = wp.astype(BF16)
        wpoolb_ref[...] = _pack_rows(wp_hi)
        wphi_ref[...] = wp_hi
        wplo_ref[...] = (wp - wp_hi.astype(F32)).astype(BF16)
        proj_ref[...] = jnp.zeros_like(proj_ref)
        projs_ref[...] = jnp.zeros_like(projs_ref)

    tile = win_ref[...].astype(BF16)
    folded = []
    for g in range(len(POOL_WINDOWS)):
        cols = slice(OFF_U + g * POOL_GROUP, OFF_U + (g + 1) * POOL_GROUP)
        a_hi = tile[:, cols]
        a_lo = (win_ref[:, cols] - a_hi.astype(F32)).astype(BF16)
        folded.append((_dot(a_hi, wphi_ref[g]) + _dot(a_hi, wplo_ref[g])
                       + _dot(a_lo, wphi_ref[g])).astype(BF16))
    tile_main = jnp.concatenate(folded + [tile[:, OFF_U + D_POOL:]], axis=-1)
    winb_ref[...] = _pack_rows(tile_main)
    wub_ref[...] = _pack_rows(tile[:, OFF_U:OFF_U + D_POOL])
    woutb_ref[...] = _pack_rows(wout_ref[...].astype(BF16))
    proj_ref[...] += _dot(hb_ref[j], tile_main)
    projs_ref[...] += _dot(hs_ref[j], tile)

    @pl.when(j == pl.num_programs(0) - 1)
    def _():
        cosf, sinf = cos_ref[...], sin_ref[...]
        row = lax.broadcasted_iota(jnp.int32, (N_META, HEAD_DIM), 0).astype(F32)
        kd_ref[...] = jnp.zeros_like(kd_ref)
        v_ref[...] = jnp.zeros_like(v_ref)
        u_ref[...] = proj_ref[:, OFF_U:OFF_U + D_POOL]
        for h in range(N_HEADS):
            kr = _rotary(proj_ref[:, OFF_K + h * HEAD_DIM:OFF_K + (h + 1) * HEAD_DIM], cosf, sinf)
            kd_ref[0:N_META, _head(h)] = kr * K_SCALE * jnp.exp(LOG_DECAY[h] * (N_META - 1.0 - row))
            v_ref[0:N_META, _head(h)] = proj_ref[:, OFF_V + h * HEAD_DIM:OFF_V + (h + 1) * HEAD_DIM]
        for h in range(N_HEADS):
            s_ref[h] = _dot_t(kd_ref[:, _head(h)].astype(BF16), v_ref[:, _head(h)].astype(BF16))

        coss, sins = coss_ref[...], sins_ref[...]
        rest_ref[:, 0:D_POOL] = projs_ref[:, OFF_U:OFF_U + D_POOL]
        rest_ref[:, D_POOL:2 * D_POOL] = projs_ref[:, OFF_GP:OFF_GP + D_POOL]
        rest_ref[:, 2 * D_POOL:2 * D_POOL + D_RET] = projs_ref[:, OFF_GR:OFF_GR + D_RET]
        for h in range(N_HEADS):
            qr = _rotary(projs_ref[:, OFF_Q + h * HEAD_DIM:OFF_Q + (h + 1) * HEAD_DIM], coss, sins)
            kr = _rotary(projs_ref[:, OFF_K + h * HEAD_DIM:OFF_K + (h + 1) * HEAD_DIM], coss, sins)
            kr = kr * K_SCALE
            v = projs_ref[:, OFF_V + h * HEAD_DIM:OFF_V + (h + 1) * HEAD_DIM]
            stage_ref[:, _head(h)] = qr * math.exp(LOG_DECAY[h])
            stage_ref[:, D_RET + h * HEAD_DIM:D_RET + (h + 1) * HEAD_DIM] = kr
            stage_ref[:, 2 * D_RET + h * HEAD_DIM:2 * D_RET + (h + 1) * HEAD_DIM] = v
            rest_ref[:, 2 * D_POOL + D_RET + h * HEAD_DIM:2 * D_POOL + D_RET + (h + 1) * HEAD_DIM] = (
                jnp.sum(qr * kr, axis=-1, keepdims=True) * v)
        n_steps, per = qkv_ref.shape[0], stage_ref.shape[0] // qkv_ref.shape[0]
        qkv_ref[...] = jnp.zeros_like(qkv_ref)
        for i in range(n_steps):
            qkv_ref[i, 0:per, :] = stage_ref[i * per:(i + 1) * per, :]


def _prep_call(meta, xs, cosf, sinf, coss, sins, normw, w_in, w_out, w_pool, n_steps):
    n = xs.shape[0]

    def const(shape):
        zeros = (0,) * len(shape)
        return pl.BlockSpec(shape, lambda j: zeros)

    row_tile = lambda j: (j, 0)
    return pl.pallas_call(
        _prep_kernel,
        grid=(PREP_TILES,),
        in_specs=[
            const((N_META, D_MODEL)),
            const((n, 1, D_MODEL)),
            const((N_META, HEAD_DIM)),
            const((N_META, HEAD_DIM)),
            const((1, HEAD_DIM)),
            const((1, HEAD_DIM)),
            const((1, D_MODEL)),
            pl.BlockSpec((D_MODEL // PREP_TILES, D_IN_PROJ), row_tile),
            pl.BlockSpec((D_MIX // PREP_TILES, D_MODEL), row_tile),
            const((len(POOL_WINDOWS), POOL_GROUP, POOL_GROUP)),
        ],
        out_specs=[
            pl.BlockSpec((D_MODEL // PREP_TILES // 2, D_IN_PROJ), row_tile),
            pl.BlockSpec((D_MODEL // PREP_TILES // 2, D_POOL), row_tile),
            pl.BlockSpec((D_MIX // PREP_TILES // 2, D_MODEL), row_tile),
            const((len(POOL_WINDOWS), POOL_GROUP // 2, POOL_GROUP)),
            const((N_HEADS, HEAD_DIM, HEAD_DIM)),
            const((N_META, D_POOL)),
            const((n_steps, SUBLANES, 3 * D_RET)),
            const((n, D_REST)),
        ],
        out_shape=(jax.ShapeDtypeStruct((D_MODEL // 2, D_IN_PROJ), PACKED),
                   jax.ShapeDtypeStruct((D_MODEL // 2, D_POOL), PACKED),
                   jax.ShapeDtypeStruct((D_MIX // 2, D_MODEL), PACKED),
                   jax.ShapeDtypeStruct((len(POOL_WINDOWS), POOL_GROUP // 2, POOL_GROUP), PACKED),
                   jax.ShapeDtypeStruct((N_HEADS, HEAD_DIM, HEAD_DIM), F32),
                   jax.ShapeDtypeStruct((N_META, D_POOL), F32),
                   jax.ShapeDtypeStruct((n_steps, SUBLANES, 3 * D_RET), F32),
                   jax.ShapeDtypeStruct((n, D_REST), F32)),
        scratch_shapes=[pltpu.VMEM((PREP_TILES, N_META, D_MODEL // PREP_TILES), BF16),
                        pltpu.VMEM((PREP_TILES, n, D_MODEL // PREP_TILES), BF16),
                        pltpu.VMEM((N_META, D_IN_PROJ), F32),
                        pltpu.VMEM((n, D_IN_PROJ), F32),
                        pltpu.VMEM((CHUNK, D_RET), F32),
                        pltpu.VMEM((CHUNK, D_RET), F32),
                        pltpu.VMEM((n, 3 * D_RET), F32),
                        pltpu.VMEM((len(POOL_WINDOWS), POOL_GROUP, POOL_GROUP), BF16),
                        pltpu.VMEM((len(POOL_WINDOWS), POOL_GROUP, POOL_GROUP), BF16)],
        compiler_params=pltpu.CompilerParams(
            dimension_semantics=("arbitrary",), vmem_limit_bytes=VMEM_LIMIT_BYTES),
        name="prep",
    )(meta, xs, cosf, sinf, coss, sins, normw, w_in, w_out, w_pool)


def _sample_state_update(qkv_ref, state_ref, sout_ref, o_ref, heads):
    per = state_ref.shape[0]
    seq_of_row = lax.broadcasted_iota(jnp.int32, (SUBLANES, per * HEAD_DIM), 0)
    seq_of_col = lax.broadcasted_iota(jnp.int32, (SUBLANES, per * HEAD_DIM), 1) // HEAD_DIM
    own_block = seq_of_row == seq_of_col
    for h in heads:
        g1 = math.exp(LOG_DECAY[h])
        qd = qkv_ref[:, _head(h)]
        kt = qkv_ref[:, D_RET + h * HEAD_DIM:D_RET + (h + 1) * HEAD_DIM].T
        v8 = qkv_ref[:, 2 * D_RET + h * HEAD_DIM:2 * D_RET + (h + 1) * HEAD_DIM]
        state = state_ref[:, h]
        q_blocks = jnp.where(own_block, jnp.concatenate([qd] * per, axis=-1), 0.0)
        o_ref[:, _head(h)] = _dot(q_blocks.astype(BF16),
                                  state.reshape(per * HEAD_DIM, HEAD_DIM).astype(BF16))
        for j in range(per):
            sout_ref[j, h] = g1 * state[j] + kt[:, j:j + 1] * v8[j:j + 1, :]


def _sample_pool_update(us_ref, pin_ref, pout_ref, pooled_ref):
    u = us_ref[...]
    pout_ref[0:POOL_BUF - 1] = pin_ref[1:POOL_BUF]
    pout_ref[POOL_BUF - 1] = u
    for g, w in enumerate(POOL_WINDOWS):
        cols = slice(g * POOL_GROUP, (g + 1) * POOL_GROUP)
        win_sum = u[:, cols]
        for r in range(POOL_BUF - (w - 1), POOL_BUF):
            win_sum = win_sum + pin_ref[r, :, cols]
        pooled_ref[:, cols] = win_sum / float(w) - u[:, cols]


def _main_kernel(x_ref, cos_ref, sin_ref, smeta_ref, umeta_ref, normw_ref, pscale_ref, rnw_ref,
                 fnw_ref, win_ref, wu_ref, wout_ref, qkv_ref, sstate_ref, us_ref, pin_ref,
                 rest_ref, wpool_ref,
                 y_ref, s_ref, pbuf_ref, sout_ref, pout_ref, ys_ref,
                 hist_ref, so_ref, pooled_ref, o_ref):
    c = pl.program_id(1)
    nb = x_ref.shape[0]
    batches = range(nb)

    @pl.when(c == 0)
    def _():
        for b in batches:
            s_ref[b] = smeta_ref[...]
            hist_ref[b] = umeta_ref[...]

    hb = _rms(x_ref[...].reshape(nb * CHUNK, D_MODEL), normw_ref[...]).astype(BF16)

    def proj(off):
        return _dot(hb, _unpack_rows(win_ref[:, off:off + TILE_N]))

    def rows(t, b):
        return t[b * CHUNK:(b + 1) * CHUNK]

    cosf, sinf = cos_ref[...], sin_ref[...]
    li = lax.broadcasted_iota(jnp.int32, (CHUNK, CHUNK), 0).astype(F32)
    mi = lax.broadcasted_iota(jnp.int32, (CHUNK, CHUNK), 1).astype(F32)
    diff = li - mi

    qkv, part, outs = {}, {}, {}

    def issue_qkv(pair):
        qkv[pair] = tuple(proj(off + pair * TILE_N) for off in (OFF_Q, OFF_K, OFF_V))

    def stage_a(pair):
        q2, k2, v2 = qkv.pop(pair)
        for i in range(2):
            h = 2 * pair + i
            lg = LOG_DECAY[h]
            sub = slice(i * HEAD_DIM, (i + 1) * HEAD_DIM)
            dmask = jnp.where(diff >= 0.0, jnp.exp(lg * jnp.maximum(diff, 0.0)), 0.0)
            q_decay = jnp.exp(lg * (li + 1.0))
            k_decay = jnp.exp(lg * (CHUNK - 1.0 - li))
            for b in batches:
                qr = _rotary(rows(q2, b)[:, sub], cosf, sinf)
                kr = _rotary(rows(k2, b)[:, sub], cosf, sinf) * K_SCALE
                vb = rows(v2, b)[:, sub].astype(BF16)
                state = s_ref[b, h]
                scores = lax.dot_general(qr.astype(BF16), kr.astype(BF16), (((1,), (1,)), ((), ())),
                                         preferred_element_type=F32)
                cross = _dot((qr * q_decay).astype(BF16), state.astype(BF16))
                s_ref[b, h] = math.exp(lg * CHUNK) * state + _dot_t((kr * k_decay).astype(BF16), vb)
                part[b, h] = ((scores * dmask).astype(BF16), vb, cross)

    def stage_b(pair):
        for i in range(2):
            h = 2 * pair + i
            for b in batches:
                p, vb, cross = part.pop((b, h))
                outs[b, h] = _dot(p, vb) + cross

    def pool_windows(u_tiles):
        pooled = []
        for g, w in enumerate(POOL_WINDOWS):
            cols = slice(g * POOL_GROUP, (g + 1) * POOL_GROUP)
            per_batch = []
            for b in batches:
                u = rows(u_tiles[g], b)
                ext = jnp.concatenate([hist_ref[b, :, cols], u], axis=0)
                win_sum = ext
                shift = 1
                while shift < w:
                    win_sum = win_sum + pltpu.roll(win_sum, shift, 0)
                    shift *= 2
                per_batch.append(win_sum[HIST:] / float(w) - u)
                hist_ref[b, :, cols] = u[CHUNK - HIST:]
            pooled.append(jnp.concatenate(per_batch, axis=0))
        return pooled

    step = pl.program_id(0) * pl.num_programs(1) + c
    so_step = so_ref.at[step]
    _sample_state_update(qkv_ref, sstate_ref, sout_ref, so_step, range(0, SAMPLE_HEADS_FIRST))
    block_rows = pl.ds(pl.multiple_of(step * sstate_ref.shape[0] // SUBLANES * SUBLANES, SUBLANES),
                       SUBLANES)
    _sample_pool_update(us_ref, pin_ref, pout_ref, pooled_ref.at[block_rows])
    issue_qkv(0)
    issue_qkv(1)
    stage_a(0)
    issue_qkv(2)
    stage_a(1)
    stage_b(0)
    issue_qkv(3)
    u_tiles = [proj(OFF_U + g * POOL_GROUP) for g in range(len(POOL_WINDOWS))]
    stage_a(2)
    stage_b(1)
    gp_tiles = [proj(OFF_GP + g * POOL_GROUP) for g in range(len(POOL_WINDOWS))]
    mixed = pool_windows(u_tiles)
    stage_a(3)
    stage_b(2)
    gr_tiles = [proj(OFF_GR + pair * TILE_N) for pair in range(N_HEADS // 2)]
    stage_b(3)
    pool_mix = jnp.concatenate(
        [(mixed[g] * pscale_ref[:, g * POOL_GROUP:(g + 1) * POOL_GROUP] * _silu(gp_tiles[g])).astype(BF16)
         for g in range(len(POOL_WINDOWS))], axis=-1)

    per_piece = max(nb // OUT_PIECES, 1)
    for first in range(0, nb, per_piece):
        piece = range(first, first + per_piece)
        ret_y = []
        for b in piece:
            heads = []
            for h in range(N_HEADS):
                sub = slice((h % 2) * HEAD_DIM, (h % 2 + 1) * HEAD_DIM)
                o = outs.pop((b, h))
                rn = o * lax.rsqrt(jnp.mean(o * o, axis=-1, keepdims=True) + EPS)
                gate = _silu(rows(gr_tiles[h // 2], b)[:, sub])
                heads.append((rn * rnw_ref[:, _head(h)] * gate).astype(BF16))
            ret_y.append(jnp.concatenate(heads, axis=-1))
        mix = jnp.concatenate([pool_mix[first * CHUNK:(first + per_piece) * CHUNK],
                               jnp.concatenate(ret_y, axis=0)], axis=-1)
        x_piece = x_ref[first:first + per_piece].reshape(per_piece * CHUNK, D_MODEL)
        acc_piece = x_piece + _dot(mix, _unpack_rows(wout_ref[...]))
        y_ref[first:first + per_piece] = _rms(acc_piece, fnw_ref[...]).reshape(
            per_piece, CHUNK, D_MODEL)

    _sample_state_update(qkv_ref, sstate_ref, sout_ref, so_step, range(SAMPLE_HEADS_FIRST, N_HEADS))

    @pl.when(step == pl.num_programs(0) * pl.num_programs(1) - 1)
    def _():
        _sample_tail(rest_ref, so_ref, pooled_ref, pscale_ref, rnw_ref, fnw_ref, wpool_ref,
                     wout_ref, ys_ref, o_ref)

    @pl.when(c == pl.num_programs(1) - 1)
    def _():
        last = jnp.concatenate([rows(hb, b)[CHUNK - HIST:] for b in batches], axis=0)
        u_last = _dot(last, _unpack_rows(wu_ref[...]))
        for group in range(pbuf_ref.shape[1] // nb):
            @pl.when(pl.program_id(0) == group)
            def _():
                for b in batches:
                    seq = group * nb + b
                    for r in range(HIST - POOL_BUF, HIST):
                        pbuf_ref[r - (HIST - POOL_BUF), seq:seq + 1, :] = (
                            u_last[b * HIST + r:b * HIST + r + 1, :])


def _main_call(x, cosf, sinf, smeta, umeta, normw, pscale, rnw, fnw, win_bf, wu_bf, wout_bf,
               qkv_s, state_s, rest_s, pool_t, wpool_bf):
    batch, seq, _ = x.shape
    nb = PROMPT_BLOCK
    n_chunks = seq // CHUNK
    n_sample = state_s.shape[0]
    per = n_sample // ((batch // nb) * n_chunks)

    def const(shape):
        zeros = (0,) * len(shape)
        return pl.BlockSpec(shape, lambda b, c: zeros, pipeline_mode=pl.Buffered(1))

    step = lambda b, c: b * n_chunks + c
    pool_block = lambda b, c: step(b, c) * per // SUBLANES
    return pl.pallas_call(
        _main_kernel,
        grid=(batch // nb, n_chunks),
        in_specs=[
            pl.BlockSpec((nb, CHUNK, D_MODEL), lambda b, c: (b, c, 0)),
            pl.BlockSpec((CHUNK, HEAD_DIM), lambda b, c: (c, 0)),
            pl.BlockSpec((CHUNK, HEAD_DIM), lambda b, c: (c, 0)),
            const((N_HEADS, HEAD_DIM, HEAD_DIM)),
            const((N_META, D_POOL)),
            const((1, D_MODEL)),
            const((1, D_POOL)),
            const((1, D_RET)),
            const((1, D_MODEL)),
            const((D_MODEL // 2, D_IN_PROJ)),
            const((D_MODEL // 2, D_POOL)),
            const((D_MIX // 2, D_MODEL)),
            pl.BlockSpec((None, SUBLANES, 3 * D_RET), lambda b, c: (step(b, c), 0, 0)),
            pl.BlockSpec((per, N_HEADS, HEAD_DIM, HEAD_DIM), lambda b, c: (step(b, c), 0, 0, 0)),
            pl.BlockSpec((SUBLANES, D_POOL), lambda b, c: (pool_block(b, c), 0)),
            pl.BlockSpec((POOL_BUF, SUBLANES, D_POOL), lambda b, c: (0, pool_block(b, c), 0)),
            const((n_sample, D_REST)),
            const((len(POOL_WINDOWS), POOL_GROUP // 2, POOL_GROUP)),
        ],
        out_specs=[
            pl.BlockSpec((nb, CHUNK, D_MODEL), lambda b, c: (b, c, 0)),
            pl.BlockSpec((nb, N_HEADS, HEAD_DIM, HEAD_DIM), lambda b, c: (b, 0, 0, 0)),
            pl.BlockSpec((POOL_BUF, batch, D_POOL), lambda b, c: (0, 0, 0)),
            pl.BlockSpec((per, N_HEADS, HEAD_DIM, HEAD_DIM), lambda b, c: (step(b, c), 0, 0, 0)),
            pl.BlockSpec((POOL_BUF, SUBLANES, D_POOL), lambda b, c: (0, pool_block(b, c), 0)),
            pl.BlockSpec((n_sample, 1, D_MODEL), lambda b, c: (0, 0, 0)),
        ],
        out_shape=(jax.ShapeDtypeStruct((batch, seq, D_MODEL), F32),
                   jax.ShapeDtypeStruct((batch, N_HEADS, HEAD_DIM, HEAD_DIM), F32),
                   jax.ShapeDtypeStruct((POOL_BUF, batch, D_POOL), F32),
                   jax.ShapeDtypeStruct((n_sample, N_HEADS, HEAD_DIM, HEAD_DIM), F32),
                   jax.ShapeDtypeStruct((POOL_BUF, n_sample, D_POOL), F32),
                   jax.ShapeDtypeStruct((n_sample, 1, D_MODEL), F32)),
        scratch_shapes=[pltpu.VMEM((nb, HIST, D_POOL), F32),
                        pltpu.VMEM((qkv_s.shape[0], SUBLANES, D_RET), F32),
                        pltpu.VMEM((n_sample, D_POOL), F32),
                        pltpu.VMEM((n_sample, D_RET), F32)],
        compiler_params=pltpu.CompilerParams(
            dimension_semantics=("arbitrary", "arbitrary"), vmem_limit_bytes=VMEM_LIMIT_BYTES),
        name="main",
    )(x, cosf, sinf, smeta, umeta, normw, pscale, rnw, fnw, win_bf, wu_bf, wout_bf,
      qkv_s, state_s, rest_s, pool_t, rest_s, wpool_bf)


def _sample_tail(rest_ref, so_ref, pooled_ref, pscale_ref, rnw_ref, fnw_ref, wpool_ref,
                 wout_ref, y_ref, o_ref):
    n_steps = so_ref.shape[0]
    per = rest_ref.shape[0] // n_steps
    for i in range(n_steps):
        o_ref[i * per:(i + 1) * per, :] = so_ref[i, 0:per, :]

    parts = []
    for g in range(len(POOL_WINDOWS)):
        cols = slice(g * POOL_GROUP, (g + 1) * POOL_GROUP)
        mixed = _dot(pooled_ref[:, cols].astype(BF16), _unpack_rows(wpool_ref[g])) * pscale_ref[:, cols]
        gate = _silu(rest_ref[:, D_POOL + g * POOL_GROUP:D_POOL + (g + 1) * POOL_GROUP])
        parts.append((mixed * gate).astype(BF16))
    for h in range(N_HEADS):
        o = o_ref[:, _head(h)] + rest_ref[:, 2 * D_POOL + D_RET + h * HEAD_DIM:
                                          2 * D_POOL + D_RET + (h + 1) * HEAD_DIM]
        rn = o * lax.rsqrt(jnp.mean(o * o, axis=-1, keepdims=True) + EPS)
        gr = rest_ref[:, 2 * D_POOL + h * HEAD_DIM:2 * D_POOL + (h + 1) * HEAD_DIM]
        parts.append((rn * rnw_ref[:, _head(h)] * _silu(gr)).astype(BF16))
    mix = jnp.concatenate(parts, axis=-1)
    x = rest_ref[:, OFF_REST_X:OFF_REST_X + D_MODEL]
    y_ref[:, 0, :] = _rms(x + _dot(mix, _unpack_rows(wout_ref[...])), fnw_ref[...])


def _rotary_tables(pos):
    half = HEAD_DIM // 2
    inv = ROPE_BASE ** (-np.arange(half, dtype=np.float64) / half)
    ang = np.asarray(pos, np.float64)[:, None] * inv[None, :]
    cos, sin = np.cos(ang), np.sin(ang)
    return (jnp.asarray(np.concatenate([cos, cos], axis=-1), F32),
            jnp.asarray(np.concatenate([-sin, sin], axis=-1), F32))


def kernel(x_prompt, x_sample, state_ret, state_pool, meta_tokens, norm_w, w_in, w_pool,
           pool_scale, ret_norm_w, w_out, final_norm_w):
    assert norm_w.shape[0] == 1, "single-layer stack"
    batch, seq, _ = x_prompt.shape
    n_sample = x_sample.shape[0]
    n_steps = (batch // PROMPT_BLOCK) * (seq // CHUNK)
    per = n_sample // n_steps
    assert per * n_steps == n_sample and SUBLANES % per == 0
    normw, pscale, rnw = norm_w, pool_scale, ret_norm_w
    fnw = final_norm_w[None, :]

    cos_p, sin_p = _rotary_tables(np.arange(N_META + seq))
    cos_s, sin_s = _rotary_tables(PAST_LEN + np.arange(1))

    win_bf, wu_bf, wout_bf, wpool_bf, smeta, umeta, qkv_s, rest_s = _prep_call(
        meta_tokens.astype(x_prompt.dtype), x_sample, cos_p[:N_META], sin_p[:N_META], cos_s, sin_s,
        normw, w_in[0], w_out[0], w_pool[0], n_steps)
    y_p, s_p, buf_p, s_s, buf_s, y_s = _main_call(
        x_prompt, cos_p[N_META:], sin_p[N_META:], smeta, umeta, normw, pscale, rnw, fnw,
        win_bf, wu_bf, wout_bf, qkv_s, state_ret[0], rest_s, jnp.transpose(state_pool[0], (1, 0, 2)),
        wpool_bf)
    return (y_p, y_s, s_p[None], s_s[None], jnp.transpose(buf_p, (1, 0, 2))[None],
            jnp.transpose(buf_s, (1, 0, 2))[None])
```

```python
import math

import jax
import jax.numpy as jnp
import numpy as np
from jax import lax
from jax.experimental import pallas as pl
from jax.experimental.pallas import tpu as pltpu

D_MODEL = 1024
D_POOL = 1024
D_RET = 1024
D_MIX = D_POOL + D_RET
POOL_WINDOWS = (2, 4, 8, 16)
POOL_GROUP = D_POOL // len(POOL_WINDOWS)
POOL_BUF = max(POOL_WINDOWS) - 1
N_HEADS = 8
HEAD_DIM = D_RET // N_HEADS
D_IN_PROJ = 2 * D_POOL + 4 * D_RET
N_META = 16
PAST_LEN = 16384
CHUNK = 128
ROPE_BASE = 10000.0
EPS = 1e-6
K_SCALE = HEAD_DIM ** -0.5

OFF_U, OFF_GP, OFF_Q, OFF_K, OFF_V, OFF_GR = (i * 1024 for i in range(6))

OFF_REST_X = 2 * D_POOL + 2 * D_RET
D_REST = OFF_REST_X + D_MODEL

LOG_DECAY = tuple(math.log(1.0 - 2.0 ** (-5.0 - h)) for h in range(N_HEADS))

SUBLANES = 8
TILE_N = 256
PROMPT_BLOCK = 4
OUT_PIECES = 2
SAMPLE_HEADS_FIRST = 4
PREP_TILES = 4
HIST = 16

VMEM_LIMIT_BYTES = 60 * 1024 * 1024

F32 = jnp.float32
BF16 = jnp.bfloat16
PACKED = jnp.uint32


def _rms(x, w):
    return x * lax.rsqrt(jnp.mean(x * x, axis=-1, keepdims=True) + EPS) * w


def _silu(x):
    return x * (1.0 / (1.0 + jnp.exp(-x)))


def _dot(a, b):
    return jnp.dot(a, b, preferred_element_type=F32)


def _dot_t(a, b):
    return lax.dot_general(a, b, (((0,), (0,)), ((), ())), preferred_element_type=F32)


def _pack_rows(w):
    return pltpu.bitcast(w, PACKED)


def _unpack_rows(w):
    return pltpu.bitcast(w, BF16)


def _rotary(x, cosf, sinf):
    return x * cosf + pltpu.roll(x, HEAD_DIM // 2, 1) * sinf


def _head(h):
    return slice(h * HEAD_DIM, (h + 1) * HEAD_DIM)


def _prep_kernel(meta_ref, xs_ref, cos_ref, sin_ref, coss_ref, sins_ref, normw_ref,
                 win_ref, wout_ref, wpool_ref,
                 winb_ref, wub_ref, woutb_ref, wpoolb_ref, s_ref, u_ref, qkv_ref, rest_ref,
                 hb_ref, hs_ref, proj_ref, projs_ref, kd_ref, v_ref, stage_ref, wphi_ref, wplo_ref):
    j = pl.program_id(0)
    k_rows = D_MODEL // PREP_TILES

    @pl.when(j == 0)
    def _():
        hb = _rms(meta_ref[...], normw_ref[...]).astype(BF16)
        xs = xs_ref[:, 0, :]
        rest_ref[:, OFF_REST_X:OFF_REST_X + D_MODEL] = xs
        hs = _rms(xs, normw_ref[...]).astype(BF16)
        for t in range(PREP_TILES):
            hb_ref[t] = hb[:, t * k_rows:(t + 1) * k_rows]
            hs_ref[t] = hs[:, t * k_rows:(t + 1) * k_rows]
        wp = wpool_ref[...]
        wp_hi = wp.astype(BF16)
        wpoolb_ref[...] = _pack_rows(wp_hi)
        wphi_ref[...] = wp_hi
        wplo_ref[...] = (wp - wp_hi.astype(F32)).astype(BF16)
        proj_ref[...] = jnp.zeros_like(proj_ref)
        projs_ref[...] = jnp.zeros_like(projs_ref)

    tile = win_ref[...].astype(BF16)
    folded = []
    for g in range(len(POOL_WINDOWS)):
        cols = slice(OFF_U + g * POOL_GROUP, OFF_U + (g + 1) * POOL_GROUP)
        a_hi = tile[:, cols]
        a_lo = (win_ref[:, cols] - a_hi.astype(F32)).astype(BF16)
        folded.append((_dot(a_hi, wphi_ref[g]) + _dot(a_hi, wplo_ref[g])
                       + _dot(a_lo, wphi_ref[g])).astype(BF16))
    tile_main = jnp.concatenate(folded + [tile[:, OFF_U + D_POOL:]], axis=-1)
    winb_ref[...] = _pack_rows(tile_main)
    wub_ref[...] = _pack_rows(tile[:, OFF_U:OFF_U + D_POOL])
    woutb_ref[...] = _pack_rows(wout_ref[...].astype(BF16))
    proj_ref[...] += _dot(hb_ref[j], tile_main)
    projs_ref[...] += _dot(hs_ref[j], tile)

    @pl.when(j == pl.num_programs(0) - 1)
    def _():
        cosf, sinf = cos_ref[...], sin_ref[...]
        row = lax.broadcasted_iota(jnp.int32, (N_META, HEAD_DIM), 0).astype(F32)
        kd_ref[...] = jnp.zeros_like(kd_ref)
        v_ref[...] = jnp.zeros_like(v_ref)
        u_ref[...] = proj_ref[:, OFF_U:OFF_U + D_POOL]
        for h in range(N_HEADS):
            kr = _rotary(proj_ref[:, OFF_K + h * HEAD_DIM:OFF_K + (h + 1) * HEAD_DIM], cosf, sinf)
            kd_ref[0:N_META, _head(h)] = kr * K_SCALE * jnp.exp(LOG_DECAY[h] * (N_META - 1.0 - row))
            v_ref[0:N_META, _head(h)] = proj_ref[:, OFF_V + h * HEAD_DIM:OFF_V + (h + 1) * HEAD_DIM]
        for h in range(N_HEADS):
            s_ref[h] = _dot_t(kd_ref[:, _head(h)].astype(BF16), v_ref[:, _head(h)].astype(BF16))

        coss, sins = coss_ref[...], sins_ref[...]
        rest_ref[:, 0:D_POOL] = projs_ref[:, OFF_U:OFF_U + D_POOL]
        rest_ref[:, D_POOL:2 * D_POOL] = projs_ref[:, OFF_GP:OFF_GP + D_POOL]
        rest_ref[:, 2 * D_POOL:2 * D_POOL + D_RET] = projs_ref[:, OFF_GR:OFF_GR + D_RET]
        for h in range(N_HEADS):
            qr = _rotary(projs_ref[:, OFF_Q + h * HEAD_DIM:OFF_Q + (h + 1) * HEAD_DIM], coss, sins)
            kr = _rotary(projs_ref[:, OFF_K + h * HEAD_DIM:OFF_K + (h + 1) * HEAD_DIM], coss, sins)
            kr = kr * K_SCALE
            v = projs_ref[:, OFF_V + h * HEAD_DIM:OFF_V + (h + 1) * HEAD_DIM]
            stage_ref[:, _head(h)] = qr * math.exp(LOG_DECAY[h])
            stage_ref[:, D_RET + h * HEAD_DIM:D_RET + (h + 1) * HEAD_DIM] = kr
            stage_ref[:, 2 * D_RET + h * HEAD_DIM:2 * D_RET + (h + 1) * HEAD_DIM] = v
            rest_ref[:, 2 * D_POOL + D_RET + h * HEAD_DIM:2 * D_POOL + D_RET + (h + 1) * HEAD_DIM] = (
                jnp.sum(qr * kr, axis=-1, keepdims=True) * v)
        n_steps, per = qkv_ref.shape[0], stage_ref.shape[0] // qkv_ref.shape[0]
        qkv_ref[...] = jnp.zeros_like(qkv_ref)
        for i in range(n_steps):
            qkv_ref[i, 0:per, :] = stage_ref[i * per:(i + 1) * per, :]


def _prep_call(meta, xs, cosf, sinf, coss, sins, normw, w_in, w_out, w_pool, n_steps):
    n = xs.shape[0]

    def const(shape):
        zeros = (0,) * len(shape)
        return pl.BlockSpec(shape, lambda j: zeros)

    row_tile = lambda j: (j, 0)
    return pl.pallas_call(
        _prep_kernel,
        grid=(PREP_TILES,),
        in_specs=[
            const((N_META, D_MODEL)),
            const((n, 1, D_MODEL)),
            const((N_META, HEAD_DIM)),
            const((N_META, HEAD_DIM)),
            const((1, HEAD_DIM)),
            const((1, HEAD_DIM)),
            const((1, D_MODEL)),
            pl.BlockSpec((D_MODEL // PREP_TILES, D_IN_PROJ), row_tile),
            pl.BlockSpec((D_MIX // PREP_TILES, D_MODEL), row_tile),
            const((len(POOL_WINDOWS), POOL_GROUP, POOL_GROUP)),
        ],
        out_specs=[
            pl.BlockSpec((D_MODEL // PREP_TILES // 2, D_IN_PROJ), row_tile),
            pl.BlockSpec((D_MODEL // PREP_TILES // 2, D_POOL), row_tile),
            pl.BlockSpec((D_MIX // PREP_TILES // 2, D_MODEL), row_tile),
            const((len(POOL_WINDOWS), POOL_GROUP // 2, POOL_GROUP)),
            const((N_HEADS, HEAD_DIM, HEAD_DIM)),
            const((N_META, D_POOL)),
            const((n_steps, SUBLANES, 3 * D_RET)),
            const((n, D_REST)),
        ],
        out_shape=(jax.ShapeDtypeStruct((D_MODEL // 2, D_IN_PROJ), PACKED),
                   jax.ShapeDtypeStruct((D_MODEL // 2, D_POOL), PACKED),
                   jax.ShapeDtypeStruct((D_MIX // 2, D_MODEL), PACKED),
                   jax.ShapeDtypeStruct((len(POOL_WINDOWS), POOL_GROUP // 2, POOL_GROUP), PACKED),
                   jax.ShapeDtypeStruct((N_HEADS, HEAD_DIM, HEAD_DIM), F32),
                   jax.ShapeDtypeStruct((N_META, D_POOL), F32),
                   jax.ShapeDtypeStruct((n_steps, SUBLANES, 3 * D_RET), F32),
                   jax.ShapeDtypeStruct((n, D_REST), F32)),
        scratch_shapes=[pltpu.VMEM((PREP_TILES, N_META, D_MODEL // PREP_TILES), BF16),
                        pltpu.VMEM((PREP_TILES, n, D_MODEL // PREP_TILES), BF16),
                        pltpu.VMEM((N_META, D_IN_PROJ), F32),
                        pltpu.VMEM((n, D_IN_PROJ), F32),
                        pltpu.VMEM((CHUNK, D_RET), F32),
                        pltpu.VMEM((CHUNK, D_RET), F32),
                        pltpu.VMEM((n, 3 * D_RET), F32),
                        pltpu.VMEM((len(POOL_WINDOWS), POOL_GROUP, POOL_GROUP), BF16),
                        pltpu.VMEM((len(POOL_WINDOWS), POOL_GROUP, POOL_GROUP), BF16)],
        compiler_params=pltpu.CompilerParams(
            dimension_semantics=("arbitrary",), vmem_limit_bytes=VMEM_LIMIT_BYTES),
        name="prep",
    )(meta, xs, cosf, sinf, coss, sins, normw, w_in, w_out, w_pool)


def _sample_state_update(qkv_ref, state_ref, sout_ref, o_ref, heads):
    per = state_ref.shape[0]
    seq_of_row = lax.broadcasted_iota(jnp.int32, (SUBLANES, per * HEAD_DIM), 0)
    seq_of_col = lax.broadcasted_iota(jnp.int32, (SUBLANES, per * HEAD_DIM), 1) // HEAD_DIM
    own_block = seq_of_row == seq_of_col
    for h in heads:
        g1 = math.exp(LOG_DECAY[h])
        qd = qkv_ref[:, _head(h)]
        kt = qkv_ref[:, D_RET + h * HEAD_DIM:D_RET + (h + 1) * HEAD_DIM].T
        v8 = qkv_ref[:, 2 * D_RET + h * HEAD_DIM:2 * D_RET + (h + 1) * HEAD_DIM]
        state = state_ref[:, h]
        q_blocks = jnp.where(own_block, jnp.concatenate([qd] * per, axis=-1), 0.0)
        o_ref[:, _head(h)] = _dot(q_blocks.astype(BF16),
                                  state.reshape(per * HEAD_DIM, HEAD_DIM).astype(BF16))
        for j in range(per):
            sout_ref[j, h] = g1 * state[j] + kt[:, j:j + 1] * v8[j:j + 1, :]


def _sample_pool_update(us_ref, pin_ref, pout_ref, pooled_ref):
    u = us_ref[...]
    pout_ref[0:POOL_BUF - 1] = pin_ref[1:POOL_BUF]
    pout_ref[POOL_BUF - 1] = u
    for g, w in enumerate(POOL_WINDOWS):
        cols = slice(g * POOL_GROUP, (g + 1) * POOL_GROUP)
        win_sum = u[:, cols]
        for r in range(POOL_BUF - (w - 1), POOL_BUF):
            win_sum = win_sum + pin_ref[r, :, cols]
        pooled_ref[:, cols] = win_sum / float(w) - u[:, cols]


def _main_kernel(x_ref, cos_ref, sin_ref, smeta_ref, umeta_ref, normw_ref, pscale_ref, rnw_ref,
                 fnw_ref, win_ref, wu_ref, wout_ref, qkv_ref, sstate_ref, us_ref, pin_ref,
                 rest_ref, wpool_ref,
                 y_ref, s_ref, pbuf_ref, sout_ref, pout_ref, ys_ref,
                 hist_ref, so_ref, pooled_ref, o_ref):
    c = pl.program_id(1)
    nb = x_ref.shape[0]
    batches = range(nb)

    @pl.when(c == 0)
    def _():
        for b in batches:
            s_ref[b] = smeta_ref[...]
            hist_ref[b] = umeta_ref[...]

    hb = _rms(x_ref[...].reshape(nb * CHUNK, D_MODEL), normw_ref[...]).astype(BF16)

    def proj(off):
        return _dot(hb, _unpack_rows(win_ref[:, off:off + TILE_N]))

    def rows(t, b):
        return t[b * CHUNK:(b + 1) * CHUNK]

    cosf, sinf = cos_ref[...], sin_ref[...]
    li = lax.broadcasted_iota(jnp.int32, (CHUNK, CHUNK), 0).astype(F32)
    mi = lax.broadcasted_iota(jnp.int32, (CHUNK, CHUNK), 1).astype(F32)
    diff = li - mi

    qkv, part, outs = {}, {}, {}

    def issue_qkv(pair):
        qkv[pair] = tuple(proj(off + pair * TILE_N) for off in (OFF_Q, OFF_K, OFF_V))

    def stage_a(pair):
        q2, k2, v2 = qkv.pop(pair)
        for i in range(2):
            h = 2 * pair + i
            lg = LOG_DECAY[h]
            sub = slice(i * HEAD_DIM, (i + 1) * HEAD_DIM)
            dmask = jnp.where(diff >= 0.0, jnp.exp(lg * jnp.maximum(diff, 0.0)), 0.0)
            q_decay = jnp.exp(lg * (li + 1.0))
            k_decay = jnp.exp(lg * (CHUNK - 1.0 - li))
            for b in batches:
                qr = _rotary(rows(q2, b)[:, sub], cosf, sinf)
                kr = _rotary(rows(k2, b)[:, sub], cosf, sinf) * K_SCALE
                vb = rows(v2, b)[:, sub].astype(BF16)
                state = s_ref[b, h]
                scores = lax.dot_general(qr.astype(BF16), kr.astype(BF16), (((1,), (1,)), ((), ())),
                                         preferred_element_type=F32)
                s_ref[b, h] = math.exp(lg * CHUNK) * state + _dot_t((kr * k_decay).astype(BF16), vb)
                part[b, h] = (jnp.concatenate([(scores * dmask).astype(BF16),
                                               (qr * q_decay).astype(BF16)], axis=-1),
                              jnp.concatenate([vb, state.astype(BF16)], axis=0))

    def stage_b(pair):
        for i in range(2):
            h = 2 * pair + i
            for b in batches:
                lhs, rhs = part.pop((b, h))
                outs[b, h] = _dot(lhs, rhs)

    def pool_windows(u_tiles):
        pooled = []
        for g, w in enumerate(POOL_WINDOWS):
            cols = slice(g * POOL_GROUP, (g + 1) * POOL_GROUP)
            per_batch = []
            for b in batches:
                u = rows(u_tiles[g], b)
                ext = jnp.concatenate([hist_ref[b, :, cols], u], axis=0)
                win_sum = ext
                shift = 1
                while shift < w:
                    win_sum = win_sum + pltpu.roll(win_sum, shift, 0)
                    shift *= 2
                per_batch.append(win_sum[HIST:] / float(w) - u)
                hist_ref[b, :, cols] = u[CHUNK - HIST:]
            pooled.append(jnp.concatenate(per_batch, axis=0))
        return pooled

    step = pl.program_id(0) * pl.num_programs(1) + c
    so_step = so_ref.at[step]
    _sample_state_update(qkv_ref, sstate_ref, sout_ref, so_step, range(0, SAMPLE_HEADS_FIRST))
    block_rows = pl.ds(pl.multiple_of(step * sstate_ref.shape[0] // SUBLANES * SUBLANES, SUBLANES),
                       SUBLANES)
    _sample_pool_update(us_ref, pin_ref, pout_ref, pooled_ref.at[block_rows])
    issue_qkv(0)
    issue_qkv(1)
    stage_a(0)
    issue_qkv(2)
    stage_a(1)
    stage_b(0)
    issue_qkv(3)
    u_tiles = [proj(OFF_U + g * POOL_GROUP) for g in range(len(POOL_WINDOWS))]
    stage_a(2)
    stage_b(1)
    gp_tiles = [proj(OFF_GP + g * POOL_GROUP) for g in range(len(POOL_WINDOWS))]
    mixed = pool_windows(u_tiles)
    stage_a(3)
    stage_b(2)
    gr_tiles = [proj(OFF_GR + pair * TILE_N) for pair in range(N_HEADS // 2)]
    stage_b(3)
    pool_mix = jnp.concatenate(
        [(mixed[g] * pscale_ref[:, g * POOL_GROUP:(g + 1) * POOL_GROUP] * _silu(gp_tiles[g])).astype(BF16)
         for g in range(len(POOL_WINDOWS))], axis=-1)

    per_piece = max(nb // OUT_PIECES, 1)
    for first in range(0, nb, per_piece):
        piece = range(first, first + per_piece)
        ret_y = []
        for b in piece:
            heads = []
            for h in range(N_HEADS):
                sub = slice((h % 2) * HEAD_DIM, (h % 2 + 1) * HEAD_DIM)
                o = outs.pop((b, h))
                rn = o * lax.rsqrt(jnp.mean(o * o, axis=-1, keepdims=True) + EPS)
                gate = _silu(rows(gr_tiles[h // 2], b)[:, sub])
                heads.append((rn * rnw_ref[:, _head(h)] * gate).astype(BF16))
            ret_y.append(jnp.concatenate(heads, axis=-1))
        mix = jnp.concatenate([pool_mix[first * CHUNK:(first + per_piece) * CHUNK],
                               jnp.concatenate(ret_y, axis=0)], axis=-1)
        x_piece = x_ref[first:first + per_piece].reshape(per_piece * CHUNK, D_MODEL)
        acc_piece = x_piece + _dot(mix, _unpack_rows(wout_ref[...]))
        y_ref[first:first + per_piece] = _rms(acc_piece, fnw_ref[...]).reshape(
            per_piece, CHUNK, D_MODEL)

    _sample_state_update(qkv_ref, sstate_ref, sout_ref, so_step, range(SAMPLE_HEADS_FIRST, N_HEADS))

    @pl.when(step == pl.num_programs(0) * pl.num_programs(1) - 1)
    def _():
        _sample_tail(rest_ref, so_ref, pooled_ref, pscale_ref, rnw_ref, fnw_ref, wpool_ref,
                     wout_ref, ys_ref, o_ref)

    @pl.when(c == pl.num_programs(1) - 1)
    def _():
        last = jnp.concatenate([rows(hb, b)[CHUNK - HIST:] for b in batches], axis=0)
        u_last = _dot(last, _unpack_rows(wu_ref[...]))
        for group in range(pbuf_ref.shape[1] // nb):
            @pl.when(pl.program_id(0) == group)
            def _():
                for b in batches:
                    seq = group * nb + b
                    for r in range(HIST - POOL_BUF, HIST):
                        pbuf_ref[r - (HIST - POOL_BUF), seq:seq + 1, :] = (
                            u_last[b * HIST + r:b * HIST + r + 1, :])


def _main_call(x, cosf, sinf, smeta, umeta, normw, pscale, rnw, fnw, win_bf, wu_bf, wout_bf,
               qkv_s, state_s, rest_s, pool_t, wpool_bf):
    batch, seq, _ = x.shape
    nb = PROMPT_BLOCK
    n_chunks = seq // CHUNK
    n_sample = state_s.shape[0]
    per = n_sample // ((batch // nb) * n_chunks)

    def const(shape):
        zeros = (0,) * len(shape)
        return pl.BlockSpec(shape, lambda b, c: zeros, pipeline_mode=pl.Buffered(1))

    step = lambda b, c: b * n_chunks + c
    pool_block = lambda b, c: step(b, c) * per // SUBLANES
    return pl.pallas_call(
        _main_kernel,
        grid=(batch // nb, n_chunks),
        in_specs=[
            pl.BlockSpec((nb, CHUNK, D_MODEL), lambda b, c: (b, c, 0)),
            pl.BlockSpec((CHUNK, HEAD_DIM), lambda b, c: (c, 0)),
            pl.BlockSpec((CHUNK, HEAD_DIM), lambda b, c: (c, 0)),
            const((N_HEADS, HEAD_DIM, HEAD_DIM)),
            const((N_META, D_POOL)),
            const((1, D_MODEL)),
            const((1, D_POOL)),
            const((1, D_RET)),
            const((1, D_MODEL)),
            const((D_MODEL // 2, D_IN_PROJ)),
            const((D_MODEL // 2, D_POOL)),
            const((D_MIX // 2, D_MODEL)),
            pl.BlockSpec((None, SUBLANES, 3 * D_RET), lambda b, c: (step(b, c), 0, 0)),
            pl.BlockSpec((per, N_HEADS, HEAD_DIM, HEAD_DIM), lambda b, c: (step(b, c), 0, 0, 0)),
            pl.BlockSpec((SUBLANES, D_POOL), lambda b, c: (pool_block(b, c), 0)),
            pl.BlockSpec((POOL_BUF, SUBLANES, D_POOL), lambda b, c: (0, pool_block(b, c), 0)),
            const((n_sample, D_REST)),
            const((len(POOL_WINDOWS), POOL_GROUP // 2, POOL_GROUP)),
        ],
        out_specs=[
            pl.BlockSpec((nb, CHUNK, D_MODEL), lambda b, c: (b, c, 0)),
            pl.BlockSpec((nb, N_HEADS, HEAD_DIM, HEAD_DIM), lambda b, c: (b, 0, 0, 0)),
            pl.BlockSpec((POOL_BUF, batch, D_POOL), lambda b, c: (0, 0, 0)),
            pl.BlockSpec((per, N_HEADS, HEAD_DIM, HEAD_DIM), lambda b, c: (step(b, c), 0, 0, 0)),
            pl.BlockSpec((POOL_BUF, SUBLANES, D_POOL), lambda b, c: (0, pool_block(b, c), 0)),
            pl.BlockSpec((n_sample, 1, D_MODEL), lambda b, c: (0, 0, 0)),
        ],
        out_shape=(jax.ShapeDtypeStruct((batch, seq, D_MODEL), F32),
                   jax.ShapeDtypeStruct((batch, N_HEADS, HEAD_DIM, HEAD_DIM), F32),
                   jax.ShapeDtypeStruct((POOL_BUF, batch, D_POOL), F32),
                   jax.ShapeDtypeStruct((n_sample, N_HEADS, HEAD_DIM, HEAD_DIM), F32),
                   jax.ShapeDtypeStruct((POOL_BUF, n_sample, D_POOL), F32),
                   jax.ShapeDtypeStruct((n_sample, 1, D_MODEL), F32)),
        scratch_shapes=[pltpu.VMEM((nb, HIST, D_POOL), F32),
                        pltpu.VMEM((qkv_s.shape[0], SUBLANES, D_RET), F32),
                        pltpu.VMEM((n_sample, D_POOL), F32),
                        pltpu.VMEM((n_sample, D_RET), F32)],
        compiler_params=pltpu.CompilerParams(
            dimension_semantics=("arbitrary", "arbitrary"), vmem_limit_bytes=VMEM_LIMIT_BYTES),
        name="main",
    )(x, cosf, sinf, smeta, umeta, normw, pscale, rnw, fnw, win_bf, wu_bf, wout_bf,
      qkv_s, state_s, rest_s, pool_t, rest_s, wpool_bf)


def _sample_tail(rest_ref, so_ref, pooled_ref, pscale_ref, rnw_ref, fnw_ref, wpool_ref,
                 wout_ref, y_ref, o_ref):
    n_steps = so_ref.shape[0]
    per = rest_ref.shape[0] // n_steps
    for i in range(n_steps):
        o_ref[i * per:(i + 1) * per, :] = so_ref[i, 0:per, :]

    parts = []
    for g in range(len(POOL_WINDOWS)):
        cols = slice(g * POOL_GROUP, (g + 1) * POOL_GROUP)
        mixed = _dot(pooled_ref[:, cols].astype(BF16), _unpack_rows(wpool_ref[g])) * pscale_ref[:, cols]
        gate = _silu(rest_ref[:, D_POOL + g * POOL_GROUP:D_POOL + (g + 1) * POOL_GROUP])
        parts.append((mixed * gate).astype(BF16))
    for h in range(N_HEADS):
        o = o_ref[:, _head(h)] + rest_ref[:, 2 * D_POOL + D_RET + h * HEAD_DIM:
                                          2 * D_POOL + D_RET + (h + 1) * HEAD_DIM]
        rn = o * lax.rsqrt(jnp.mean(o * o, axis=-1, keepdims=True) + EPS)
        gr = rest_ref[:, 2 * D_POOL + h * HEAD_DIM:2 * D_POOL + (h + 1) * HEAD_DIM]
        parts.append((rn * rnw_ref[:, _head(h)] * _silu(gr)).astype(BF16))
    mix = jnp.concatenate(parts, axis=-1)
    x = rest_ref[:, OFF_REST_X:OFF_REST_X + D_MODEL]
    y_ref[:, 0, :] = _rms(x + _dot(mix, _unpack_rows(wout_ref[...])), fnw_ref[...])


def _rotary_tables(pos):
    half = HEAD_DIM // 2
    inv = ROPE_BASE ** (-np.arange(half, dtype=np.float64) / half)
    ang = np.asarray(pos, np.float64)[:, None] * inv[None, :]
    cos, sin = np.cos(ang), np.sin(ang)
    return (jnp.asarray(np.concatenate([cos, cos], axis=-1), F32),
            jnp.asarray(np.concatenate([-sin, sin], axis=-1), F32))


def kernel(x_prompt, x_sample, state_ret, state_pool, meta_tokens, norm_w, w_in, w_pool,
           pool_scale, ret_norm_w, w_out, final_norm_w):
    assert norm_w.shape[0] == 1, "single-layer stack"
    batch, seq, _ = x_prompt.shape
    n_sample = x_sample.shape[0]
    n_steps = (batch // PROMPT_BLOCK) * (seq // CHUNK)
    per = n_sample // n_steps
    assert per * n_steps == n_sample and SUBLANES % per == 0
    normw, pscale, rnw = norm_w, pool_scale, ret_norm_w
    fnw = final_norm_w[None, :]

    cos_p, sin_p = _rotary_tables(np.arange(N_META + seq))
    cos_s, sin_s = _rotary_tables(PAST_LEN + np.arange(1))

    win_bf, wu_bf, wout_bf, wpool_bf, smeta, umeta, qkv_s, rest_s = _prep_call(
        meta_tokens.astype(x_prompt.dtype), x_sample, cos_p[:N_META], sin_p[:N_META], cos_s, sin_s,
        normw, w_in[0], w_out[0], w_pool[0], n_steps)
    y_p, s_p, buf_p, s_s, buf_s, y_s = _main_call(
        x_prompt, cos_p[N_META:], sin_p[N_META:], smeta, umeta, normw, pscale, rnw, fnw,
        win_bf, wu_bf, wout_bf, qkv_s, state_ret[0], rest_s, jnp.transpose(state_pool[0], (1, 0, 2)),
        wpool_bf)
    return (y_p, y_s, s_p[None], s_s[None], jnp.transpose(buf_p, (1, 0, 2))[None],
            jnp.transpose(buf_s, (1, 0, 2))[None])
```
